```python
import jax, jax.numpy as jnp
from jax import lax
import numpy as np

D_MODEL = 1024
BATCH = 4
SEQ = 8192
DEPTH = 2

GRID_W = 64
CTX_LEN = 256
EPS = 1e-6
ROPE_BASE = 10000.0
NEG_INF = -1e30

D_MIX = D_MODEL
GROUP_W = D_MIX // 4
MLA_HEADS = 4
MLA_NOPE = 64
MLA_ROPE = 32
MLA_V = GROUP_W // MLA_HEADS
MLA_Q_RANK = D_MODEL // 4
MLA_KV_RANK = D_MODEL // 8
Q_BLOCK = 128
FN_W = GROUP_W
FN_GROUPS = 4
NA_HEADS = 4
NA_HD = GROUP_W // NA_HEADS
NA_WIN_H = 8
NA_WIN_W = 16
NA_QCOL = 16
NA_KCOL = NA_QCOL + NA_WIN_W
CV_W = GROUP_W
CV_K = 31
D_FF = 4 * D_MODEL

IN_WIDTHS = (MLA_Q_RANK, MLA_KV_RANK, MLA_ROPE, GROUP_W, GROUP_W, GROUP_W, FN_W, 2 * CV_W)
D_IN = MLA_Q_RANK + MLA_KV_RANK + MLA_ROPE + 3 * GROUP_W + FN_W + 2 * CV_W

kernel_name = "hybrid_mla_fnet_natten_conformer_dit"


def rmsnorm(x, g):
    xf = x.astype(jnp.float32)
    y = xf * lax.rsqrt(jnp.mean(xf * xf, axis=-1, keepdims=True) + EPS)
    return (y * g.astype(jnp.float32)).astype(x.dtype)


def layernorm(x, g, b):
    xf = x.astype(jnp.float32)
    mu = jnp.mean(xf, axis=-1, keepdims=True)
    var = jnp.mean(jnp.square(xf - mu), axis=-1, keepdims=True)
    y = (xf - mu) * lax.rsqrt(var + EPS)
    return (y * g.astype(jnp.float32) + b.astype(jnp.float32)).astype(x.dtype)


def modulate(h, shift, scale):
    return h * (1 + scale[:, None, :]) + shift[:, None, :]


def axial_rope(n, dim):
    t = jnp.arange(n)
    row = (t // GRID_W).astype(jnp.float32)
    col = (t % GRID_W).astype(jnp.float32)
    per_axis = dim // 2
    inv = ROPE_BASE ** (-jnp.arange(0, per_axis, 2, dtype=jnp.float32) / per_axis)
    ang = jnp.concatenate([row[:, None] * inv, col[:, None] * inv], axis=-1)
    return jnp.cos(ang), jnp.sin(ang)


def apply_rope(x, cos, sin):
    half = x.shape[-1] // 2
    xf = x.astype(jnp.float32)
    x1, x2 = xf[..., :half], xf[..., half:]
    return jnp.concatenate([x1 * cos - x2 * sin, x1 * sin + x2 * cos], axis=-1).astype(x.dtype)


def attend(q, k, v):
    s = jnp.einsum('bqhd,bkhd->bhqk', q, k).astype(jnp.float32) * (q.shape[-1] ** -0.5)
    w = jax.nn.softmax(s, axis=-1).astype(v.dtype)
    return jnp.einsum('bhqk,bkhd->bqhd', w, v)


def mla_q(q_a, q_norm, w_uq, rope):
    B, L, _ = q_a.shape
    q = (rmsnorm(q_a, q_norm) @ w_uq).reshape(B, L, MLA_HEADS, MLA_NOPE + MLA_ROPE)
    if rope is not None:
        cos, sin = rope
        q = jnp.concatenate([q[..., :MLA_NOPE], apply_rope(q[..., MLA_NOPE:], cos[:, None], sin[:, None])], axis=-1)
    return q


def mla_kv(kv_a, k_r, kv_norm, w_ukv, rope):
    B, L, _ = kv_a.shape
    kv = (rmsnorm(kv_a, kv_norm) @ w_ukv).reshape(B, L, MLA_HEADS, MLA_NOPE + MLA_V)
    if rope is not None:
        cos, sin = rope
        k_r = apply_rope(k_r, cos, sin)
    k_r = jnp.broadcast_to(k_r[:, :, None, :], (B, L, MLA_HEADS, MLA_ROPE))
    k = jnp.concatenate([kv[..., :MLA_NOPE], k_r], axis=-1)
    return k, kv[..., MLA_NOPE:]


def mla_latent(q, k, v, kc, vc):
    B, S, H, d = q.shape
    k_all = jnp.concatenate([k, kc], axis=1)
    v_all = jnp.concatenate([v, vc], axis=1)
    qb = q.reshape(B, S // Q_BLOCK, Q_BLOCK, H, d).transpose(1, 0, 2, 3, 4)
    ob = lax.map(lambda qi: attend(qi, k_all, v_all), qb)
    return ob.transpose(1, 0, 2, 3, 4).reshape(B, S, H * MLA_V)


def natten_cols():
    n_cb = GRID_W // NA_QCOL
    qcol = np.arange(GRID_W).reshape(n_cb, NA_QCOL)
    win_start = np.clip(qcol - NA_WIN_W // 2, 0, GRID_W - NA_WIN_W)
    blk_start = np.clip(np.arange(n_cb) * NA_QCOL - NA_WIN_W // 2, 0, GRID_W - NA_KCOL)
    kcol = blk_start[:, None] + np.arange(NA_KCOL)
    ws = win_start[:, :, None]
    in_win = (kcol[:, None, :] >= ws) & (kcol[:, None, :] < ws + NA_WIN_W)
    rel_idx = np.clip(kcol[:, None, :] - qcol[:, :, None] + NA_WIN_W - 1, 0, 2 * NA_WIN_W - 2)
    return kcol, in_win, rel_idx


def natten_latent(q, k, v, kc, vc, rpb):
    B, S, H, d = q.shape
    rows = S // GRID_W
    wh = min(NA_WIN_H, rows)
    n_cb = GRID_W // NA_QCOL
    kcol, in_win, rel_idx = natten_cols()
    qg = q.reshape(B, rows, n_cb, NA_QCOL, H, d)
    kg = k.reshape(B, rows, GRID_W, H, d)[:, :, kcol]
    vg = v.reshape(B, rows, GRID_W, H, d)[:, :, kcol]
    rpb = rpb.astype(jnp.float32)
    scale = d ** -0.5
    n_loc = wh * NA_KCOL

    def row_block(r):
        rs = jnp.clip(r - wh // 2, 0, rows - wh)
        kb = lax.dynamic_slice_in_dim(kg, rs, wh, axis=1)
        vb = lax.dynamic_slice_in_dim(vg, rs, wh, axis=1)
        qr = lax.dynamic_index_in_dim(qg, r, axis=1, keepdims=False)
        dr_idx = rs + jnp.arange(wh) - r + (NA_WIN_H - 1)
        bias = rpb[:, dr_idx][:, :, rel_idx].transpose(0, 2, 3, 1, 4)
        bias = jnp.where(in_win[:, :, None, :], bias, NEG_INF)
        s_loc = jnp.einsum('bcqhd,bwckhd->bhcqwk', qr, kb).astype(jnp.float32) * scale + bias[None]
        s_ctx = jnp.einsum('bcqhd,bkhd->bhcqk', qr, kc).astype(jnp.float32) * scale
        s = jnp.concatenate([s_loc.reshape(B, H, n_cb, NA_QCOL, n_loc), s_ctx], axis=-1)
        w = jax.nn.softmax(s, axis=-1).astype(v.dtype)
        w_loc = w[..., :n_loc].reshape(B, H, n_cb, NA_QCOL, wh, NA_KCOL)
        return (jnp.einsum('bhcqwk,bwckhd->bcqhd', w_loc, vb)
                + jnp.einsum('bhcqk,bkhd->bcqhd', w[..., n_loc:], vc))

    o = lax.map(row_block, jnp.arange(rows))
    return o.transpose(1, 0, 2, 3, 4, 5).reshape(B, S, H * d)


def heads(u, n_heads):
    B, L, W = u.shape
    return u.reshape(B, L, n_heads, W // n_heads)


def fourier_mix(u):
    B, L, W = u.shape
    ug = u.reshape(B, L, FN_GROUPS, W // FN_GROUPS).astype(jnp.float32)
    f = jnp.fft.fft2(ug, axes=(1, 3), norm='ortho').real
    return f.reshape(B, L, W).astype(u.dtype)


def conformer_conv(u, w_dw, b_dw, ln_g, ln_b):
    a, g = jnp.split(u, 2, axis=-1)
    y = a * jax.nn.sigmoid(g)
    y = lax.conv_general_dilated(y, w_dw[:, None, :], window_strides=(1,),
                                 padding=[(CV_K // 2, CV_K // 2)],
                                 dimension_numbers=('NWC', 'WIO', 'NWC'),
                                 feature_group_count=CV_W) + b_dw
    y = layernorm(y, ln_g, ln_b)
    return jax.nn.silu(y)


def sq_relu_mlp(h, w1, w2):
    return jnp.square(jax.nn.relu(h @ w1)) @ w2


def split_cols(p):
    return jnp.split(p, np.cumsum(IN_WIDTHS)[:-1].tolist(), axis=-1)


def trunk_layer(x, xc, mod_x, mod_c, rope, prm, last):
    (w_in, q_norm, w_uq, kv_norm, w_ukv, rpb, cv_w, cv_b, cv_g, cv_beta,
     w_out, g1, g2, w1, w2) = prm
    sh1, sc1, gt1, sh2, sc2, gt2 = jnp.split(mod_x, 6, axis=-1)
    csh1, csc1, cgt1, csh2, csc2, cgt2 = jnp.split(mod_c, 6, axis=-1)

    h = modulate(rmsnorm(x, g1), sh1, sc1)
    hc = modulate(rmsnorm(xc, g1), csh1, csc1)
    q_a, kv_a, k_r, na_q, na_k, na_v, fn_u, cv_u = split_cols(h @ w_in)
    cq_a, ckv_a, ck_r, cna_q, cna_k, cna_v, cfn_u, ccv_u = split_cols(hc @ w_in)

    mk_c, mv_c = mla_kv(ckv_a, ck_r, kv_norm, w_ukv, None)
    nk_c, nv_c = heads(cna_k, NA_HEADS), heads(cna_v, NA_HEADS)

    mq = mla_q(q_a, q_norm, w_uq, rope)
    mk, mv = mla_kv(kv_a, k_r, kv_norm, w_ukv, rope)
    o_mla = mla_latent(mq, mk, mv, mk_c, mv_c)
    o_na = natten_latent(heads(na_q, NA_HEADS), heads(na_k, NA_HEADS), heads(na_v, NA_HEADS), nk_c, nv_c, rpb)
    o_fn = fourier_mix(fn_u)
    o_cv = conformer_conv(cv_u, cv_w, cv_b, cv_g, cv_beta)
    y = jnp.concatenate([o_mla, o_na, o_fn, o_cv], axis=-1) @ w_out
    x = x + gt1[:, None, :] * y
    x = x + gt2[:, None, :] * sq_relu_mlp(modulate(rmsnorm(x, g2), sh2, sc2), w1, w2)

    if not last:
        B, C, _ = hc.shape
        o_mla_c = attend(mla_q(cq_a, q_norm, w_uq, None), mk_c, mv_c).reshape(B, C, MLA_HEADS * MLA_V)
        o_na_c = attend(heads(cna_q, NA_HEADS), nk_c, nv_c).reshape(B, C, GROUP_W)
        yc = jnp.concatenate([o_mla_c, o_na_c, fourier_mix(cfn_u),
                              conformer_conv(ccv_u, cv_w, cv_b, cv_g, cv_beta)], axis=-1) @ w_out
        xc = xc + cgt1[:, None, :] * yc
        xc = xc + cgt2[:, None, :] * sq_relu_mlp(modulate(rmsnorm(xc, g2), csh2, csc2), w1, w2)
    return x, xc


def setup_inputs(seed: int = 0) -> dict:
    key = jax.random.key(seed)
    ks = iter(jax.random.split(key, 32))

    def nrm(shape, scale):
        return jax.random.normal(next(ks), shape, jnp.float32) * scale

    def gain(shape):
        return 1.0 + nrm(shape, 0.05)

    L = DEPTH
    return {
        "x": nrm((BATCH, SEQ, D_MODEL), 1.0),
        "c": nrm((BATCH, D_MODEL), 1.0),
        "ctx": nrm((BATCH, CTX_LEN, D_MODEL), 1.0),
        "c_ctx": nrm((D_MODEL,), 1.0),
        "w_mod": nrm((L, D_MODEL, 6 * D_MODEL), 0.5 * D_MODEL ** -0.5),
        "b_mod": nrm((L, 6 * D_MODEL), 0.01),
        "norm1_g": gain((L, D_MODEL)),
        "norm2_g": gain((L, D_MODEL)),
        "w_in": nrm((L, D_MODEL, D_IN), D_MODEL ** -0.5),
        "mla_q_norm": gain((L, MLA_Q_RANK)),
        "mla_w_uq": nrm((L, MLA_Q_RANK, MLA_HEADS * (MLA_NOPE + MLA_ROPE)), MLA_Q_RANK ** -0.5),
        "mla_kv_norm": gain((L, MLA_KV_RANK)),
        "mla_w_ukv": nrm((L, MLA_KV_RANK, MLA_HEADS * (MLA_NOPE + MLA_V)), MLA_KV_RANK ** -0.5),
        "na_rpb": nrm((L, NA_HEADS, 2 * NA_WIN_H - 1, 2 * NA_WIN_W - 1), 0.5),
        "cv_w_dw": nrm((L, CV_K, CV_W), CV_K ** -0.5),
        "cv_b_dw": nrm((L, CV_W), 0.01),
        "cv_ln_g": gain((L, CV_W)),
        "cv_ln_b": nrm((L, CV_W), 0.01),
        "w_out": nrm((L, D_MIX, D_MODEL), D_MIX ** -0.5),
        "w_ff1": nrm((L, D_MODEL, D_FF), D_MODEL ** -0.5),
        "w_ff2": nrm((L, D_FF, D_MODEL), D_FF ** -0.5),
        "final_g": gain((D_MODEL,)),
    }


def reference(x, c, ctx, c_ctx, w_mod, b_mod, norm1_g, norm2_g, w_in, mla_q_norm, mla_w_uq,
              mla_kv_norm, mla_w_ukv, na_rpb, cv_w_dw, cv_b_dw, cv_ln_g, cv_ln_b, w_out,
              w_ff1, w_ff2, final_g):
    rope = axial_rope(x.shape[1], MLA_ROPE)
    xc = ctx
    for i in range(DEPTH):
        mod_x = jax.nn.silu(c) @ w_mod[i] + b_mod[i]
        mod_c = jax.nn.silu(c_ctx)[None, :] @ w_mod[i] + b_mod[i]
        prm = (w_in[i], mla_q_norm[i], mla_w_uq[i], mla_kv_norm[i], mla_w_ukv[i], na_rpb[i],
               cv_w_dw[i], cv_b_dw[i], cv_ln_g[i], cv_ln_b[i], w_out[i],
               norm1_g[i], norm2_g[i], w_ff1[i], w_ff2[i])
        x, xc = trunk_layer(x, xc, mod_x, mod_c, rope, prm, i == DEPTH - 1)
    return rmsnorm(x, final_g)
```

```python
import functools

import jax
import jax.numpy as jnp
import numpy as np
from jax import lax
from jax.experimental import pallas as pl
from jax.experimental.pallas import tpu as pltpu

F32 = jnp.float32
BF16 = jnp.bfloat16

EPS = 1e-6
ROPE_BASE = 10000.0
NEG_INF = -1e30

D_MODEL = 1024
GRID_W = 64
N_HEADS = 4
GROUP_W = 256
HEAD_SLOT = 128
MLA_NOPE = 64
MLA_ROPE = 32
MLA_QK = MLA_NOPE + MLA_ROPE
MLA_V = 64
MLA_W = N_HEADS * HEAD_SLOT
NA_HD = 64
NA_WIN_H = 8
NA_WIN_W = 16
FN_GROUPS = 4
FN_GW = GROUP_W // FN_GROUPS
CV_K = 31
CV_HALO = 16
D_FF = 4 * D_MODEL
D_IN_PAD = 2048
DFT_N2 = 128
V7X_VMEM_LIMIT = 56 * 1024 * 1024


def _cparams(n_grid):
    return pltpu.CompilerParams(dimension_semantics=("parallel",) * n_grid,
                                vmem_limit_bytes=V7X_VMEM_LIMIT)


def _const_spec(shape):
    nd = len(shape)
    return pl.BlockSpec(shape, lambda *_: (0,) * nd, pipeline_mode=pl.Buffered(1))


def _sigmoid(v):
    return 1.0 / (1.0 + jnp.exp(-v))


def _rms(v, g):
    return v * lax.rsqrt(jnp.mean(v * v, axis=-1, keepdims=True) + EPS) * g


def _dot(a, b):
    return jnp.dot(a, b, preferred_element_type=F32)


def _dot_nt(a, b):
    return lax.dot_general(a, b, (((1,), (1,)), ((), ())), preferred_element_type=F32)


def _mod_kernel(c_ref, w_ref, b_ref, o_ref):
    c = c_ref[...]
    s = (c * _sigmoid(c)).astype(BF16)
    o_ref[0] = _dot(s, w_ref[0].astype(BF16)) + b_ref[0]


def _mod_call(cc, w_mod, b_mod):
    depth, d, n = w_mod.shape
    tn = 1536
    return pl.pallas_call(
        _mod_kernel,
        grid=(depth, n // tn),
        in_specs=[pl.BlockSpec(cc.shape, lambda l, j: (0, 0)),
                  pl.BlockSpec((1, d, tn), lambda l, j: (l, 0, j)),
                  pl.BlockSpec((1, 1, tn), lambda l, j: (l, 0, j))],
        out_specs=pl.BlockSpec((1, cc.shape[0], tn), lambda l, j: (l, 0, j)),
        out_shape=jax.ShapeDtypeStruct((depth, cc.shape[0], n), F32),
        compiler_params=_cparams(2),
        name="mod",
    )(cc, w_mod, b_mod.reshape(depth, 1, n))


def _inproj_kernel(x_ref, sh_ref, sc_ref, g1_ref, win_ref, qn_ref, wq_ref, kvn_ref, wkv_ref,
                   dc_ref, cs_ref,
                   q_ref, k_ref, v_ref, nq_ref, nk_ref, nv_ref, z_ref, cy_ref):
    x = x_ref[0]
    h = _rms(x, g1_ref[...]) * (1.0 + sc_ref[0]) + sh_ref[0]
    p = _dot(h.astype(BF16), win_ref[...])

    cs = cs_ref[...]
    cos_t, sin_t = cs[:, :HEAD_SLOT], cs[:, HEAD_SLOT:]
    cos4 = jnp.concatenate([cos_t] * N_HEADS, axis=-1)
    sin4 = jnp.concatenate([sin_t] * N_HEADS, axis=-1)

    r = _rms(p[:, 0:256], qn_ref[...]).astype(BF16)
    qq = _dot(r, wq_ref[...])
    q = (qq[:, :MLA_W] * cos4 + qq[:, MLA_W:] * sin4) * (MLA_QK ** -0.5)
    q_ref[0] = q.astype(BF16)

    kvr = p[:, 256:512]
    kvn = _rms(kvr[:, :128], kvn_ref[...])
    comb = jnp.concatenate([kvn, kvr[:, 128:]], axis=-1).astype(BF16)
    kk = _dot(comb, wkv_ref[...])
    k = kk[:, :MLA_W] * cos4 + kk[:, MLA_W:2 * MLA_W] * sin4
    k_ref[0] = k.astype(BF16)
    lane = lax.broadcasted_iota(jnp.int32, (1, MLA_W), 1)
    ones_col = jnp.where(lane % HEAD_SLOT == MLA_V, 1.0, 0.0).astype(F32)
    v_ref[0] = (kk[:, 2 * MLA_W:] + ones_col).astype(BF16)

    nq_ref[0] = (p[:, 512:768] * (NA_HD ** -0.5)).astype(BF16)
    nk_ref[0] = p[:, 768:1024].astype(BF16)
    nv_ref[0] = p[:, 1024:1280].astype(BF16)

    z_ref[0] = _dot(p[:, 1280:1536].astype(BF16), dc_ref[...]).astype(BF16)

    cy_ref[0] = p[:, 1536:1792] * _sigmoid(p[:, 1792:2048])


def _inproj_call(x, sh, sc, g1, win, qn, wq, kvn, wkv, dc, cs):
    b, l, d = x.shape
    t = min(l, 512)
    tok = lambda w: pl.BlockSpec((1, t, w), lambda bi, i: (bi, i, 0))
    vec = pl.BlockSpec((1, 1, d), lambda bi, i: (bi, 0, 0))
    outs = [(MLA_W, BF16), (MLA_W, BF16), (MLA_W, BF16), (GROUP_W, BF16), (GROUP_W, BF16),
            (GROUP_W, BF16), (2 * GROUP_W, BF16), (GROUP_W, F32)]
    return pl.pallas_call(
        _inproj_kernel,
        grid=(b, l // t),
        in_specs=[tok(d), vec, vec, _const_spec(g1.shape), _const_spec(win.shape),
                  _const_spec(qn.shape), _const_spec(wq.shape), _const_spec(kvn.shape),
                  _const_spec(wkv.shape), _const_spec(dc.shape),
                  pl.BlockSpec((t, 2 * HEAD_SLOT), lambda bi, i: (i, 0))],
        out_specs=[tok(w) for w, _ in outs],
        out_shape=[jax.ShapeDtypeStruct((b, l, w), dt) for w, dt in outs],
        compiler_params=_cparams(2),
        name="inproj",
    )(x, sh, sc, g1, win, qn, wq, kvn, wkv, dc, cs)


def _flash_kernel(*refs, n_chunks, tk, has_extra):
    if has_extra:
        q_ref, k_ref, v_ref, k2_ref, v2_ref, o_ref = refs
    else:
        q_ref, k_ref, v_ref, o_ref = refs
    q = q_ref[0]
    tq = q.shape[0]

    def step(kc, vc, carry):
        m, acc = carry
        s = _dot_nt(q, kc)
        m_new = jnp.maximum(m, jnp.max(s, axis=-1, keepdims=True))
        p = jnp.exp(s - m_new).astype(BF16)
        acc = jnp.exp(m - m_new) * acc + _dot(p, vc)
        return m_new, acc

    def body(c, carry):
        start = pl.multiple_of(c * tk, tk)
        return step(k_ref[0, pl.ds(start, tk), :], v_ref[0, pl.ds(start, tk), :], carry)

    carry = (jnp.full((tq, 1), -jnp.inf, F32), jnp.zeros((tq, HEAD_SLOT), F32))
    carry = lax.fori_loop(0, n_chunks, body, carry)
    if has_extra:
        carry = step(k2_ref[0], v2_ref[0], carry)
    _, acc = carry
    lane = lax.broadcasted_iota(jnp.int32, acc.shape, 1)
    denom = jnp.sum(jnp.where(lane == MLA_V, acc, 0.0), axis=-1, keepdims=True)
    o_ref[0] = (acc / denom).astype(BF16)


def _flash_call(q, k, v, k2=None, v2=None):
    b, lq, _ = q.shape
    lk = k.shape[1]
    tq = min(lq, 512)
    tk = min(lk, 1024)
    has_extra = k2 is not None
    qspec = pl.BlockSpec((1, tq, HEAD_SLOT), lambda bi, h, i: (bi, i, h))
    kvspec = lambda n: pl.BlockSpec((1, n, HEAD_SLOT), lambda bi, h, i: (bi, 0, h))
    in_specs = [qspec, kvspec(lk), kvspec(lk)]
    args = [q, k, v]
    if has_extra:
        in_specs += [kvspec(k2.shape[1]), kvspec(k2.shape[1])]
        args += [k2, v2]
    return pl.pallas_call(
        functools.partial(_flash_kernel, n_chunks=lk // tk, tk=tk, has_extra=has_extra),
        grid=(b, N_HEADS, lq // tq),
        in_specs=in_specs,
        out_specs=qspec,
        out_shape=jax.ShapeDtypeStruct((b, lq, MLA_W), BF16),
        compiler_params=_cparams(3),
        name="mla_attn",
    )(*args)


def _head_stack(q):
    lane = lax.broadcasted_iota(jnp.int32, q.shape, 1)
    return jnp.concatenate(
        [jnp.where(lane // NA_HD == h, q, jnp.zeros_like(q)) for h in range(N_HEADS)], axis=0)


def _head_unstack(o, n):
    lane = lax.broadcasted_iota(jnp.int32, (n, GROUP_W), 1)
    out = jnp.zeros((n, GROUP_W), F32)
    for h in range(N_HEADS):
        out = out + jnp.where(lane // NA_HD == h, o[h * n:(h + 1) * n], 0.0)
    return out


def _natten_kernel(q_ref, k_ref, v_ref, kc_ref, vc_ref, bias_ref, o_ref, *, rows_per_step, rows):
    blk = pl.program_id(1)
    kc = kc_ref[0]
    vc = vc_ref[0]
    n_loc = NA_WIN_H * GRID_W

    def body(j, _):
        r = blk * rows_per_step + j
        rs = jnp.clip(r - NA_WIN_H // 2, 0, rows - NA_WIN_H)
        d0 = rs - r + (NA_WIN_H - 1)
        qs = _head_stack(q_ref[0, pl.ds(pl.multiple_of(j * GRID_W, GRID_W), GRID_W), :])
        kstart = pl.multiple_of(rs * GRID_W, GRID_W)
        s_loc = _dot_nt(qs, k_ref[0, pl.ds(kstart, n_loc), :])
        bias = jnp.concatenate(
            [jnp.concatenate([bias_ref[h, d0 + 2 * w] for w in range(NA_WIN_H // 2)], axis=-1)
             for h in range(N_HEADS)], axis=0)
        s_loc = s_loc + bias
        s_ctx = _dot_nt(qs, kc)
        m = jnp.maximum(jnp.max(s_loc, axis=-1, keepdims=True),
                        jnp.max(s_ctx, axis=-1, keepdims=True))
        p_loc = jnp.exp(s_loc - m)
        p_ctx = jnp.exp(s_ctx - m)
        denom = jnp.sum(p_loc, axis=-1, keepdims=True) + jnp.sum(p_ctx, axis=-1, keepdims=True)
        o = (_dot(p_loc.astype(BF16), v_ref[0, pl.ds(kstart, n_loc), :])
             + _dot(p_ctx.astype(BF16), vc))
        o = _head_unstack(o / denom, GRID_W)
        o_ref[0, pl.ds(pl.multiple_of(j * GRID_W, GRID_W), GRID_W), :] = o.astype(BF16)
        return 0

    lax.fori_loop(0, rows_per_step, body, 0)


def _natten_call(q, k, v, kc, vc, bias):
    b, l, w = q.shape
    rows = l // GRID_W
    rows_per_step = 8
    t = rows_per_step * GRID_W
    full = lambda n: pl.BlockSpec((1, n, w), lambda bi, i: (bi, 0, 0))
    return pl.pallas_call(
        functools.partial(_natten_kernel, rows_per_step=rows_per_step, rows=rows),
        grid=(b, rows // rows_per_step),
        in_specs=[pl.BlockSpec((1, t, w), lambda bi, i: (bi, i, 0)),
                  full(l), full(l), full(kc.shape[1]), full(kc.shape[1]),
                  _const_spec(bias.shape)],
        out_specs=pl.BlockSpec((1, t, w), lambda bi, i: (bi, i, 0)),
        out_shape=jax.ShapeDtypeStruct((b, l, w), BF16),
        compiler_params=_cparams(2),
        name="natten",
    )(q, k, v, kc, vc, bias)


def _na_ctx_kernel(q_ref, k_ref, v_ref, o_ref):
    n = q_ref.shape[1]
    s = _dot_nt(_head_stack(q_ref[0]), k_ref[0])
    m = jnp.max(s, axis=-1, keepdims=True)
    p = jnp.exp(s - m)
    denom = jnp.sum(p, axis=-1, keepdims=True)
    o = _dot(p.astype(BF16), v_ref[0]) / denom
    o_ref[0] = _head_unstack(o, n).astype(BF16)


def _na_ctx_call(q, k, v):
    b, n, w = q.shape
    spec = pl.BlockSpec((1, n, w), lambda bi: (bi, 0, 0))
    return pl.pallas_call(
        _na_ctx_kernel, grid=(b,), in_specs=[spec, spec, spec], out_specs=spec,
        out_shape=jax.ShapeDtypeStruct((b, n, w), BF16),
        compiler_params=_cparams(1), name="na_ctx",
    )(q, k, v)


def _fnet1_kernel(z_ref, w_ref, y_ref, *, n_inner):
    p = _dot(w_ref[...], z_ref[0])
    n1 = p.shape[0] // 2
    for j in range(n_inner):
        zr_c = p[:n1, j * 512:j * 512 + 256]
        zi_c = p[:n1, j * 512 + 256:(j + 1) * 512]
        zr_s = p[n1:, j * 512:j * 512 + 256]
        zi_s = p[n1:, j * 512 + 256:(j + 1) * 512]
        y_ref[0, 0, :, j * 256:(j + 1) * 256] = (zr_c + zi_s).astype(BF16)
        y_ref[0, 1, :, j * 256:(j + 1) * 256] = (zi_c - zr_s).astype(BF16)


def _fnet1_call(z2, w1s, n1):
    b = z2.shape[0]
    n_inner = 8
    return pl.pallas_call(
        functools.partial(_fnet1_kernel, n_inner=n_inner),
        grid=(b, DFT_N2 // n_inner),
        in_specs=[pl.BlockSpec((1, n1, n_inner * 512), lambda bi, i: (bi, 0, i)),
                  _const_spec(w1s.shape)],
        out_specs=pl.BlockSpec((1, 2, n1, n_inner * 256), lambda bi, i: (bi, 0, 0, i)),
        out_shape=jax.ShapeDtypeStruct((b, 2, n1, DFT_N2 * GROUP_W), BF16),
        compiler_params=_cparams(2),
        name="fnet_stage1",
    )(z2, w1s)


def _fnet2_kernel(y_ref, f_ref, o_ref, *, n_inner, norm):
    for j in range(n_inner):
        f = f_ref[j]
        o = _dot(f[:, :DFT_N2], y_ref[0, 0, j]) + _dot(f[:, DFT_N2:], y_ref[0, 1, j])
        o_ref[0, :, j * GROUP_W:(j + 1) * GROUP_W] = (o * norm).astype(BF16)


def _fnet2_call(y5, ftab, n1, norm):
    b = y5.shape[0]
    n_inner = 8
    return pl.pallas_call(
        functools.partial(_fnet2_kernel, n_inner=n_inner, norm=norm),
        grid=(b, n1 // n_inner),
        in_specs=[pl.BlockSpec((1, 2, n_inner, DFT_N2, GROUP_W), lambda bi, i: (bi, 0, i, 0, 0)),
                  pl.BlockSpec((n_inner, DFT_N2, 2 * DFT_N2), lambda bi, i: (i, 0, 0))],
        out_specs=pl.BlockSpec((1, DFT_N2, n_inner * GROUP_W), lambda bi, i: (bi, 0, i)),
        out_shape=jax.ShapeDtypeStruct((b, DFT_N2, n1 * GROUP_W), BF16),
        compiler_params=_cparams(2),
        name="fnet_stage2",
    )(y5, ftab)


def _dft_small_kernel(z_ref, f_ref, o_ref, *, norm):
    z = z_ref[0]
    f = f_ref[...]
    n = z.shape[0]
    o = _dot(f[:, :n], z[:, :GROUP_W]) + _dot(f[:, n:], z[:, GROUP_W:])
    o_ref[0] = (o * norm).astype(BF16)


def _dft_small_call(z, ftab, norm):
    b, n, _ = z.shape
    return pl.pallas_call(
        functools.partial(_dft_small_kernel, norm=norm),
        grid=(b,),
        in_specs=[pl.BlockSpec((1, n, 2 * GROUP_W), lambda bi: (bi, 0, 0)), _const_spec(ftab.shape)],
        out_specs=pl.BlockSpec((1, n, GROUP_W), lambda bi: (bi, 0, 0)),
        out_shape=jax.ShapeDtypeStruct((b, n, GROUP_W), BF16),
        compiler_params=_cparams(1), name="fnet_ctx",
    )(z, ftab)


def _fourier_tables(l):
    if l <= 256:
        m = (np.arange(l)[:, None] * np.arange(l)[None, :]) % l
        ang = 2.0 * np.pi * m / l
        return None, np.concatenate([np.cos(ang), np.sin(ang)], axis=1).astype(np.float32), 0
    n1 = l // DFT_N2
    m1 = (np.arange(n1)[:, None] * np.arange(n1)[None, :]) % n1
    a1 = 2.0 * np.pi * m1 / n1
    w1s = np.concatenate([np.cos(a1), np.sin(a1)], axis=0).astype(np.float32)
    kk = np.arange(n1)[:, None, None] + n1 * np.arange(DFT_N2)[None, :, None]
    m2 = (kk * np.arange(DFT_N2)[None, None, :]) % l
    a2 = 2.0 * np.pi * m2 / l
    ftab = np.concatenate([np.cos(a2), np.sin(a2)], axis=2).astype(np.float32)
    return w1s, ftab, n1


def _fourier_mix(z):
    b, l, _ = z.shape
    norm = float((l * FN_GW) ** -0.5)
    w1s, ftab, n1 = _fourier_tables(l)
    if w1s is None:
        return _dft_small_call(z, jnp.asarray(ftab).astype(BF16), norm)
    y = _fnet1_call(z.reshape(b, n1, DFT_N2 * 2 * GROUP_W), jnp.asarray(w1s).astype(BF16), n1)
    o = _fnet2_call(y.reshape(b, 2, n1, DFT_N2, GROUP_W), jnp.asarray(ftab).astype(BF16), n1, norm)
    return o.reshape(b, l, GROUP_W)


def _conv_kernel(prev_ref, cur_ref, next_ref, w_ref, b_ref, g_ref, beta_ref, o_ref, buf_ref):
    i = pl.program_id(1)
    n = pl.num_programs(1)
    t = cur_ref.shape[1]
    buf_ref[0:CV_HALO, :] = jnp.where(i > 0, prev_ref[0], 0.0)
    buf_ref[CV_HALO:CV_HALO + t, :] = cur_ref[0]
    buf_ref[CV_HALO + t:, :] = jnp.where(i < n - 1, next_ref[0], 0.0)
    w = w_ref[...]
    acc = jnp.zeros((t, GROUP_W), F32)
    for j in range(CV_K):
        off = CV_HALO - CV_K // 2 + j
        acc = acc + w[j:j + 1, :] * buf_ref[off:off + t, :]
    y = acc + b_ref[...]
    mu = jnp.mean(y, axis=-1, keepdims=True)
    var = jnp.mean(jnp.square(y - mu), axis=-1, keepdims=True)
    y = (y - mu) * lax.rsqrt(var + EPS) * g_ref[...] + beta_ref[...]
    o_ref[0] = (y * _sigmoid(y)).astype(BF16)


def _conv_call(cy, w_dw, b_dw, ln_g, ln_b):
    b, l, w = cy.shape
    t = min(l, 512)
    hb = t // CV_HALO
    n_halo = l // CV_HALO
    return pl.pallas_call(
        _conv_kernel,
        grid=(b, l // t),
        in_specs=[pl.BlockSpec((1, CV_HALO, w), lambda bi, i: (bi, jnp.maximum(i * hb - 1, 0), 0)),
                  pl.BlockSpec((1, t, w), lambda bi, i: (bi, i, 0)),
                  pl.BlockSpec((1, CV_HALO, w),
                               lambda bi, i: (bi, jnp.minimum((i + 1) * hb, n_halo - 1), 0)),
                  _const_spec(w_dw.shape), _const_spec(b_dw.shape), _const_spec(ln_g.shape),
                  _const_spec(ln_b.shape)],
        out_specs=pl.BlockSpec((1, t, w), lambda bi, i: (bi, i, 0)),
        out_shape=jax.ShapeDtypeStruct((b, l, w), BF16),
        scratch_shapes=[pltpu.VMEM((t + 2 * CV_HALO, w), F32)],
        compiler_params=_cparams(2),
        name="conformer_conv",
    )(cy, cy, cy, w_dw, b_dw, ln_g, ln_b)


def _outmlp_kernel(x_ref, oa_ref, oc_ref, ob_ref, od_ref, wo_ref, gt1_ref, sh2_ref, sc2_ref,
                   gt2_ref, g2_ref, w1_ref, w2_ref, fg_ref, o_ref, *, last, ff_chunk):
    y = (_dot(oa_ref[0], wo_ref[0:MLA_W, :])
         + _dot(oc_ref[0], wo_ref[MLA_W:MLA_W + GROUP_W, :])
         + _dot(ob_ref[0], wo_ref[MLA_W + GROUP_W:MLA_W + 2 * GROUP_W, :])
         + _dot(od_ref[0], wo_ref[MLA_W + 2 * GROUP_W:, :]))
    x1 = x_ref[0] + gt1_ref[0] * y
    h = (_rms(x1, g2_ref[...]) * (1.0 + sc2_ref[0]) + sh2_ref[0]).astype(BF16)
    ff = jnp.zeros(x1.shape, F32)
    for c in range(D_FF // ff_chunk):
        u = jnp.maximum(_dot(h, w1_ref[:, c * ff_chunk:(c + 1) * ff_chunk]), 0.0)
        ff = ff + _dot((u * u).astype(BF16), w2_ref[c * ff_chunk:(c + 1) * ff_chunk, :])
    x2 = x1 + gt2_ref[0] * ff
    if last:
        x2 = _rms(x2, fg_ref[...])
    o_ref[0] = x2


def _outmlp_call(x, o_mla, o_na, o_fn, o_cv, wo, gt1, sh2, sc2, gt2, g2, w1, w2, fg, last):
    b, l, d = x.shape
    t = min(l, 512)
    tok = lambda w: pl.BlockSpec((1, t, w), lambda bi, i: (bi, i, 0))
    vec = pl.BlockSpec((1, 1, d), lambda bi, i: (bi, 0, 0))
    return pl.pallas_call(
        functools.partial(_outmlp_kernel, last=last, ff_chunk=1024),
        grid=(b, l // t),
        in_specs=[tok(d), tok(MLA_W), tok(GROUP_W), tok(GROUP_W), tok(GROUP_W),
                  _const_spec(wo.shape), vec, vec, vec, vec, _const_spec(g2.shape),
                  _const_spec(w1.shape), _const_spec(w2.shape), _const_spec(fg.shape)],
        out_specs=tok(d),
        out_shape=jax.ShapeDtypeStruct((b, l, d), F32),
        compiler_params=_cparams(2),
        name="outproj_mlp",
    )(x, o_mla, o_na, o_fn, o_cv, wo, gt1, sh2, sc2, gt2, g2, w1, w2, fg)


def _rope_tables(l, with_rope):
    ones = jnp.ones((l, MLA_NOPE), F32)
    zeros = jnp.zeros((l, MLA_NOPE), F32)
    pad = jnp.zeros((l, HEAD_SLOT - MLA_QK), F32)
    if with_rope:
        t = jnp.arange(l)
        row = (t // GRID_W).astype(F32)
        col = (t % GRID_W).astype(F32)
        per_axis = MLA_ROPE // 2
        inv = ROPE_BASE ** (-jnp.arange(0, per_axis, 2, dtype=F32) / per_axis)
        ang = jnp.concatenate([row[:, None] * inv, col[:, None] * inv], axis=-1)
        cos, sin = jnp.cos(ang), jnp.sin(ang)
    else:
        cos = jnp.ones((l, MLA_ROPE // 2), F32)
        sin = jnp.zeros((l, MLA_ROPE // 2), F32)
    return jnp.concatenate([ones, cos, cos, pad, zeros, -sin, sin, pad], axis=-1)


def _swap_halves(w):
    half = w.shape[-1] // 2
    return jnp.concatenate([w[..., half:], w[..., :half]], axis=-1)


def _layer_weights(w_in, w_uq, w_ukv, w_out, rpb):
    d = w_in.shape[0]
    k_r_end = 256 + 128 + MLA_ROPE
    win = jnp.concatenate([w_in[:, :k_r_end], jnp.zeros((d, HEAD_SLOT - MLA_ROPE), F32),
                           w_in[:, k_r_end:]], axis=1).astype(BF16)

    rq = w_uq.shape[0]
    w3 = w_uq.reshape(rq, N_HEADS, MLA_QK)
    zpad = jnp.zeros((rq, N_HEADS, HEAD_SLOT - MLA_QK), F32)
    plain = jnp.concatenate([w3, zpad], axis=-1)
    swapped = jnp.concatenate([jnp.zeros((rq, N_HEADS, MLA_NOPE), F32),
                               _swap_halves(w3[..., MLA_NOPE:]), zpad], axis=-1)
    wq = jnp.concatenate([plain.reshape(rq, MLA_W), swapped.reshape(rq, MLA_W)], axis=1).astype(BF16)

    rkv = w_ukv.shape[0]
    u3 = w_ukv.reshape(rkv, N_HEADS, MLA_NOPE + MLA_V)
    eye = jnp.eye(MLA_ROPE, dtype=F32)
    slot_pad_r = jnp.zeros((MLA_ROPE, N_HEADS, HEAD_SLOT - MLA_QK), F32)

    def slots(top, rope_block):
        top = jnp.concatenate([top, jnp.zeros((rkv, N_HEADS, HEAD_SLOT - top.shape[-1]), F32)], axis=-1)
        mid = jnp.concatenate([jnp.zeros((MLA_ROPE, N_HEADS, MLA_NOPE), F32),
                               jnp.broadcast_to(rope_block[:, None, :], (MLA_ROPE, N_HEADS, MLA_ROPE)),
                               slot_pad_r], axis=-1)
        bot = jnp.zeros((2 * HEAD_SLOT - rkv - MLA_ROPE, N_HEADS, HEAD_SLOT), F32)
        return jnp.concatenate([top, mid, bot], axis=0).reshape(2 * HEAD_SLOT, MLA_W)

    zero_rope = jnp.zeros((MLA_ROPE, MLA_ROPE), F32)
    k_plain = slots(u3[..., :MLA_NOPE], eye)
    k_swap = slots(jnp.zeros((rkv, N_HEADS, MLA_NOPE), F32), _swap_halves(eye))
    v_cols = slots(u3[..., MLA_NOPE:], zero_rope)
    wkv = jnp.concatenate([k_plain, k_swap, v_cols], axis=1).astype(BF16)

    o3 = w_out[:N_HEADS * MLA_V].reshape(N_HEADS, MLA_V, d)
    o3 = jnp.concatenate([o3, jnp.zeros((N_HEADS, HEAD_SLOT - MLA_V, d), F32)], axis=1)
    wo = jnp.concatenate([o3.reshape(MLA_W, d), w_out[N_HEADS * MLA_V:]], axis=0).astype(BF16)

    qc = np.arange(GRID_W)[:, None]
    kc = np.arange(GRID_W)[None, :]
    ws = np.clip(qc - NA_WIN_W // 2, 0, GRID_W - NA_WIN_W)
    in_win = (kc >= ws) & (kc < ws + NA_WIN_W)
    rel = np.clip(kc - qc + NA_WIN_W - 1, 0, 2 * NA_WIN_W - 2)
    t2 = jnp.where(in_win[None, None], rpb.astype(F32)[:, :, rel], NEG_INF)
    bias = jnp.concatenate([t2[:, :-1], t2[:, 1:]], axis=-1)
    return win, wq, wkv, wo, bias


def _channel_dft():
    m = (np.arange(FN_GW)[:, None] * np.arange(FN_GW)[None, :]) % FN_GW
    ang = 2.0 * np.pi * m / FN_GW
    eye = np.eye(FN_GROUPS)
    return np.concatenate([np.kron(eye, np.cos(ang)), -np.kron(eye, np.sin(ang))],
                          axis=1).astype(np.float32)


def kernel(x, c, ctx, c_ctx, w_mod, b_mod, norm1_g, norm2_g, w_in, mla_q_norm, mla_w_uq, mla_kv_norm, mla_w_ukv, na_rpb, cv_w_dw, cv_b_dw, cv_ln_g, cv_ln_b, w_out, w_ff1, w_ff2, final_g):
    depth = w_mod.shape[0]
    b, s, d = x.shape
    n_ctx = ctx.shape[1]

    cc = jnp.concatenate([c, c_ctx[None, :], jnp.zeros((8 - b - 1, d), F32)], axis=0)
    mods = _mod_call(cc, w_mod, b_mod)

    cs_lat = _rope_tables(s, True)
    cs_ctx = _rope_tables(n_ctx, False)
    dc = jnp.asarray(_channel_dft()).astype(BF16)
    row = lambda p: p.reshape(1, -1)
    fg = row(final_g)

    xc = ctx
    for i in range(depth):
        last = i == depth - 1
        win, wq, wkv, wo, bias = _layer_weights(w_in[i], mla_w_uq[i], mla_w_ukv[i], w_out[i], na_rpb[i])
        w1 = w_ff1[i].astype(BF16)
        w2 = w_ff2[i].astype(BF16)
        mx = [m.reshape(b, 1, d) for m in jnp.split(mods[i, :b], 6, axis=-1)]
        mc = [jnp.broadcast_to(m.reshape(1, 1, d), (b, 1, d))
              for m in jnp.split(mods[i, b:b + 1], 6, axis=-1)]
        proj = functools.partial(_inproj_call, g1=row(norm1_g[i]), win=win, qn=row(mla_q_norm[i]),
                                 wq=wq, kvn=row(mla_kv_norm[i]), wkv=wkv, dc=dc)
        conv = functools.partial(_conv_call, w_dw=cv_w_dw[i], b_dw=row(cv_b_dw[i]),
                                 ln_g=row(cv_ln_g[i]), ln_b=row(cv_ln_b[i]))
        mlp = functools.partial(_outmlp_call, wo=wo, g2=row(norm2_g[i]), w1=w1, w2=w2, fg=fg)

        q, k, v, nq, nk, nv, z, cy = proj(x, mx[0], mx[1], cs=cs_lat)
        cq, ck, cv, cnq, cnk, cnv, cz, ccy = proj(xc, mc[0], mc[1], cs=cs_ctx)

        o_mla = _flash_call(q, k, v, ck, cv)
        o_na = _natten_call(nq, nk, nv, cnk, cnv, bias)
        o_fn = _fourier_mix(z)
        o_cv = conv(cy)
        x = mlp(x, o_mla, o_na, o_fn, o_cv, gt1=mx[2], sh2=mx[3], sc2=mx[4], gt2=mx[5], last=last)

        if not last:
            co_mla = _flash_call(cq, ck, cv)
            co_na = _na_ctx_call(cnq, cnk, cnv)
            co_fn = _fourier_mix(cz)
            co_cv = conv(ccy)
            xc = mlp(xc, co_mla, co_na, co_fn, co_cv, gt1=mc[2], sh2=mc[3], sc2=mc[4], gt2=mc[5],
                     last=False)
    return x
```

```python
import functools

import jax
import jax.numpy as jnp
import numpy as np
from jax import lax
from jax.experimental import pallas as pl
from jax.experimental.pallas import tpu as pltpu

F32 = jnp.float32
BF16 = jnp.bfloat16

EPS = 1e-6
ROPE_BASE = 10000.0
NEG_INF = -1e30

D_MODEL = 1024
GRID_W = 64
N_HEADS = 4
GROUP_W = 256
HEAD_SLOT = 128
MLA_NOPE = 64
MLA_ROPE = 32
MLA_QK = MLA_NOPE + MLA_ROPE
MLA_V = 64
MLA_W = N_HEADS * HEAD_SLOT
VT_ROWS = 80
LOG2_E = 1.4426950408889634
NA_HD = 64
NA_WIN_H = 8
NA_WIN_W = 16
FN_GROUPS = 4
FN_GW = GROUP_W // FN_GROUPS
CV_K = 31
CV_HALO = 16
D_FF = 4 * D_MODEL
D_IN_PAD = 2048
DFT_N2 = 128
V7X_VMEM_LIMIT = 56 * 1024 * 1024


def _cparams(n_grid):
    return pltpu.CompilerParams(dimension_semantics=("parallel",) * n_grid,
                                vmem_limit_bytes=V7X_VMEM_LIMIT)


def _const_spec(shape):
    nd = len(shape)
    return pl.BlockSpec(shape, lambda *_: (0,) * nd, pipeline_mode=pl.Buffered(1))


def _sigmoid(v):
    return 1.0 / (1.0 + jnp.exp(-v))


def _rms(v, g):
    return v * lax.rsqrt(jnp.mean(v * v, axis=-1, keepdims=True) + EPS) * g


def _dot(a, b):
    return jnp.dot(a, b, preferred_element_type=F32)


def _dot_nt(a, b):
    return lax.dot_general(a, b, (((1,), (1,)), ((), ())), preferred_element_type=F32)


def _mod_kernel(c_ref, w_ref, b_ref, o_ref):
    c = c_ref[...]
    s = (c * _sigmoid(c)).astype(BF16)
    o_ref[0] = _dot(s, w_ref[0].astype(BF16)) + b_ref[0]


def _mod_call(cc, w_mod, b_mod):
    depth, d, n = w_mod.shape
    tn = 1536
    return pl.pallas_call(
        _mod_kernel,
        grid=(depth, n // tn),
        in_specs=[pl.BlockSpec(cc.shape, lambda l, j: (0, 0)),
                  pl.BlockSpec((1, d, tn), lambda l, j: (l, 0, j)),
                  pl.BlockSpec((1, 1, tn), lambda l, j: (l, 0, j))],
        out_specs=pl.BlockSpec((1, cc.shape[0], tn), lambda l, j: (l, 0, j)),
        out_shape=jax.ShapeDtypeStruct((depth, cc.shape[0], n), F32),
        compiler_params=_cparams(2),
        name="mod",
    )(cc, w_mod, b_mod.reshape(depth, 1, n))


def _inproj_kernel(x_ref, sh_ref, sc_ref, g1_ref, win_ref, qn_ref, wq_ref, kvn_ref, wkv_ref,
                   wvt_ref, dc_ref, cs_ref,
                   q_ref, k_ref, vt_ref, nq_ref, nk_ref, nv_ref, z_ref, cy_ref):
    x = x_ref[0]
    h = _rms(x, g1_ref[...]) * (1.0 + sc_ref[0]) + sh_ref[0]
    p = _dot(h.astype(BF16), win_ref[...])

    cs = cs_ref[...]
    cos_t, sin_t = cs[:, :HEAD_SLOT], cs[:, HEAD_SLOT:]
    cos4 = jnp.concatenate([cos_t] * N_HEADS, axis=-1)
    sin4 = jnp.concatenate([sin_t] * N_HEADS, axis=-1)

    r = _rms(p[:, 0:256], qn_ref[...]).astype(BF16)
    qq = _dot(r, wq_ref[...])
    q = (qq[:, :MLA_W] * cos4 + qq[:, MLA_W:] * sin4) * (MLA_QK ** -0.5 * LOG2_E)
    q_ref[0] = q.astype(BF16)

    kvr = p[:, 256:512]
    kvn = _rms(kvr[:, :128], kvn_ref[...])
    comb = jnp.concatenate([kvn, kvr[:, 128:]], axis=-1).astype(BF16)
    kk = _dot(comb, wkv_ref[...])
    k = kk[:, :MLA_W] * cos4 + kk[:, MLA_W:] * sin4
    k_ref[0] = k.astype(BF16)
    vt = _dot_nt(wvt_ref[...], comb)
    sub = lax.broadcasted_iota(jnp.int32, (vt.shape[0], 1), 0)
    ones_row = jnp.where(sub % VT_ROWS == MLA_V, 1.0, 0.0).astype(F32)
    vt_ref[0] = (vt + ones_row).astype(BF16)

    nq_ref[0] = (p[:, 512:768] * (NA_HD ** -0.5)).astype(BF16)
    nk_ref[0] = p[:, 768:1024].astype(BF16)
    nv_ref[0] = p[:, 1024:1280].astype(BF16)

    z_ref[0] = _dot(p[:, 1280:1536].astype(BF16), dc_ref[...]).astype(BF16)

    cy_ref[0] = p[:, 1536:1792] * _sigmoid(p[:, 1792:2048])


def _inproj_call(x, sh, sc, g1, win, qn, wq, kvn, wkv, wvt, dc, cs):
    b, l, d = x.shape
    t = min(l, 512)
    tok = lambda w: pl.BlockSpec((1, t, w), lambda bi, i: (bi, i, 0))
    vec = pl.BlockSpec((1, 1, d), lambda bi, i: (bi, 0, 0))
    outs = [(MLA_W, BF16), (MLA_W, BF16), None, (GROUP_W, BF16), (GROUP_W, BF16),
            (GROUP_W, BF16), (2 * GROUP_W, BF16), (GROUP_W, F32)]
    vt_rows = N_HEADS * VT_ROWS
    out_specs = [tok(o[0]) if o else pl.BlockSpec((1, vt_rows, t), lambda bi, i: (bi, 0, i))
                 for o in outs]
    out_shape = [jax.ShapeDtypeStruct((b, l, o[0]), o[1]) if o
                 else jax.ShapeDtypeStruct((b, vt_rows, l), BF16) for o in outs]
    return pl.pallas_call(
        _inproj_kernel,
        grid=(b, l // t),
        in_specs=[tok(d), vec, vec, _const_spec(g1.shape), _const_spec(win.shape),
                  _const_spec(qn.shape), _const_spec(wq.shape), _const_spec(kvn.shape),
                  _const_spec(wkv.shape), _const_spec(wvt.shape), _const_spec(dc.shape),
                  pl.BlockSpec((t, 2 * HEAD_SLOT), lambda bi, i: (i, 0))],
        out_specs=out_specs,
        out_shape=out_shape,
        compiler_params=_cparams(2),
        name="inproj",
    )(x, sh, sc, g1, win, qn, wq, kvn, wkv, wvt, dc, cs)


def _flash_kernel(*refs, n_chunks, tk, has_extra):
    if has_extra:
        q_ref, k_ref, vt_ref, k2_ref, vt2_ref, o_ref, sa_ref, sb_ref, sx_ref = refs
    else:
        q_ref, k_ref, vt_ref, o_ref, sa_ref, sb_ref = refs
    q = q_ref[0]
    tq = q.shape[0]

    def scores(kc, s_ref):
        st = _dot_nt(kc, q)
        s_ref[...] = st
        return jnp.max(st, axis=0, keepdims=True)

    def accumulate(s_ref, mx, vtc, carry):
        m, acc = carry
        m_new = jnp.maximum(m, mx)
        p = jnp.exp2(s_ref[...] - m_new).astype(BF16)
        return m_new, jnp.exp2(m - m_new) * acc + _dot(vtc, p)

    def k_chunk(c):
        return k_ref[0, pl.ds(pl.multiple_of(c * tk, tk), tk), :]

    def vt_chunk(c):
        return vt_ref[0, :, pl.ds(pl.multiple_of(c * tk, tk), tk)]

    def pair(j, carry):
        mx_a, m, acc = carry
        mx_b = scores(k_chunk(2 * j + 1), sb_ref)
        m, acc = accumulate(sa_ref, mx_a, vt_chunk(2 * j), (m, acc))
        mx_a = scores(k_chunk(2 * j + 2), sa_ref)
        m, acc = accumulate(sb_ref, mx_b, vt_chunk(2 * j + 1), (m, acc))
        return mx_a, m, acc

    n_pairs = (n_chunks - 1) // 2
    carry = (scores(k_chunk(0), sa_ref), jnp.full((1, tq), -jnp.inf, F32),
             jnp.zeros((VT_ROWS, tq), F32))
    mx, m, acc = lax.fori_loop(0, n_pairs, pair, carry)
    pending = (sa_ref, mx, vt_chunk(2 * n_pairs))
    tail = [(k_chunk(c), vt_chunk(c), sb_ref if c % 2 else sa_ref)
            for c in range(2 * n_pairs + 1, n_chunks)]
    if has_extra:
        tail.append((k2_ref[0], vt2_ref[0], sx_ref))
    for kc, vtc, s_ref in tail:
        mx_next = scores(kc, s_ref)
        m, acc = accumulate(*pending, (m, acc))
        pending = (s_ref, mx_next, vtc)
    m, acc = accumulate(*pending, (m, acc))

    o_t = acc / acc[MLA_V:MLA_V + 1, :]
    o_t = jnp.concatenate([o_t, jnp.zeros((HEAD_SLOT - VT_ROWS, tq), F32)], axis=0)
    o_ref[0] = o_t.T.astype(BF16)


def _flash_call(q, k, vt, k2=None, vt2=None):
    b, lq, _ = q.shape
    lk = k.shape[1]
    tq = min(lq, 512)
    tk = min(lk, 1024)
    has_extra = k2 is not None
    qspec = pl.BlockSpec((1, tq, HEAD_SLOT), lambda bi, h, i: (bi, i, h))
    kspec = lambda n: pl.BlockSpec((1, n, HEAD_SLOT), lambda bi, h, i: (bi, 0, h))
    vspec = lambda n: pl.BlockSpec((1, VT_ROWS, n), lambda bi, h, i: (bi, h, 0))
    in_specs = [qspec, kspec(lk), vspec(lk)]
    args = [q, k, vt]
    scratch = [pltpu.VMEM((tk, tq), F32), pltpu.VMEM((tk, tq), F32)]
    if has_extra:
        in_specs += [kspec(k2.shape[1]), vspec(k2.shape[1])]
        args += [k2, vt2]
        scratch.append(pltpu.VMEM((k2.shape[1], tq), F32))
    return pl.pallas_call(
        functools.partial(_flash_kernel, n_chunks=lk // tk, tk=tk, has_extra=has_extra),
        grid=(b, N_HEADS, lq // tq),
        in_specs=in_specs,
        out_specs=qspec,
        out_shape=jax.ShapeDtypeStruct((b, lq, MLA_W), BF16),
        scratch_shapes=scratch,
        compiler_params=_cparams(3),
        name="mla_attn",
    )(*args)


def _head_stack(q):
    lane = lax.broadcasted_iota(jnp.int32, q.shape, 1)
    return jnp.concatenate(
        [jnp.where(lane // NA_HD == h, q, jnp.zeros_like(q)) for h in range(N_HEADS)], axis=0)


def _head_unstack(o, n):
    lane = lax.broadcasted_iota(jnp.int32, (n, GROUP_W), 1)
    out = jnp.zeros((n, GROUP_W), F32)
    for h in range(N_HEADS):
        out = out + jnp.where(lane // NA_HD == h, o[h * n:(h + 1) * n], 0.0)
    return out


def _natten_kernel(q_ref, k_ref, v_ref, kc_ref, vc_ref, bias_ref, o_ref, *, rows_per_step, rows):
    blk = pl.program_id(1)
    kc = kc_ref[0]
    vc = vc_ref[0]
    n_loc = NA_WIN_H * GRID_W

    def body(j, _):
        r = blk * rows_per_step + j
        rs = jnp.clip(r - NA_WIN_H // 2, 0, rows - NA_WIN_H)
        d0 = rs - r + (NA_WIN_H - 1)
        qs = _head_stack(q_ref[0, pl.ds(pl.multiple_of(j * GRID_W, GRID_W), GRID_W), :])
        kstart = pl.multiple_of(rs * GRID_W, GRID_W)
        s_loc = _dot_nt(qs, k_ref[0, pl.ds(kstart, n_loc), :])
        bias = jnp.concatenate(
            [jnp.concatenate([bias_ref[h, d0 + 2 * w] for w in range(NA_WIN_H // 2)], axis=-1)
             for h in range(N_HEADS)], axis=0)
        s_loc = s_loc + bias
        s_ctx = _dot_nt(qs, kc)
        m = jnp.maximum(jnp.max(s_loc, axis=-1, keepdims=True),
                        jnp.max(s_ctx, axis=-1, keepdims=True))
        p_loc = jnp.exp(s_loc - m)
        p_ctx = jnp.exp(s_ctx - m)
        denom = jnp.sum(p_loc, axis=-1, keepdims=True) + jnp.sum(p_ctx, axis=-1, keepdims=True)
        o = (_dot(p_loc.astype(BF16), v_ref[0, pl.ds(kstart, n_loc), :])
             + _dot(p_ctx.astype(BF16), vc))
        o = _head_unstack(o / denom, GRID_W)
        o_ref[0, pl.ds(pl.multiple_of(j * GRID_W, GRID_W), GRID_W), :] = o.astype(BF16)
        return 0

    lax.fori_loop(0, rows_per_step, body, 0, unroll=2)


def _natten_call(q, k, v, kc, vc, bias):
    b, l, w = q.shape
    rows = l // GRID_W
    rows_per_step = 8
    t = rows_per_step * GRID_W
    full = lambda n: pl.BlockSpec((1, n, w), lambda bi, i: (bi, 0, 0))
    return pl.pallas_call(
        functools.partial(_natten_kernel, rows_per_step=rows_per_step, rows=rows),
        grid=(b, rows // rows_per_step),
        in_specs=[pl.BlockSpec((1, t, w), lambda bi, i: (bi, i, 0)),
                  full(l), full(l), full(kc.shape[1]), full(kc.shape[1]),
                  _const_spec(bias.shape)],
        out_specs=pl.BlockSpec((1, t, w), lambda bi, i: (bi, i, 0)),
        out_shape=jax.ShapeDtypeStruct((b, l, w), BF16),
        compiler_params=_cparams(2),
        name="natten",
    )(q, k, v, kc, vc, bias)


def _na_ctx_kernel(q_ref, k_ref, v_ref, o_ref):
    n = q_ref.shape[1]
    s = _dot_nt(_head_stack(q_ref[0]), k_ref[0])
    m = jnp.max(s, axis=-1, keepdims=True)
    p = jnp.exp(s - m)
    denom = jnp.sum(p, axis=-1, keepdims=True)
    o = _dot(p.astype(BF16), v_ref[0]) / denom
    o_ref[0] = _head_unstack(o, n).astype(BF16)


def _na_ctx_call(q, k, v):
    b, n, w = q.shape
    spec = pl.BlockSpec((1, n, w), lambda bi: (bi, 0, 0))
    return pl.pallas_call(
        _na_ctx_kernel, grid=(b,), in_specs=[spec, spec, spec], out_specs=spec,
        out_shape=jax.ShapeDtypeStruct((b, n, w), BF16),
        compiler_params=_cparams(1), name="na_ctx",
    )(q, k, v)


def _fnet1_kernel(z_ref, w_ref, y_ref, *, n_inner):
    p = _dot(w_ref[...], z_ref[0])
    n1 = p.shape[0] // 2
    for j in range(n_inner):
        zr_c = p[:n1, j * 512:j * 512 + 256]
        zi_c = p[:n1, j * 512 + 256:(j + 1) * 512]
        zr_s = p[n1:, j * 512:j * 512 + 256]
        zi_s = p[n1:, j * 512 + 256:(j + 1) * 512]
        y_ref[0, 0, :, j * 256:(j + 1) * 256] = (zr_c + zi_s).astype(BF16)
        y_ref[0, 1, :, j * 256:(j + 1) * 256] = (zi_c - zr_s).astype(BF16)


def _fnet1_call(z2, w1s, n1):
    b = z2.shape[0]
    n_inner = 8
    return pl.pallas_call(
        functools.partial(_fnet1_kernel, n_inner=n_inner),
        grid=(b, DFT_N2 // n_inner),
        in_specs=[pl.BlockSpec((1, n1, n_inner * 512), lambda bi, i: (bi, 0, i)),
                  _const_spec(w1s.shape)],
        out_specs=pl.BlockSpec((1, 2, n1, n_inner * 256), lambda bi, i: (bi, 0, 0, i)),
        out_shape=jax.ShapeDtypeStruct((b, 2, n1, DFT_N2 * GROUP_W), BF16),
        compiler_params=_cparams(2),
        name="fnet_stage1",
    )(z2, w1s)


def _fnet2_kernel(y_ref, f_ref, o_ref, *, n_inner, norm):
    for j in range(n_inner):
        f = f_ref[j]
        o = _dot(f[:, :DFT_N2], y_ref[0, 0, j]) + _dot(f[:, DFT_N2:], y_ref[0, 1, j])
        o_ref[0, :, j * GROUP_W:(j + 1) * GROUP_W] = (o * norm).astype(BF16)


def _fnet2_call(y5, ftab, n1, norm):
    b = y5.shape[0]
    n_inner = 8
    return pl.pallas_call(
        functools.partial(_fnet2_kernel, n_inner=n_inner, norm=norm),
        grid=(b, n1 // n_inner),
        in_specs=[pl.BlockSpec((1, 2, n_inner, DFT_N2, GROUP_W), lambda bi, i: (bi, 0, i, 0, 0)),
                  pl.BlockSpec((n_inner, DFT_N2, 2 * DFT_N2), lambda bi, i: (i, 0, 0))],
        out_specs=pl.BlockSpec((1, DFT_N2, n_inner * GROUP_W), lambda bi, i: (bi, 0, i)),
        out_shape=jax.ShapeDtypeStruct((b, DFT_N2, n1 * GROUP_W), BF16),
        compiler_params=_cparams(2),
        name="fnet_stage2",
    )(y5, ftab)


def _dft_small_kernel(z_ref, f_ref, o_ref, *, norm):
    z = z_ref[0]
    f = f_ref[...]
    n = z.shape[0]
    o = _dot(f[:, :n], z[:, :GROUP_W]) + _dot(f[:, n:], z[:, GROUP_W:])
    o_ref[0] = (o * norm).astype(BF16)


def _dft_small_call(z, ftab, norm):
    b, n, _ = z.shape
    return pl.pallas_call(
        functools.partial(_dft_small_kernel, norm=norm),
        grid=(b,),
        in_specs=[pl.BlockSpec((1, n, 2 * GROUP_W), lambda bi: (bi, 0, 0)), _const_spec(ftab.shape)],
        out_specs=pl.BlockSpec((1, n, GROUP_W), lambda bi: (bi, 0, 0)),
        out_shape=jax.ShapeDtypeStruct((b, n, GROUP_W), BF16),
        compiler_params=_cparams(1), name="fnet_ctx",
    )(z, ftab)


def _fourier_tables(l):
    if l <= 256:
        m = (np.arange(l)[:, None] * np.arange(l)[None, :]) % l
        ang = 2.0 * np.pi * m / l
        return None, np.concatenate([np.cos(ang), np.sin(ang)], axis=1).astype(np.float32), 0
    n1 = l // DFT_N2
    m1 = (np.arange(n1)[:, None] * np.arange(n1)[None, :]) % n1
    a1 = 2.0 * np.pi * m1 / n1
    w1s = np.concatenate([np.cos(a1), np.sin(a1)], axis=0).astype(np.float32)
    kk = np.arange(n1)[:, None, None] + n1 * np.arange(DFT_N2)[None, :, None]
    m2 = (kk * np.arange(DFT_N2)[None, None, :]) % l
    a2 = 2.0 * np.pi * m2 / l
    ftab = np.concatenate([np.cos(a2), np.sin(a2)], axis=2).astype(np.float32)
    return w1s, ftab, n1


def _fourier_mix(z):
    b, l, _ = z.shape
    norm = float((l * FN_GW) ** -0.5)
    w1s, ftab, n1 = _fourier_tables(l)
    if w1s is None:
        return _dft_small_call(z, jnp.asarray(ftab).astype(BF16), norm)
    y = _fnet1_call(z.reshape(b, n1, DFT_N2 * 2 * GROUP_W), jnp.asarray(w1s).astype(BF16), n1)
    o = _fnet2_call(y.reshape(b, 2, n1, DFT_N2, GROUP_W), jnp.asarray(ftab).astype(BF16), n1, norm)
    return o.reshape(b, l, GROUP_W)


def _conv_kernel(prev_ref, cur_ref, next_ref, w_ref, b_ref, g_ref, beta_ref, o_ref, buf_ref):
    i = pl.program_id(1)
    n = pl.num_programs(1)
    t = cur_ref.shape[1]
    buf_ref[0:CV_HALO, :] = jnp.where(i > 0, prev_ref[0], 0.0)
    buf_ref[CV_HALO:CV_HALO + t, :] = cur_ref[0]
    buf_ref[CV_HALO + t:, :] = jnp.where(i < n - 1, next_ref[0], 0.0)
    w = w_ref[...]
    acc = jnp.zeros((t, GROUP_W), F32)
    for j in range(CV_K):
        off = CV_HALO - CV_K // 2 + j
        acc = acc + w[j:j + 1, :] * buf_ref[off:off + t, :]
    y = acc + b_ref[...]
    mu = jnp.mean(y, axis=-1, keepdims=True)
    var = jnp.mean(jnp.square(y - mu), axis=-1, keepdims=True)
    y = (y - mu) * lax.rsqrt(var + EPS) * g_ref[...] + beta_ref[...]
    o_ref[0] = (y * _sigmoid(y)).astype(BF16)


def _conv_call(cy, w_dw, b_dw, ln_g, ln_b):
    b, l, w = cy.shape
    t = min(l, 512)
    hb = t // CV_HALO
    n_halo = l // CV_HALO
    return pl.pallas_call(
        _conv_kernel,
        grid=(b, l // t),
        in_specs=[pl.BlockSpec((1, CV_HALO, w), lambda bi, i: (bi, jnp.maximum(i * hb - 1, 0), 0)),
                  pl.BlockSpec((1, t, w), lambda bi, i: (bi, i, 0)),
                  pl.BlockSpec((1, CV_HALO, w),
                               lambda bi, i: (bi, jnp.minimum((i + 1) * hb, n_halo - 1), 0)),
                  _const_spec(w_dw.shape), _const_spec(b_dw.shape), _const_spec(ln_g.shape),
                  _const_spec(ln_b.shape)],
        out_specs=pl.BlockSpec((1, t, w), lambda bi, i: (bi, i, 0)),
        out_shape=jax.ShapeDtypeStruct((b, l, w), BF16),
        scratch_shapes=[pltpu.VMEM((t + 2 * CV_HALO, w), F32)],
        compiler_params=_cparams(2),
        name="conformer_conv",
    )(cy, cy, cy, w_dw, b_dw, ln_g, ln_b)


def _outmlp_kernel(x_ref, oa_ref, oc_ref, ob_ref, od_ref, wo_ref, gt1_ref, sh2_ref, sc2_ref,
                   gt2_ref, g2_ref, w1_ref, w2_ref, fg_ref, o_ref, *, last, ff_chunk):
    y = (_dot(oa_ref[0], wo_ref[0:MLA_W, :])
         + _dot(oc_ref[0], wo_ref[MLA_W:MLA_W + GROUP_W, :])
         + _dot(ob_ref[0], wo_ref[MLA_W + GROUP_W:MLA_W + 2 * GROUP_W, :])
         + _dot(od_ref[0], wo_ref[MLA_W + 2 * GROUP_W:, :]))
    x1 = x_ref[0] + gt1_ref[0] * y
    h = (_rms(x1, g2_ref[...]) * (1.0 + sc2_ref[0]) + sh2_ref[0]).astype(BF16)
    ff = jnp.zeros(x1.shape, F32)
    for c in range(D_FF // ff_chunk):
        u = jnp.maximum(_dot(h, w1_ref[:, c * ff_chunk:(c + 1) * ff_chunk]), 0.0)
        ff = ff + _dot((u * u).astype(BF16), w2_ref[c * ff_chunk:(c + 1) * ff_chunk, :])
    x2 = x1 + gt2_ref[0] * ff
    if last:
        x2 = _rms(x2, fg_ref[...])
    o_ref[0] = x2


def _outmlp_call(x, o_mla, o_na, o_fn, o_cv, wo, gt1, sh2, sc2, gt2, g2, w1, w2, fg, last):
    b, l, d = x.shape
    t = min(l, 512)
    tok = lambda w: pl.BlockSpec((1, t, w), lambda bi, i: (bi, i, 0))
    vec = pl.BlockSpec((1, 1, d), lambda bi, i: (bi, 0, 0))
    return pl.pallas_call(
        functools.partial(_outmlp_kernel, last=last, ff_chunk=1024),
        grid=(b, l // t),
        in_specs=[tok(d), tok(MLA_W), tok(GROUP_W), tok(GROUP_W), tok(GROUP_W),
                  _const_spec(wo.shape), vec, vec, vec, vec, _const_spec(g2.shape),
                  _const_spec(w1.shape), _const_spec(w2.shape), _const_spec(fg.shape)],
        out_specs=tok(d),
        out_shape=jax.ShapeDtypeStruct((b, l, d), F32),
        compiler_params=_cparams(2),
        name="outproj_mlp",
    )(x, o_mla, o_na, o_fn, o_cv, wo, gt1, sh2, sc2, gt2, g2, w1, w2, fg)


def _rope_tables(l, with_rope):
    ones = jnp.ones((l, MLA_NOPE), F32)
    zeros = jnp.zeros((l, MLA_NOPE), F32)
    pad = jnp.zeros((l, HEAD_SLOT - MLA_QK), F32)
    if with_rope:
        t = jnp.arange(l)
        row = (t // GRID_W).astype(F32)
        col = (t % GRID_W).astype(F32)
        per_axis = MLA_ROPE // 2
        inv = ROPE_BASE ** (-jnp.arange(0, per_axis, 2, dtype=F32) / per_axis)
        ang = jnp.concatenate([row[:, None] * inv, col[:, None] * inv], axis=-1)
        cos, sin = jnp.cos(ang), jnp.sin(ang)
    else:
        cos = jnp.ones((l, MLA_ROPE // 2), F32)
        sin = jnp.zeros((l, MLA_ROPE // 2), F32)
    return jnp.concatenate([ones, cos, cos, pad, zeros, -sin, sin, pad], axis=-1)


def _swap_halves(w):
    half = w.shape[-1] // 2
    return jnp.concatenate([w[..., half:], w[..., :half]], axis=-1)


def _layer_weights(w_in, w_uq, w_ukv, w_out, rpb):
    d = w_in.shape[0]
    k_r_end = 256 + 128 + MLA_ROPE
    win = jnp.concatenate([w_in[:, :k_r_end], jnp.zeros((d, HEAD_SLOT - MLA_ROPE), F32),
                           w_in[:, k_r_end:]], axis=1).astype(BF16)

    rq = w_uq.shape[0]
    w3 = w_uq.reshape(rq, N_HEADS, MLA_QK)
    zpad = jnp.zeros((rq, N_HEADS, HEAD_SLOT - MLA_QK), F32)
    plain = jnp.concatenate([w3, zpad], axis=-1)
    swapped = jnp.concatenate([jnp.zeros((rq, N_HEADS, MLA_NOPE), F32),
                               _swap_halves(w3[..., MLA_NOPE:]), zpad], axis=-1)
    wq = jnp.concatenate([plain.reshape(rq, MLA_W), swapped.reshape(rq, MLA_W)], axis=1).astype(BF16)

    rkv = w_ukv.shape[0]
    u3 = w_ukv.reshape(rkv, N_HEADS, MLA_NOPE + MLA_V)
    eye = jnp.eye(MLA_ROPE, dtype=F32)
    slot_pad_r = jnp.zeros((MLA_ROPE, N_HEADS, HEAD_SLOT - MLA_QK), F32)

    def slots(top, rope_block):
        top = jnp.concatenate([top, jnp.zeros((rkv, N_HEADS, HEAD_SLOT - top.shape[-1]), F32)], axis=-1)
        mid = jnp.concatenate([jnp.zeros((MLA_ROPE, N_HEADS, MLA_NOPE), F32),
                               jnp.broadcast_to(rope_block[:, None, :], (MLA_ROPE, N_HEADS, MLA_ROPE)),
                               slot_pad_r], axis=-1)
        bot = jnp.zeros((2 * HEAD_SLOT - rkv - MLA_ROPE, N_HEADS, HEAD_SLOT), F32)
        return jnp.concatenate([top, mid, bot], axis=0).reshape(2 * HEAD_SLOT, MLA_W)

    k_plain = slots(u3[..., :MLA_NOPE], eye)
    k_swap = slots(jnp.zeros((rkv, N_HEADS, MLA_NOPE), F32), _swap_halves(eye))
    wkv = jnp.concatenate([k_plain, k_swap], axis=1).astype(BF16)

    vt3 = jnp.transpose(u3[..., MLA_NOPE:], (1, 2, 0))
    vt3 = jnp.pad(vt3, ((0, 0), (0, VT_ROWS - MLA_V), (0, 2 * HEAD_SLOT - rkv)))
    wvt = vt3.reshape(N_HEADS * VT_ROWS, 2 * HEAD_SLOT).astype(BF16)

    o3 = w_out[:N_HEADS * MLA_V].reshape(N_HEADS, MLA_V, d)
    o3 = jnp.concatenate([o3, jnp.zeros((N_HEADS, HEAD_SLOT - MLA_V, d), F32)], axis=1)
    wo = jnp.concatenate([o3.reshape(MLA_W, d), w_out[N_HEADS * MLA_V:]], axis=0).astype(BF16)

    qc = np.arange(GRID_W)[:, None]
    kc = np.arange(GRID_W)[None, :]
    ws = np.clip(qc - NA_WIN_W // 2, 0, GRID_W - NA_WIN_W)
    in_win = (kc >= ws) & (kc < ws + NA_WIN_W)
    rel = np.clip(kc - qc + NA_WIN_W - 1, 0, 2 * NA_WIN_W - 2)
    t2 = jnp.where(in_win[None, None], rpb.astype(F32)[:, :, rel], NEG_INF)
    bias = jnp.concatenate([t2[:, :-1], t2[:, 1:]], axis=-1)
    return win, wq, wkv, wvt, wo, bias


def _channel_dft():
    m = (np.arange(FN_GW)[:, None] * np.arange(FN_GW)[None, :]) % FN_GW
    ang = 2.0 * np.pi * m / FN_GW
    eye = np.eye(FN_GROUPS)
    return np.concatenate([np.kron(eye, np.cos(ang)), -np.kron(eye, np.sin(ang))],
                          axis=1).astype(np.float32)


def kernel(x, c, ctx, c_ctx, w_mod, b_mod, norm1_g, norm2_g, w_in, mla_q_norm, mla_w_uq, mla_kv_norm, mla_w_ukv, na_rpb, cv_w_dw, cv_b_dw, cv_ln_g, cv_ln_b, w_out, w_ff1, w_ff2, final_g):
    depth = w_mod.shape[0]
    b, s, d = x.shape
    n_ctx = ctx.shape[1]

    cc = jnp.concatenate([c, c_ctx[None, :], jnp.zeros((8 - b - 1, d), F32)], axis=0)
    mods = _mod_call(cc, w_mod, b_mod)

    cs_lat = _rope_tables(s, True)
    cs_ctx = _rope_tables(n_ctx, False)
    dc = jnp.asarray(_channel_dft()).astype(BF16)
    row = lambda p: p.reshape(1, -1)
    fg = row(final_g)

    xc = ctx
    for i in range(depth):
        last = i == depth - 1
        win, wq, wkv, wvt, wo, bias = _layer_weights(w_in[i], mla_w_uq[i], mla_w_ukv[i], w_out[i], na_rpb[i])
        w1 = w_ff1[i].astype(BF16)
        w2 = w_ff2[i].astype(BF16)
        mx = [m.reshape(b, 1, d) for m in jnp.split(mods[i, :b], 6, axis=-1)]
        mc = [jnp.broadcast_to(m.reshape(1, 1, d), (b, 1, d))
              for m in jnp.split(mods[i, b:b + 1], 6, axis=-1)]
        proj = functools.partial(_inproj_call, g1=row(norm1_g[i]), win=win, qn=row(mla_q_norm[i]),
                                 wq=wq, kvn=row(mla_kv_norm[i]), wkv=wkv, wvt=wvt, dc=dc)
        conv = functools.partial(_conv_call, w_dw=cv_w_dw[i], b_dw=row(cv_b_dw[i]),
                                 ln_g=row(cv_ln_g[i]), ln_b=row(cv_ln_b[i]))
        mlp = functools.partial(_outmlp_call, wo=wo, g2=row(norm2_g[i]), w1=w1, w2=w2, fg=fg)

        q, k, v, nq, nk, nv, z, cy = proj(x, mx[0], mx[1], cs=cs_lat)
        cq, ck, cv, cnq, cnk, cnv, cz, ccy = proj(xc, mc[0], mc[1], cs=cs_ctx)

        o_mla = _flash_call(q, k, v, ck, cv)
        o_na = _natten_call(nq, nk, nv, cnk, cnv, bias)
        o_fn = _fourier_mix(z)
        o_cv = conv(cy)
        x = mlp(x, o_mla, o_na, o_fn, o_cv, gt1=mx[2], sh2=mx[3], sc2=mx[4], gt2=mx[5], last=last)

        if not last:
            co_mla = _flash_call(cq, ck, cv)
            co_na = _na_ctx_call(cnq, cnk, cnv)
            co_fn = _fourier_mix(cz)
            co_cv = conv(ccy)
            xc = mlp(xc, co_mla, co_na, co_fn, co_cv, gt1=mc[2], sh2=mc[3], sc2=mc[4], gt2=mc[5],
                     last=False)
    return x
```

```python
import functools

import jax
import jax.numpy as jnp
import numpy as np
from jax import lax
from jax.experimental import pallas as pl
from jax.experimental.pallas import tpu as pltpu

F32 = jnp.float32
BF16 = jnp.bfloat16

EPS = 1e-6
ROPE_BASE = 10000.0
NEG_INF = -1e30

D_MODEL = 1024
GRID_W = 64
N_HEADS = 4
GROUP_W = 256
HEAD_SLOT = 128
MLA_NOPE = 64
MLA_ROPE = 32
MLA_QK = MLA_NOPE + MLA_ROPE
MLA_V = 64
MLA_W = N_HEADS * HEAD_SLOT
VT_ROWS = 80
LOG2_E = 1.4426950408889634
NA_HD = 64
NA_WIN_H = 8
NA_WIN_W = 16
FN_GROUPS = 4
FN_GW = GROUP_W // FN_GROUPS
CV_K = 31
CV_HALO = 16
SUBLANES = 8
D_FF = 4 * D_MODEL
DFT_N2 = 128
V7X_VMEM_LIMIT = 56 * 1024 * 1024


def _cparams(n_grid):
    return pltpu.CompilerParams(dimension_semantics=("parallel",) * n_grid,
                                vmem_limit_bytes=V7X_VMEM_LIMIT)


def _const_spec(shape):
    nd = len(shape)
    return pl.BlockSpec(shape, lambda *_: (0,) * nd, pipeline_mode=pl.Buffered(1))


def _sigmoid(v):
    return 1.0 / (1.0 + jnp.exp(-v))


def _rms(v, g):
    return v * lax.rsqrt(jnp.mean(v * v, axis=-1, keepdims=True) + EPS) * g


def _dot(a, b):
    return jnp.dot(a, b, preferred_element_type=F32)


def _dot_nt(a, b):
    return lax.dot_general(a, b, (((1,), (1,)), ((), ())), preferred_element_type=F32)


def _mod_kernel(c_ref, w_ref, b_ref, o_ref):
    c = c_ref[...]
    s = (c * _sigmoid(c)).astype(BF16)
    o_ref[0] = _dot(s, w_ref[0].astype(BF16)) + b_ref[0]


def _mod_call(cc, w_mod, b_mod):
    depth, d, n = w_mod.shape
    tn = 1536
    return pl.pallas_call(
        _mod_kernel,
        grid=(depth, n // tn),
        in_specs=[pl.BlockSpec(cc.shape, lambda l, j: (0, 0)),
                  pl.BlockSpec((1, d, tn), lambda l, j: (l, 0, j)),
                  pl.BlockSpec((1, 1, tn), lambda l, j: (l, 0, j))],
        out_specs=pl.BlockSpec((1, cc.shape[0], tn), lambda l, j: (l, 0, j)),
        out_shape=jax.ShapeDtypeStruct((depth, cc.shape[0], n), F32),
        compiler_params=_cparams(2),
        name="mod",
    )(cc, w_mod, b_mod.reshape(depth, 1, n))


def _inproj_kernel(x_ref, sh_ref, sc_ref, g1_ref, win_ref, qn_ref, wq_ref, kvn_ref, wkv_ref,
                   wvt_ref, dc_ref, cs_ref,
                   q_ref, k_ref, vt_ref, nq_ref, nk_ref, nv_ref, z_ref, cy_ref):
    x = x_ref[0]
    h = _rms(x, g1_ref[...]) * (1.0 + sc_ref[0]) + sh_ref[0]
    p = _dot(h.astype(BF16), win_ref[...])

    cs = cs_ref[...]
    cos_t, sin_t = cs[:, :HEAD_SLOT], cs[:, HEAD_SLOT:]
    cos4 = jnp.concatenate([cos_t] * N_HEADS, axis=-1)
    sin4 = jnp.concatenate([sin_t] * N_HEADS, axis=-1)

    r = _rms(p[:, 0:256], qn_ref[...]).astype(BF16)
    qq = _dot(r, wq_ref[...])
    q = (qq[:, :MLA_W] * cos4 + qq[:, MLA_W:] * sin4) * (MLA_QK ** -0.5 * LOG2_E)
    q_ref[0] = q.astype(BF16)

    kvr = p[:, 256:512]
    kvn = _rms(kvr[:, :128], kvn_ref[...])
    comb = jnp.concatenate([kvn, kvr[:, 128:]], axis=-1).astype(BF16)
    kk = _dot(comb, wkv_ref[...])
    k = kk[:, :MLA_W] * cos4 + kk[:, MLA_W:] * sin4
    k_ref[0] = k.astype(BF16)
    vt = _dot_nt(wvt_ref[...], comb)
    sub = lax.broadcasted_iota(jnp.int32, (vt.shape[0], 1), 0)
    ones_row = jnp.where(sub % VT_ROWS == MLA_V, 1.0, 0.0).astype(F32)
    vt_ref[0] = (vt + ones_row).astype(BF16)

    nq_ref[0] = (p[:, 512:768] * (NA_HD ** -0.5)).astype(BF16)
    nk_ref[0] = p[:, 768:1024].astype(BF16)
    nv_ref[0] = p[:, 1024:1280].astype(BF16)

    z_ref[0] = _dot(p[:, 1280:1536].astype(BF16), dc_ref[...]).astype(BF16)

    cy_ref[0] = p[:, 1536:1792] * _sigmoid(p[:, 1792:2048])


def _inproj_call(x, sh, sc, g1, win, qn, wq, kvn, wkv, wvt, dc, cs):
    b, l, d = x.shape
    t = min(l, 512)
    tok = lambda w: pl.BlockSpec((1, t, w), lambda bi, i: (bi, i, 0))
    vec = pl.BlockSpec((1, 1, d), lambda bi, i: (bi, 0, 0))
    outs = [(MLA_W, BF16), (MLA_W, BF16), None, (GROUP_W, BF16), (GROUP_W, BF16),
            (GROUP_W, BF16), (2 * GROUP_W, BF16), (GROUP_W, F32)]
    vt_rows = N_HEADS * VT_ROWS
    out_specs = [tok(o[0]) if o else pl.BlockSpec((1, vt_rows, t), lambda bi, i: (bi, 0, i))
                 for o in outs]
    out_shape = [jax.ShapeDtypeStruct((b, l, o[0]), o[1]) if o
                 else jax.ShapeDtypeStruct((b, vt_rows, l), BF16) for o in outs]
    return pl.pallas_call(
        _inproj_kernel,
        grid=(b, l // t),
        in_specs=[tok(d), vec, vec, _const_spec(g1.shape), _const_spec(win.shape),
                  _const_spec(qn.shape), _const_spec(wq.shape), _const_spec(kvn.shape),
                  _const_spec(wkv.shape), _const_spec(wvt.shape), _const_spec(dc.shape),
                  pl.BlockSpec((t, 2 * HEAD_SLOT), lambda bi, i: (i, 0))],
        out_specs=out_specs,
        out_shape=out_shape,
        compiler_params=_cparams(2),
        name="inproj",
    )(x, sh, sc, g1, win, qn, wq, kvn, wkv, wvt, dc, cs)


def _flash_kernel(*refs, n_chunks, tk, has_extra):
    if has_extra:
        q_ref, k_ref, vt_ref, k2_ref, vt2_ref, o_ref, sa_ref, sb_ref, sx_ref = refs
    else:
        q_ref, k_ref, vt_ref, o_ref, sa_ref, sb_ref = refs
    q = q_ref[0]
    tq = q.shape[0]

    def scores(kc, s_ref):
        st = _dot_nt(kc, q)
        s_ref[...] = st
        return jnp.max(st, axis=0, keepdims=True)

    def accumulate(s_ref, mx, vtc, carry):
        m, acc = carry
        m_new = jnp.maximum(m, mx)
        p = jnp.exp2(s_ref[...] - m_new).astype(BF16)
        return m_new, jnp.exp2(m - m_new) * acc + _dot(vtc, p)

    def k_chunk(c):
        return k_ref[0, pl.ds(pl.multiple_of(c * tk, tk), tk), :]

    def vt_chunk(c):
        return vt_ref[0, :, pl.ds(pl.multiple_of(c * tk, tk), tk)]

    def pair(j, carry):
        mx_a, m, acc = carry
        mx_b = scores(k_chunk(2 * j + 1), sb_ref)
        m, acc = accumulate(sa_ref, mx_a, vt_chunk(2 * j), (m, acc))
        mx_a = scores(k_chunk(2 * j + 2), sa_ref)
        m, acc = accumulate(sb_ref, mx_b, vt_chunk(2 * j + 1), (m, acc))
        return mx_a, m, acc

    n_pairs = (n_chunks - 1) // 2
    carry = (scores(k_chunk(0), sa_ref), jnp.full((1, tq), -jnp.inf, F32),
             jnp.zeros((VT_ROWS, tq), F32))
    mx, m, acc = lax.fori_loop(0, n_pairs, pair, carry)
    pending = (sa_ref, mx, vt_chunk(2 * n_pairs))
    tail = [(k_chunk(c), vt_chunk(c), sb_ref if c % 2 else sa_ref)
            for c in range(2 * n_pairs + 1, n_chunks)]
    if has_extra:
        tail.append((k2_ref[0], vt2_ref[0], sx_ref))
    for kc, vtc, s_ref in tail:
        mx_next = scores(kc, s_ref)
        m, acc = accumulate(*pending, (m, acc))
        pending = (s_ref, mx_next, vtc)
    m, acc = accumulate(*pending, (m, acc))

    o_t = acc / acc[MLA_V:MLA_V + 1, :]
    o_t = jnp.concatenate([o_t, jnp.zeros((HEAD_SLOT - VT_ROWS, tq), F32)], axis=0)
    o_ref[0] = o_t.T.astype(BF16)


def _flash_call(q, k, vt, k2=None, vt2=None):
    b, lq, _ = q.shape
    lk = k.shape[1]
    tq = min(lq, 1024)
    tk = min(lk, 1024)
    has_extra = k2 is not None
    qspec = pl.BlockSpec((1, tq, HEAD_SLOT), lambda bi, h, i: (bi, i, h))
    kspec = lambda n: pl.BlockSpec((1, n, HEAD_SLOT), lambda bi, h, i: (bi, 0, h))
    vspec = lambda n: pl.BlockSpec((1, VT_ROWS, n), lambda bi, h, i: (bi, h, 0))
    in_specs = [qspec, kspec(lk), vspec(lk)]
    args = [q, k, vt]
    scratch = [pltpu.VMEM((tk, tq), F32), pltpu.VMEM((tk, tq), F32)]
    if has_extra:
        in_specs += [kspec(k2.shape[1]), vspec(k2.shape[1])]
        args += [k2, vt2]
        scratch.append(pltpu.VMEM((k2.shape[1], tq), F32))
    return pl.pallas_call(
        functools.partial(_flash_kernel, n_chunks=lk // tk, tk=tk, has_extra=has_extra),
        grid=(b, N_HEADS, lq // tq),
        in_specs=in_specs,
        out_specs=qspec,
        out_shape=jax.ShapeDtypeStruct((b, lq, MLA_W), BF16),
        scratch_shapes=scratch,
        compiler_params=_cparams(3),
        name="mla_attn",
    )(*args)


def _head_stack(q):
    lane = lax.broadcasted_iota(jnp.int32, q.shape, 1)
    return jnp.concatenate(
        [jnp.where(lane // NA_HD == h, q, jnp.zeros_like(q)) for h in range(N_HEADS)], axis=0)


def _head_unstack(o, n):
    lane = lax.broadcasted_iota(jnp.int32, (n, GROUP_W), 1)
    out = jnp.zeros((n, GROUP_W), F32)
    for h in range(N_HEADS):
        out = out + jnp.where(lane // NA_HD == h, o[h * n:(h + 1) * n], 0.0)
    return out


def _natten_kernel(q_ref, k_ref, v_ref, kc_ref, vc_ref, bias_ref, o_ref, *, rows_per_step, rows):
    blk = pl.program_id(1)
    kc = kc_ref[0]
    vc = vc_ref[0]
    n_loc = NA_WIN_H * GRID_W

    def body(j, _):
        r = blk * rows_per_step + j
        rs = jnp.clip(r - NA_WIN_H // 2, 0, rows - NA_WIN_H)
        d0 = rs - r + (NA_WIN_H - 1)
        qs = _head_stack(q_ref[0, pl.ds(pl.multiple_of(j * GRID_W, GRID_W), GRID_W), :])
        kstart = pl.multiple_of(rs * GRID_W, GRID_W)
        s_loc = _dot_nt(qs, k_ref[0, pl.ds(kstart, n_loc), :])
        bias = jnp.concatenate(
            [jnp.concatenate([bias_ref[h, d0 + 2 * w] for w in range(NA_WIN_H // 2)], axis=-1)
             for h in range(N_HEADS)], axis=0)
        s_loc = s_loc + bias
        s_ctx = _dot_nt(qs, kc)
        m = jnp.maximum(jnp.max(s_loc, axis=-1, keepdims=True),
                        jnp.max(s_ctx, axis=-1, keepdims=True))
        p_loc = jnp.exp(s_loc - m)
        p_ctx = jnp.exp(s_ctx - m)
        denom = jnp.sum(p_loc, axis=-1, keepdims=True) + jnp.sum(p_ctx, axis=-1, keepdims=True)
        o = (_dot(p_loc.astype(BF16), v_ref[0, pl.ds(kstart, n_loc), :])
             + _dot(p_ctx.astype(BF16), vc))
        o = _head_unstack(o / denom, GRID_W)
        o_ref[0, pl.ds(pl.multiple_of(j * GRID_W, GRID_W), GRID_W), :] = o.astype(BF16)
        return 0

    lax.fori_loop(0, rows_per_step, body, 0, unroll=True)


def _natten_call(q, k, v, kc, vc, bias):
    b, l, w = q.shape
    rows = l // GRID_W
    rows_per_step = 8
    t = rows_per_step * GRID_W
    full = lambda n: pl.BlockSpec((1, n, w), lambda bi, i: (bi, 0, 0))
    return pl.pallas_call(
        functools.partial(_natten_kernel, rows_per_step=rows_per_step, rows=rows),
        grid=(b, rows // rows_per_step),
        in_specs=[pl.BlockSpec((1, t, w), lambda bi, i: (bi, i, 0)),
                  full(l), full(l), full(kc.shape[1]), full(kc.shape[1]),
                  _const_spec(bias.shape)],
        out_specs=pl.BlockSpec((1, t, w), lambda bi, i: (bi, i, 0)),
        out_shape=jax.ShapeDtypeStruct((b, l, w), BF16),
        compiler_params=_cparams(2),
        name="natten",
    )(q, k, v, kc, vc, bias)


def _na_ctx_kernel(q_ref, k_ref, v_ref, o_ref):
    n = q_ref.shape[1]
    s = _dot_nt(_head_stack(q_ref[0]), k_ref[0])
    m = jnp.max(s, axis=-1, keepdims=True)
    p = jnp.exp(s - m)
    denom = jnp.sum(p, axis=-1, keepdims=True)
    o = _dot(p.astype(BF16), v_ref[0]) / denom
    o_ref[0] = _head_unstack(o, n).astype(BF16)


def _na_ctx_call(q, k, v):
    b, n, w = q.shape
    spec = pl.BlockSpec((1, n, w), lambda bi: (bi, 0, 0))
    return pl.pallas_call(
        _na_ctx_kernel, grid=(b,), in_specs=[spec, spec, spec], out_specs=spec,
        out_shape=jax.ShapeDtypeStruct((b, n, w), BF16),
        compiler_params=_cparams(1), name="na_ctx",
    )(q, k, v)


def _fnet1_kernel(z_ref, w_ref, y_ref, *, n_inner):
    p = _dot(w_ref[...], z_ref[0])
    n1 = p.shape[0] // 2
    for j in range(n_inner):
        zr_c = p[:n1, j * 512:j * 512 + 256]
        zi_c = p[:n1, j * 512 + 256:(j + 1) * 512]
        zr_s = p[n1:, j * 512:j * 512 + 256]
        zi_s = p[n1:, j * 512 + 256:(j + 1) * 512]
        y_ref[0, 0, :, j * 256:(j + 1) * 256] = (zr_c + zi_s).astype(BF16)
        y_ref[0, 1, :, j * 256:(j + 1) * 256] = (zi_c - zr_s).astype(BF16)


def _fnet1_call(z2, w1s, n1):
    b = z2.shape[0]
    n_inner = 8
    return pl.pallas_call(
        functools.partial(_fnet1_kernel, n_inner=n_inner),
        grid=(b, DFT_N2 // n_inner),
        in_specs=[pl.BlockSpec((1, n1, n_inner * 512), lambda bi, i: (bi, 0, i)),
                  _const_spec(w1s.shape)],
        out_specs=pl.BlockSpec((1, 2, n1, n_inner * 256), lambda bi, i: (bi, 0, 0, i)),
        out_shape=jax.ShapeDtypeStruct((b, 2, n1, DFT_N2 * GROUP_W), BF16),
        compiler_params=_cparams(2),
        name="fnet_stage1",
    )(z2, w1s)


def _fnet2_kernel(y_ref, f_ref, o_ref, *, n_inner, norm):
    for j in range(n_inner):
        f = f_ref[j]
        o = _dot(f[:, :DFT_N2], y_ref[0, 0, j]) + _dot(f[:, DFT_N2:], y_ref[0, 1, j])
        o_ref[0, :, j * GROUP_W:(j + 1) * GROUP_W] = (o * norm).astype(BF16)


def _fnet2_call(y5, ftab, n1, norm):
    b = y5.shape[0]
    n_inner = 8
    return pl.pallas_call(
        functools.partial(_fnet2_kernel, n_inner=n_inner, norm=norm),
        grid=(b, n1 // n_inner),
        in_specs=[pl.BlockSpec((1, 2, n_inner, DFT_N2, GROUP_W), lambda bi, i: (bi, 0, i, 0, 0)),
                  pl.BlockSpec((n_inner, DFT_N2, 2 * DFT_N2), lambda bi, i: (i, 0, 0))],
        out_specs=pl.BlockSpec((1, DFT_N2, n_inner * GROUP_W), lambda bi, i: (bi, 0, i)),
        out_shape=jax.ShapeDtypeStruct((b, DFT_N2, n1 * GROUP_W), BF16),
        compiler_params=_cparams(2),
        name="fnet_stage2",
    )(y5, ftab)


def _dft_small_kernel(z_ref, f_ref, o_ref, *, norm):
    z = z_ref[0]
    f = f_ref[...]
    n = z.shape[0]
    o = _dot(f[:, :n], z[:, :GROUP_W]) + _dot(f[:, n:], z[:, GROUP_W:])
    o_ref[0] = (o * norm).astype(BF16)


def _dft_small_call(z, ftab, norm):
    b, n, _ = z.shape
    return pl.pallas_call(
        functools.partial(_dft_small_kernel, norm=norm),
        grid=(b,),
        in_specs=[pl.BlockSpec((1, n, 2 * GROUP_W), lambda bi: (bi, 0, 0)), _const_spec(ftab.shape)],
        out_specs=pl.BlockSpec((1, n, GROUP_W), lambda bi: (bi, 0, 0)),
        out_shape=jax.ShapeDtypeStruct((b, n, GROUP_W), BF16),
        compiler_params=_cparams(1), name="fnet_ctx",
    )(z, ftab)


def _fourier_tables(l):
    if l <= 256:
        m = (np.arange(l)[:, None] * np.arange(l)[None, :]) % l
        ang = 2.0 * np.pi * m / l
        return None, np.concatenate([np.cos(ang), np.sin(ang)], axis=1).astype(np.float32), 0
    n1 = l // DFT_N2
    m1 = (np.arange(n1)[:, None] * np.arange(n1)[None, :]) % n1
    a1 = 2.0 * np.pi * m1 / n1
    w1s = np.concatenate([np.cos(a1), np.sin(a1)], axis=0).astype(np.float32)
    kk = np.arange(n1)[:, None, None] + n1 * np.arange(DFT_N2)[None, :, None]
    m2 = (kk * np.arange(DFT_N2)[None, None, :]) % l
    a2 = 2.0 * np.pi * m2 / l
    ftab = np.concatenate([np.cos(a2), np.sin(a2)], axis=2).astype(np.float32)
    return w1s, ftab, n1


def _fourier_mix(z):
    b, l, _ = z.shape
    norm = float((l * FN_GW) ** -0.5)
    w1s, ftab, n1 = _fourier_tables(l)
    if w1s is None:
        return _dft_small_call(z, jnp.asarray(ftab).astype(BF16), norm)
    y = _fnet1_call(z.reshape(b, n1, DFT_N2 * 2 * GROUP_W), jnp.asarray(w1s).astype(BF16), n1)
    o = _fnet2_call(y.reshape(b, 2, n1, DFT_N2, GROUP_W), jnp.asarray(ftab).astype(BF16), n1, norm)
    return o.reshape(b, l, GROUP_W)


def _conv_kernel(prev_ref, cur_ref, next_ref, w_ref, b_ref, g_ref, beta_ref, o_ref, buf_ref):
    i = pl.program_id(1)
    n = pl.num_programs(1)
    t = cur_ref.shape[1]
    buf_ref[0:CV_HALO, :] = jnp.where(i > 0, prev_ref[0], 0.0)
    buf_ref[CV_HALO:CV_HALO + t, :] = cur_ref[0]
    buf_ref[CV_HALO + t:, :] = jnp.where(i < n - 1, next_ref[0], 0.0)
    w = w_ref[...]
    first = CV_HALO - CV_K // 2
    acc = None
    for res in range(SUBLANES):
        z = None
        for base in range(0, first + CV_K, SUBLANES):
            j = base + res - first
            if 0 <= j < CV_K:
                term = w[j:j + 1, :] * buf_ref[base:base + t + SUBLANES, :]
                z = term if z is None else z + term
        z = z[res:res + t, :]
        acc = z if acc is None else acc + z
    y = acc + b_ref[...]
    mu = jnp.mean(y, axis=-1, keepdims=True)
    var = jnp.mean(jnp.square(y - mu), axis=-1, keepdims=True)
    y = (y - mu) * lax.rsqrt(var + EPS) * g_ref[...] + beta_ref[...]
    o_ref[0] = (y * _sigmoid(y)).astype(BF16)


def _conv_call(cy, w_dw, b_dw, ln_g, ln_b):
    b, l, w = cy.shape
    t = min(l, 512)
    hb = t // CV_HALO
    n_halo = l // CV_HALO
    return pl.pallas_call(
        _conv_kernel,
        grid=(b, l // t),
        in_specs=[pl.BlockSpec((1, CV_HALO, w), lambda bi, i: (bi, jnp.maximum(i * hb - 1, 0), 0)),
                  pl.BlockSpec((1, t, w), lambda bi, i: (bi, i, 0)),
                  pl.BlockSpec((1, CV_HALO, w),
                               lambda bi, i: (bi, jnp.minimum((i + 1) * hb, n_halo - 1), 0)),
                  _const_spec(w_dw.shape), _const_spec(b_dw.shape), _const_spec(ln_g.shape),
                  _const_spec(ln_b.shape)],
        out_specs=pl.BlockSpec((1, t, w), lambda bi, i: (bi, i, 0)),
        out_shape=jax.ShapeDtypeStruct((b, l, w), BF16),
        scratch_shapes=[pltpu.VMEM((t + 2 * CV_HALO, w), F32)],
        compiler_params=_cparams(2),
        name="conformer_conv",
    )(cy, cy, cy, w_dw, b_dw, ln_g, ln_b)


def _outmlp_kernel(x_ref, oa_ref, oc_ref, ob_ref, od_ref, wo_ref, gt1_ref, sh2_ref, sc2_ref,
                   gt2_ref, g2_ref, w1_ref, w2_ref, fg_ref, o_ref, *, last, ff_chunk):
    y = (_dot(oa_ref[0], wo_ref[0:MLA_W, :])
         + _dot(oc_ref[0], wo_ref[MLA_W:MLA_W + GROUP_W, :])
         + _dot(ob_ref[0], wo_ref[MLA_W + GROUP_W:MLA_W + 2 * GROUP_W, :])
         + _dot(od_ref[0], wo_ref[MLA_W + 2 * GROUP_W:, :]))
    x1 = x_ref[0] + gt1_ref[0] * y
    h = (_rms(x1, g2_ref[...]) * (1.0 + sc2_ref[0]) + sh2_ref[0]).astype(BF16)
    ff = jnp.zeros(x1.shape, F32)
    for c in range(D_FF // ff_chunk):
        u = jnp.maximum(_dot(h, w1_ref[:, c * ff_chunk:(c + 1) * ff_chunk]), 0.0)
        ff = ff + _dot((u * u).astype(BF16), w2_ref[c * ff_chunk:(c + 1) * ff_chunk, :])
    x2 = x1 + gt2_ref[0] * ff
    if last:
        x2 = _rms(x2, fg_ref[...])
    o_ref[0] = x2


def _outmlp_call(x, o_mla, o_na, o_fn, o_cv, wo, gt1, sh2, sc2, gt2, g2, w1, w2, fg, last):
    b, l, d = x.shape
    t = min(l, 512)
    tok = lambda w: pl.BlockSpec((1, t, w), lambda bi, i: (bi, i, 0))
    vec = pl.BlockSpec((1, 1, d), lambda bi, i: (bi, 0, 0))
    return pl.pallas_call(
        functools.partial(_outmlp_kernel, last=last, ff_chunk=1024),
        grid=(b, l // t),
        in_specs=[tok(d), tok(MLA_W), tok(GROUP_W), tok(GROUP_W), tok(GROUP_W),
                  _const_spec(wo.shape), vec, vec, vec, vec, _const_spec(g2.shape),
                  _const_spec(w1.shape), _const_spec(w2.shape), _const_spec(fg.shape)],
        out_specs=tok(d),
        out_shape=jax.ShapeDtypeStruct((b, l, d), F32),
        compiler_params=_cparams(2),
        name="outproj_mlp",
    )(x, o_mla, o_na, o_fn, o_cv, wo, gt1, sh2, sc2, gt2, g2, w1, w2, fg)


def _rope_tables(l, with_rope):
    ones = jnp.ones((l, MLA_NOPE), F32)
    zeros = jnp.zeros((l, MLA_NOPE), F32)
    pad = jnp.zeros((l, HEAD_SLOT - MLA_QK), F32)
    if with_rope:
        t = jnp.arange(l)
        row = (t // GRID_W).astype(F32)
        col = (t % GRID_W).astype(F32)
        per_axis = MLA_ROPE // 2
        inv = ROPE_BASE ** (-jnp.arange(0, per_axis, 2, dtype=F32) / per_axis)
        ang = jnp.concatenate([row[:, None] * inv, col[:, None] * inv], axis=-1)
        cos, sin = jnp.cos(ang), jnp.sin(ang)
    else:
        cos = jnp.ones((l, MLA_ROPE // 2), F32)
        sin = jnp.zeros((l, MLA_ROPE // 2), F32)
    return jnp.concatenate([ones, cos, cos, pad, zeros, -sin, sin, pad], axis=-1)


def _swap_halves(w):
    half = w.shape[-1] // 2
    return jnp.concatenate([w[..., half:], w[..., :half]], axis=-1)


def _layer_weights(w_in, w_uq, w_ukv, w_out, rpb):
    d = w_in.shape[0]
    k_r_end = 256 + 128 + MLA_ROPE
    win = jnp.concatenate([w_in[:, :k_r_end], jnp.zeros((d, HEAD_SLOT - MLA_ROPE), F32),
                           w_in[:, k_r_end:]], axis=1).astype(BF16)

    rq = w_uq.shape[0]
    w3 = w_uq.reshape(rq, N_HEADS, MLA_QK)
    zpad = jnp.zeros((rq, N_HEADS, HEAD_SLOT - MLA_QK), F32)
    plain = jnp.concatenate([w3, zpad], axis=-1)
    swapped = jnp.concatenate([jnp.zeros((rq, N_HEADS, MLA_NOPE), F32),
                               _swap_halves(w3[..., MLA_NOPE:]), zpad], axis=-1)
    wq = jnp.concatenate([plain.reshape(rq, MLA_W), swapped.reshape(rq, MLA_W)], axis=1).astype(BF16)

    rkv = w_ukv.shape[0]
    u3 = w_ukv.reshape(rkv, N_HEADS, MLA_NOPE + MLA_V)
    eye = jnp.eye(MLA_ROPE, dtype=F32)
    slot_pad_r = jnp.zeros((MLA_ROPE, N_HEADS, HEAD_SLOT - MLA_QK), F32)

    def slots(top, rope_block):
        top = jnp.concatenate([top, jnp.zeros((rkv, N_HEADS, HEAD_SLOT - top.shape[-1]), F32)], axis=-1)
        mid = jnp.concatenate([jnp.zeros((MLA_ROPE, N_HEADS, MLA_NOPE), F32),
                               jnp.broadcast_to(rope_block[:, None, :], (MLA_ROPE, N_HEADS, MLA_ROPE)),
                               slot_pad_r], axis=-1)
        bot = jnp.zeros((2 * HEAD_SLOT - rkv - MLA_ROPE, N_HEADS, HEAD_SLOT), F32)
        return jnp.concatenate([top, mid, bot], axis=0).reshape(2 * HEAD_SLOT, MLA_W)

    k_plain = slots(u3[..., :MLA_NOPE], eye)
    k_swap = slots(jnp.zeros((rkv, N_HEADS, MLA_NOPE), F32), _swap_halves(eye))
    wkv = jnp.concatenate([k_plain, k_swap], axis=1).astype(BF16)

    vt3 = jnp.transpose(u3[..., MLA_NOPE:], (1, 2, 0))
    vt3 = jnp.pad(vt3, ((0, 0), (0, VT_ROWS - MLA_V), (0, 2 * HEAD_SLOT - rkv)))
    wvt = vt3.reshape(N_HEADS * VT_ROWS, 2 * HEAD_SLOT).astype(BF16)

    o3 = w_out[:N_HEADS * MLA_V].reshape(N_HEADS, MLA_V, d)
    o3 = jnp.concatenate([o3, jnp.zeros((N_HEADS, HEAD_SLOT - MLA_V, d), F32)], axis=1)
    wo = jnp.concatenate([o3.reshape(MLA_W, d), w_out[N_HEADS * MLA_V:]], axis=0).astype(BF16)

    qc = np.arange(GRID_W)[:, None]
    kc = np.arange(GRID_W)[None, :]
    ws = np.clip(qc - NA_WIN_W // 2, 0, GRID_W - NA_WIN_W)
    in_win = (kc >= ws) & (kc < ws + NA_WIN_W)
    edge = GRID_W - NA_WIN_W
    rp = jnp.pad(rpb.astype(F32), ((0, 0), (0, 0), (edge, edge)))
    toep = jnp.stack([rp[:, :, GRID_W - 1 - q:2 * GRID_W - 1 - q] for q in range(GRID_W)], axis=2)
    t2 = jnp.where(in_win[None, None], toep, NEG_INF)
    bias = jnp.concatenate([t2[:, :-1], t2[:, 1:]], axis=-1)
    return win, wq, wkv, wvt, wo, bias


def _channel_dft():
    m = (np.arange(FN_GW)[:, None] * np.arange(FN_GW)[None, :]) % FN_GW
    ang = 2.0 * np.pi * m / FN_GW
    eye = np.eye(FN_GROUPS)
    return np.concatenate([np.kron(eye, np.cos(ang)), -np.kron(eye, np.sin(ang))],
                          axis=1).astype(np.float32)


def kernel(x, c, ctx, c_ctx, w_mod, b_mod, norm1_g, norm2_g, w_in, mla_q_norm, mla_w_uq, mla_kv_norm, mla_w_ukv, na_rpb, cv_w_dw, cv_b_dw, cv_ln_g, cv_ln_b, w_out, w_ff1, w_ff2, final_g):
    depth = w_mod.shape[0]
    b, s, d = x.shape
    n_ctx = ctx.shape[1]

    cc = jnp.concatenate([c, c_ctx[None, :], jnp.zeros((8 - b - 1, d), F32)], axis=0)
    mods = _mod_call(cc, w_mod, b_mod)

    cs_lat = _rope_tables(s, True)
    cs_ctx = _rope_tables(n_ctx, False)
    dc = jnp.asarray(_channel_dft()).astype(BF16)
    row = lambda p: p.reshape(1, -1)
    fg = row(final_g)

    xc = ctx
    for i in range(depth):
        last = i == depth - 1
        win, wq, wkv, wvt, wo, bias = _layer_weights(w_in[i], mla_w_uq[i], mla_w_ukv[i], w_out[i], na_rpb[i])
        w1 = w_ff1[i].astype(BF16)
        w2 = w_ff2[i].astype(BF16)
        mx = [m.reshape(b, 1, d) for m in jnp.split(mods[i, :b], 6, axis=-1)]
        mc = [jnp.broadcast_to(m.reshape(1, 1, d), (b, 1, d))
              for m in jnp.split(mods[i, b:b + 1], 6, axis=-1)]
        proj = functools.partial(_inproj_call, g1=row(norm1_g[i]), win=win, qn=row(mla_q_norm[i]),
                                 wq=wq, kvn=row(mla_kv_norm[i]), wkv=wkv, wvt=wvt, dc=dc)
        conv = functools.partial(_conv_call, w_dw=cv_w_dw[i], b_dw=row(cv_b_dw[i]),
                                 ln_g=row(cv_ln_g[i]), ln_b=row(cv_ln_b[i]))
        mlp = functools.partial(_outmlp_call, wo=wo, g2=row(norm2_g[i]), w1=w1, w2=w2, fg=fg)

        q, k, vt, nq, nk, nv, z, cy = proj(x, mx[0], mx[1], cs=cs_lat)
        cq, ck, cvt, cnq, cnk, cnv, cz, ccy = proj(xc, mc[0], mc[1], cs=cs_ctx)

        o_mla = _flash_call(q, k, vt, ck, cvt)
        o_na = _natten_call(nq, nk, nv, cnk, cnv, bias)
        o_fn = _fourier_mix(z)
        o_cv = conv(cy)
        x = mlp(x, o_mla, o_na, o_fn, o_cv, gt1=mx[2], sh2=mx[3], sc2=mx[4], gt2=mx[5], last=last)

        if not last:
            co_mla = _flash_call(cq, ck, cvt)
            co_na = _na_ctx_call(cnq, cnk, cnv)
            co_fn = _fourier_mix(cz)
            co_cv = conv(ccy)
            xc = mlp(xc, co_mla, co_na, co_fn, co_cv, gt1=mc[2], sh2=mc[3], sc2=mc[4], gt2=mc[5],
                     last=False)
    return x
```

```python
import functools

import jax
import jax.numpy as jnp
import numpy as np
from jax import lax
from jax.experimental import pallas as pl
from jax.experimental.pallas import tpu as pltpu

F32 = jnp.float32
BF16 = jnp.bfloat16

EPS = 1e-6
ROPE_BASE = 10000.0
NEG_INF = -1e30

D_MODEL = 1024
GRID_W = 64
N_HEADS = 4
GROUP_W = 256
HEAD_SLOT = 128
MLA_NOPE = 64
MLA_ROPE = 32
MLA_QK = MLA_NOPE + MLA_ROPE
MLA_V = 64
MLA_W = N_HEADS * HEAD_SLOT
VT_ROWS = 80
LOG2_E = 1.4426950408889634
ATTN_SUB_KEYS = 256
ATTN_QK_LEAD = 1
NA_HD = 64
NA_WIN_H = 8
NA_WIN_W = 16
FN_GROUPS = 4
FN_GW = GROUP_W // FN_GROUPS
CV_K = 31
CV_HALO = 16
SUBLANES = 8
D_FF = 4 * D_MODEL
DFT_N2 = 128
V7X_VMEM_LIMIT = 56 * 1024 * 1024


def _cparams(n_grid):
    return pltpu.CompilerParams(dimension_semantics=("parallel",) * n_grid,
                                vmem_limit_bytes=V7X_VMEM_LIMIT)


def _const_spec(shape):
    nd = len(shape)
    return pl.BlockSpec(shape, lambda *_: (0,) * nd, pipeline_mode=pl.Buffered(1))


def _sigmoid(v):
    return 1.0 / (1.0 + jnp.exp(-v))


def _rms(v, g):
    return v * lax.rsqrt(jnp.mean(v * v, axis=-1, keepdims=True) + EPS) * g


def _dot(a, b):
    return jnp.dot(a, b, preferred_element_type=F32)


def _dot_nt(a, b):
    return lax.dot_general(a, b, (((1,), (1,)), ((), ())), preferred_element_type=F32)


def _mod_kernel(c_ref, w_ref, b_ref, o_ref):
    c = c_ref[...]
    s = (c * _sigmoid(c)).astype(BF16)
    o_ref[0] = _dot(s, w_ref[0].astype(BF16)) + b_ref[0]


def _mod_call(cc, w_mod, b_mod):
    depth, d, n = w_mod.shape
    tn = 1536
    return pl.pallas_call(
        _mod_kernel,
        grid=(depth, n // tn),
        in_specs=[pl.BlockSpec(cc.shape, lambda l, j: (0, 0)),
                  pl.BlockSpec((1, d, tn), lambda l, j: (l, 0, j)),
                  pl.BlockSpec((1, 1, tn), lambda l, j: (l, 0, j))],
        out_specs=pl.BlockSpec((1, cc.shape[0], tn), lambda l, j: (l, 0, j)),
        out_shape=jax.ShapeDtypeStruct((depth, cc.shape[0], n), F32),
        compiler_params=_cparams(2),
        name="mod",
    )(cc, w_mod, b_mod.reshape(depth, 1, n))


def _inproj_kernel(x_ref, sh_ref, sc_ref, g1_ref, win_ref, qn_ref, wq_ref, kvn_ref, wkv_ref,
                   wvt_ref, dc_ref, cs_ref,
                   q_ref, k_ref, vt_ref, nq_ref, nk_ref, nv_ref, z_ref, cy_ref):
    x = x_ref[0]
    h = _rms(x, g1_ref[...]) * (1.0 + sc_ref[0]) + sh_ref[0]
    p = _dot(h.astype(BF16), win_ref[...])

    cs = cs_ref[...]
    cos_t, sin_t = cs[:, :HEAD_SLOT], cs[:, HEAD_SLOT:]
    cos4 = jnp.concatenate([cos_t] * N_HEADS, axis=-1)
    sin4 = jnp.concatenate([sin_t] * N_HEADS, axis=-1)

    r = _rms(p[:, 0:256], qn_ref[...]).astype(BF16)
    qq = _dot(r, wq_ref[...])
    q = (qq[:, :MLA_W] * cos4 + qq[:, MLA_W:] * sin4) * (MLA_QK ** -0.5 * LOG2_E)
    q_ref[0] = q.astype(BF16)

    kvr = p[:, 256:512]
    kvn = _rms(kvr[:, :128], kvn_ref[...])
    comb = jnp.concatenate([kvn, kvr[:, 128:]], axis=-1).astype(BF16)
    kk = _dot(comb, wkv_ref[...])
    k = kk[:, :MLA_W] * cos4 + kk[:, MLA_W:] * sin4
    k_ref[0] = k.astype(BF16)
    vt = _dot_nt(wvt_ref[...], comb)
    sub = lax.broadcasted_iota(jnp.int32, (vt.shape[0], 1), 0)
    ones_row = jnp.where(sub % VT_ROWS == MLA_V, 1.0, 0.0).astype(F32)
    vt_ref[0] = (vt + ones_row).astype(BF16)

    nq_ref[0] = (p[:, 512:768] * (NA_HD ** -0.5 * LOG2_E)).astype(BF16)
    nk_ref[0] = p[:, 768:1024].astype(BF16)
    nv_ref[0] = p[:, 1024:1280].astype(BF16)

    z_ref[0] = _dot(p[:, 1280:1536].astype(BF16), dc_ref[...]).astype(BF16)

    cy_ref[0] = p[:, 1536:1792] * _sigmoid(p[:, 1792:2048])


def _inproj_call(x, sh, sc, g1, win, qn, wq, kvn, wkv, wvt, dc, cs):
    b, l, d = x.shape
    t = min(l, 512)
    tok = lambda w: pl.BlockSpec((1, t, w), lambda bi, i: (bi, i, 0))
    vec = pl.BlockSpec((1, 1, d), lambda bi, i: (bi, 0, 0))
    outs = [(MLA_W, BF16), (MLA_W, BF16), None, (GROUP_W, BF16), (GROUP_W, BF16),
            (GROUP_W, BF16), (2 * GROUP_W, BF16), (GROUP_W, F32)]
    vt_rows = N_HEADS * VT_ROWS
    out_specs = [tok(o[0]) if o else pl.BlockSpec((1, vt_rows, t), lambda bi, i: (bi, 0, i))
                 for o in outs]
    out_shape = [jax.ShapeDtypeStruct((b, l, o[0]), o[1]) if o
                 else jax.ShapeDtypeStruct((b, vt_rows, l), BF16) for o in outs]
    return pl.pallas_call(
        _inproj_kernel,
        grid=(b, l // t),
        in_specs=[tok(d), vec, vec, _const_spec(g1.shape), _const_spec(win.shape),
                  _const_spec(qn.shape), _const_spec(wq.shape), _const_spec(kvn.shape),
                  _const_spec(wkv.shape), _const_spec(wvt.shape), _const_spec(dc.shape),
                  pl.BlockSpec((t, 2 * HEAD_SLOT), lambda bi, i: (i, 0))],
        out_specs=out_specs,
        out_shape=out_shape,
        compiler_params=_cparams(2),
        name="inproj",
    )(x, sh, sc, g1, win, qn, wq, kvn, wkv, wvt, dc, cs)


def _flash_kernel(*refs, n_chunks, tk, has_extra):
    if has_extra:
        q_ref, k_ref, vt_ref, k2_ref, vt2_ref, o_ref, s0, s1, sx_ref = refs
    else:
        q_ref, k_ref, vt_ref, o_ref, s0, s1 = refs
    s_bufs = (s0, s1)
    q = q_ref[0]
    tq = q.shape[0]

    def chunk(c, parity):
        if isinstance(c, int) and c >= n_chunks:
            return (lambda r, n: k2_ref[0, pl.ds(r, n), :], lambda r, n: vt2_ref[0, :, pl.ds(r, n)],
                    k2_ref.shape[1], sx_ref)
        at = lambda r: pl.multiple_of(c * tk + r, ATTN_SUB_KEYS)
        return (lambda r, n: k_ref[0, pl.ds(at(r), n), :], lambda r, n: vt_ref[0, :, pl.ds(at(r), n)],
                tk, s_bufs[parity])

    def qk_block(k_rows, s_ref, r, mx):
        st = _dot_nt(k_rows(r, ATTN_SUB_KEYS), q)
        s_ref[pl.ds(r, ATTN_SUB_KEYS), :] = st
        mx_r = jnp.max(st, axis=0, keepdims=True)
        return mx_r if mx is None else jnp.maximum(mx, mx_r)

    def phase(t, parity, state, has_qk):
        mx, m, acc = state
        _, vt_cur, n_cur, s_cur = chunk(t, parity)
        k_next, _, n_next, s_next = chunk(t + 1, 1 - parity) if has_qk else (None, None, 0, None)
        m_new = jnp.maximum(m, mx)
        mx_next, part = None, None
        lead = ATTN_QK_LEAD * ATTN_SUB_KEYS
        for r in range(0, max(n_cur + lead, n_next), ATTN_SUB_KEYS):
            if r < n_next:
                mx_next = qk_block(k_next, s_next, r, mx_next)
            rp = r - lead
            if 0 <= rp < n_cur:
                p = jnp.exp2(s_cur[pl.ds(rp, ATTN_SUB_KEYS), :] - m_new).astype(BF16)
                d = _dot(vt_cur(rp, ATTN_SUB_KEYS), p)
                part = d if part is None else part + d
        acc = jnp.exp2(m - m_new) * acc + part
        return (mx_next if has_qk else mx), m_new, acc

    total = n_chunks + (1 if has_extra else 0)
    k0, _, n0, s_first = chunk(0, 0)
    mx0 = None
    for r in range(0, n0, ATTN_SUB_KEYS):
        mx0 = qk_block(k0, s_first, r, mx0)
    state = (mx0, jnp.full((1, tq), -jnp.inf, F32), jnp.zeros((VT_ROWS, tq), F32))

    def two_phases(j, state):
        state = phase(2 * j, 0, state, True)
        return phase(2 * j + 1, 1, state, True)

    n_loop = (n_chunks - 1) // 2
    state = lax.fori_loop(0, n_loop, two_phases, state)
    for t in range(2 * n_loop, total):
        state = phase(t, t % 2, state, t + 1 < total)
    _, _, acc = state

    o_t = acc / acc[MLA_V:MLA_V + 1, :]
    o_t = jnp.concatenate([o_t, jnp.zeros((HEAD_SLOT - VT_ROWS, tq), F32)], axis=0)
    o_ref[0] = o_t.T.astype(BF16)


def _flash_call(q, k, vt, k2=None, vt2=None):
    b, lq, _ = q.shape
    lk = k.shape[1]
    tq = min(lq, 1024)
    tk = min(lk, 1024)
    has_extra = k2 is not None
    qspec = pl.BlockSpec((1, tq, HEAD_SLOT), lambda bi, h, i: (bi, i, h))
    kspec = lambda n: pl.BlockSpec((1, n, HEAD_SLOT), lambda bi, h, i: (bi, 0, h))
    vspec = lambda n: pl.BlockSpec((1, VT_ROWS, n), lambda bi, h, i: (bi, h, 0))
    in_specs = [qspec, kspec(lk), vspec(lk)]
    args = [q, k, vt]
    scratch = [pltpu.VMEM((tk, tq), F32), pltpu.VMEM((tk, tq), F32)]
    if has_extra:
        in_specs += [kspec(k2.shape[1]), vspec(k2.shape[1])]
        args += [k2, vt2]
        scratch.append(pltpu.VMEM((k2.shape[1], tq), F32))
    return pl.pallas_call(
        functools.partial(_flash_kernel, n_chunks=lk // tk, tk=tk, has_extra=has_extra),
        grid=(b, N_HEADS, lq // tq),
        in_specs=in_specs,
        out_specs=qspec,
        out_shape=jax.ShapeDtypeStruct((b, lq, MLA_W), BF16),
        scratch_shapes=scratch,
        compiler_params=_cparams(3),
        name="mla_attn",
    )(*args)


def _head_stack(q):
    lane = lax.broadcasted_iota(jnp.int32, q.shape, 1)
    return jnp.concatenate(
        [jnp.where(lane // NA_HD == h, q, jnp.zeros_like(q)) for h in range(N_HEADS)], axis=0)


def _head_unstack(o, n):
    lane = lax.broadcasted_iota(jnp.int32, (n, GROUP_W), 1)
    out = jnp.zeros((n, GROUP_W), F32)
    for h in range(N_HEADS):
        out = out + jnp.where(lane // NA_HD == h, o[h * n:(h + 1) * n], 0.0)
    return out


def _natten_kernel(q_ref, k_ref, v_ref, kc_ref, vc_ref, bias_ref, o_ref, *, rows_per_step, rows):
    blk = pl.program_id(1)
    kc = kc_ref[0]
    vc = vc_ref[0]
    n_loc = NA_WIN_H * GRID_W

    def body(j, _):
        r = blk * rows_per_step + j
        rs = jnp.clip(r - NA_WIN_H // 2, 0, rows - NA_WIN_H)
        d0 = rs - r + (NA_WIN_H - 1)
        qs = _head_stack(q_ref[0, pl.ds(pl.multiple_of(j * GRID_W, GRID_W), GRID_W), :])
        kstart = pl.multiple_of(rs * GRID_W, GRID_W)
        s_loc = _dot_nt(qs, k_ref[0, pl.ds(kstart, n_loc), :])
        bias = jnp.concatenate(
            [jnp.concatenate([bias_ref[h, d0 + 2 * w] for w in range(NA_WIN_H // 2)], axis=-1)
             for h in range(N_HEADS)], axis=0)
        s_loc = s_loc + bias
        s_ctx = _dot_nt(qs, kc)
        m = jnp.maximum(jnp.max(s_loc, axis=-1, keepdims=True),
                        jnp.max(s_ctx, axis=-1, keepdims=True))
        p_loc = jnp.exp2(s_loc - m)
        p_ctx = jnp.exp2(s_ctx - m)
        denom = jnp.sum(p_loc, axis=-1, keepdims=True) + jnp.sum(p_ctx, axis=-1, keepdims=True)
        o = (_dot(p_loc.astype(BF16), v_ref[0, pl.ds(kstart, n_loc), :])
             + _dot(p_ctx.astype(BF16), vc))
        o = _head_unstack(o / denom, GRID_W)
        o_ref[0, pl.ds(pl.multiple_of(j * GRID_W, GRID_W), GRID_W), :] = o.astype(BF16)
        return 0

    lax.fori_loop(0, rows_per_step, body, 0, unroll=True)


def _natten_call(q, k, v, kc, vc, bias):
    b, l, w = q.shape
    rows = l // GRID_W
    rows_per_step = 16
    t = rows_per_step * GRID_W
    full = lambda n: pl.BlockSpec((1, n, w), lambda bi, i: (bi, 0, 0))
    return pl.pallas_call(
        functools.partial(_natten_kernel, rows_per_step=rows_per_step, rows=rows),
        grid=(b, rows // rows_per_step),
        in_specs=[pl.BlockSpec((1, t, w), lambda bi, i: (bi, i, 0)),
                  full(l), full(l), full(kc.shape[1]), full(kc.shape[1]),
                  _const_spec(bias.shape)],
        out_specs=pl.BlockSpec((1, t, w), lambda bi, i: (bi, i, 0)),
        out_shape=jax.ShapeDtypeStruct((b, l, w), BF16),
        compiler_params=_cparams(2),
        name="natten",
    )(q, k, v, kc, vc, bias)


def _na_ctx_kernel(q_ref, k_ref, v_ref, o_ref):
    n = q_ref.shape[1]
    s = _dot_nt(_head_stack(q_ref[0]), k_ref[0])
    m = jnp.max(s, axis=-1, keepdims=True)
    p = jnp.exp2(s - m)
    denom = jnp.sum(p, axis=-1, keepdims=True)
    o = _dot(p.astype(BF16), v_ref[0]) / denom
    o_ref[0] = _head_unstack(o, n).astype(BF16)


def _na_ctx_call(q, k, v):
    b, n, w = q.shape
    spec = pl.BlockSpec((1, n, w), lambda bi: (bi, 0, 0))
    return pl.pallas_call(
        _na_ctx_kernel, grid=(b,), in_specs=[spec, spec, spec], out_specs=spec,
        out_shape=jax.ShapeDtypeStruct((b, n, w), BF16),
        compiler_params=_cparams(1), name="na_ctx",
    )(q, k, v)


def _fnet1_kernel(z_ref, w_ref, y_ref, *, n_inner):
    p = _dot(w_ref[...], z_ref[0])
    n1 = p.shape[0] // 2
    for j in range(n_inner):
        zr_c = p[:n1, j * 512:j * 512 + 256]
        zi_c = p[:n1, j * 512 + 256:(j + 1) * 512]
        zr_s = p[n1:, j * 512:j * 512 + 256]
        zi_s = p[n1:, j * 512 + 256:(j + 1) * 512]
        y_ref[0, 0, :, j * 256:(j + 1) * 256] = (zr_c + zi_s).astype(BF16)
        y_ref[0, 1, :, j * 256:(j + 1) * 256] = (zi_c - zr_s).astype(BF16)


def _fnet1_call(z2, w1s, n1):
    b = z2.shape[0]
    n_inner = 8
    return pl.pallas_call(
        functools.partial(_fnet1_kernel, n_inner=n_inner),
        grid=(b, DFT_N2 // n_inner),
        in_specs=[pl.BlockSpec((1, n1, n_inner * 512), lambda bi, i: (bi, 0, i)),
                  _const_spec(w1s.shape)],
        out_specs=pl.BlockSpec((1, 2, n1, n_inner * 256), lambda bi, i: (bi, 0, 0, i)),
        out_shape=jax.ShapeDtypeStruct((b, 2, n1, DFT_N2 * GROUP_W), BF16),
        compiler_params=_cparams(2),
        name="fnet_stage1",
    )(z2, w1s)


def _fnet2_kernel(y_ref, f_ref, o_ref, *, n_inner, norm):
    for j in range(n_inner):
        f = f_ref[j]
        o = _dot(f[:, :DFT_N2], y_ref[0, 0, j]) + _dot(f[:, DFT_N2:], y_ref[0, 1, j])
        o_ref[0, :, j * GROUP_W:(j + 1) * GROUP_W] = (o * norm).astype(BF16)


def _fnet2_call(y5, ftab, n1, norm):
    b = y5.shape[0]
    n_inner = 8
    return pl.pallas_call(
        functools.partial(_fnet2_kernel, n_inner=n_inner, norm=norm),
        grid=(b, n1 // n_inner),
        in_specs=[pl.BlockSpec((1, 2, n_inner, DFT_N2, GROUP_W), lambda bi, i: (bi, 0, i, 0, 0)),
                  pl.BlockSpec((n_inner, DFT_N2, 2 * DFT_N2), lambda bi, i: (i, 0, 0))],
        out_specs=pl.BlockSpec((1, DFT_N2, n_inner * GROUP_W), lambda bi, i: (bi, 0, i)),
        out_shape=jax.ShapeDtypeStruct((b, DFT_N2, n1 * GROUP_W), BF16),
        compiler_params=_cparams(2),
        name="fnet_stage2",
    )(y5, ftab)


def _dft_small_kernel(z_ref, f_ref, o_ref, *, norm):
    z = z_ref[0]
    f = f_ref[...]
    n = z.shape[0]
    o = _dot(f[:, :n], z[:, :GROUP_W]) + _dot(f[:, n:], z[:, GROUP_W:])
    o_ref[0] = (o * norm).astype(BF16)


def _dft_small_call(z, ftab, norm):
    b, n, _ = z.shape
    return pl.pallas_call(
        functools.partial(_dft_small_kernel, norm=norm),
        grid=(b,),
        in_specs=[pl.BlockSpec((1, n, 2 * GROUP_W), lambda bi: (bi, 0, 0)), _const_spec(ftab.shape)],
        out_specs=pl.BlockSpec((1, n, GROUP_W), lambda bi: (bi, 0, 0)),
        out_shape=jax.ShapeDtypeStruct((b, n, GROUP_W), BF16),
        compiler_params=_cparams(1), name="fnet_ctx",
    )(z, ftab)


def _fourier_tables(l):
    if l <= 256:
        m = (np.arange(l)[:, None] * np.arange(l)[None, :]) % l
        ang = 2.0 * np.pi * m / l
        return None, np.concatenate([np.cos(ang), np.sin(ang)], axis=1).astype(np.float32), 0
    n1 = l // DFT_N2
    m1 = (np.arange(n1)[:, None] * np.arange(n1)[None, :]) % n1
    a1 = 2.0 * np.pi * m1 / n1
    w1s = np.concatenate([np.cos(a1), np.sin(a1)], axis=0).astype(np.float32)
    kk = np.arange(n1)[:, None, None] + n1 * np.arange(DFT_N2)[None, :, None]
    m2 = (kk * np.arange(DFT_N2)[None, None, :]) % l
    a2 = 2.0 * np.pi * m2 / l
    ftab = np.concatenate([np.cos(a2), np.sin(a2)], axis=2).astype(np.float32)
    return w1s, ftab, n1


def _fourier_mix(z):
    b, l, _ = z.shape
    norm = float((l * FN_GW) ** -0.5)
    w1s, ftab, n1 = _fourier_tables(l)
    if w1s is None:
        return _dft_small_call(z, jnp.asarray(ftab).astype(BF16), norm)
    y = _fnet1_call(z.reshape(b, n1, DFT_N2 * 2 * GROUP_W), jnp.asarray(w1s).astype(BF16), n1)
    o = _fnet2_call(y.reshape(b, 2, n1, DFT_N2, GROUP_W), jnp.asarray(ftab).astype(BF16), n1, norm)
    return o.reshape(b, l, GROUP_W)


def _conv_kernel(prev_ref, cur_ref, next_ref, w_ref, b_ref, g_ref, beta_ref, o_ref, buf_ref):
    i = pl.program_id(1)
    n = pl.num_programs(1)
    t = cur_ref.shape[1]
    buf_ref[0:CV_HALO, :] = jnp.where(i > 0, prev_ref[0], 0.0)
    buf_ref[CV_HALO:CV_HALO + t, :] = cur_ref[0]
    buf_ref[CV_HALO + t:, :] = jnp.where(i < n - 1, next_ref[0], 0.0)
    w = w_ref[...]
    first = CV_HALO - CV_K // 2
    acc = None
    for res in range(SUBLANES):
        z = None
        for base in range(0, first + CV_K, SUBLANES):
            j = base + res - first
            if 0 <= j < CV_K:
                term = w[j:j + 1, :] * buf_ref[base:base + t + SUBLANES, :]
                z = term if z is None else z + term
        z = z[res:res + t, :]
        acc = z if acc is None else acc + z
    y = acc + b_ref[...]
    mu = jnp.mean(y, axis=-1, keepdims=True)
    var = jnp.mean(jnp.square(y - mu), axis=-1, keepdims=True)
    y = (y - mu) * lax.rsqrt(var + EPS) * g_ref[...] + beta_ref[...]
    o_ref[0] = (y * _sigmoid(y)).astype(BF16)


def _conv_call(cy, w_dw, b_dw, ln_g, ln_b):
    b, l, w = cy.shape
    t = min(l, 512)
    hb = t // CV_HALO
    n_halo = l // CV_HALO
    return pl.pallas_call(
        _conv_kernel,
        grid=(b, l // t),
        in_specs=[pl.BlockSpec((1, CV_HALO, w), lambda bi, i: (bi, jnp.maximum(i * hb - 1, 0), 0)),
                  pl.BlockSpec((1, t, w), lambda bi, i: (bi, i, 0)),
                  pl.BlockSpec((1, CV_HALO, w),
                               lambda bi, i: (bi, jnp.minimum((i + 1) * hb, n_halo - 1), 0)),
                  _const_spec(w_dw.shape), _const_spec(b_dw.shape), _const_spec(ln_g.shape),
                  _const_spec(ln_b.shape)],
        out_specs=pl.BlockSpec((1, t, w), lambda bi, i: (bi, i, 0)),
        out_shape=jax.ShapeDtypeStruct((b, l, w), BF16),
        scratch_shapes=[pltpu.VMEM((t + 2 * CV_HALO, w), F32)],
        compiler_params=_cparams(2),
        name="conformer_conv",
    )(cy, cy, cy, w_dw, b_dw, ln_g, ln_b)


def _outmlp_kernel(x_ref, oa_ref, oc_ref, ob_ref, od_ref, wo_ref, gt1_ref, sh2_ref, sc2_ref,
                   gt2_ref, g2_ref, w1_ref, w2_ref, fg_ref, o_ref, *, last, ff_chunk):
    y = (_dot(oa_ref[0], wo_ref[0:MLA_W, :])
         + _dot(oc_ref[0], wo_ref[MLA_W:MLA_W + GROUP_W, :])
         + _dot(ob_ref[0], wo_ref[MLA_W + GROUP_W:MLA_W + 2 * GROUP_W, :])
         + _dot(od_ref[0], wo_ref[MLA_W + 2 * GROUP_W:, :]))
    x1 = x_ref[0] + gt1_ref[0] * y
    h = (_rms(x1, g2_ref[...]) * (1.0 + sc2_ref[0]) + sh2_ref[0]).astype(BF16)
    ff = jnp.zeros(x1.shape, F32)
    for c in range(D_FF // ff_chunk):
        u = jnp.maximum(_dot(h, w1_ref[:, c * ff_chunk:(c + 1) * ff_chunk]), 0.0)
        ff = ff + _dot((u * u).astype(BF16), w2_ref[c * ff_chunk:(c + 1) * ff_chunk, :])
    x2 = x1 + gt2_ref[0] * ff
    if last:
        x2 = _rms(x2, fg_ref[...])
    o_ref[0] = x2


def _outmlp_call(x, o_mla, o_na, o_fn, o_cv, wo, gt1, sh2, sc2, gt2, g2, w1, w2, fg, last):
    b, l, d = x.shape
    t = min(l, 512)
    tok = lambda w: pl.BlockSpec((1, t, w), lambda bi, i: (bi, i, 0))
    vec = pl.BlockSpec((1, 1, d), lambda bi, i: (bi, 0, 0))
    return pl.pallas_call(
        functools.partial(_outmlp_kernel, last=last, ff_chunk=1024),
        grid=(b, l // t),
        in_specs=[tok(d), tok(MLA_W), tok(GROUP_W), tok(GROUP_W), tok(GROUP_W),
                  _const_spec(wo.shape), vec, vec, vec, vec, _const_spec(g2.shape),
                  _const_spec(w1.shape), _const_spec(w2.shape), _const_spec(fg.shape)],
        out_specs=tok(d),
        out_shape=jax.ShapeDtypeStruct((b, l, d), F32),
        compiler_params=_cparams(2),
        name="outproj_mlp",
    )(x, o_mla, o_na, o_fn, o_cv, wo, gt1, sh2, sc2, gt2, g2, w1, w2, fg)


def _rope_tables(l, with_rope):
    ones = jnp.ones((l, MLA_NOPE), F32)
    zeros = jnp.zeros((l, MLA_NOPE), F32)
    pad = jnp.zeros((l, HEAD_SLOT - MLA_QK), F32)
    if with_rope:
        t = jnp.arange(l)
        row = (t // GRID_W).astype(F32)
        col = (t % GRID_W).astype(F32)
        per_axis = MLA_ROPE // 2
        inv = ROPE_BASE ** (-jnp.arange(0, per_axis, 2, dtype=F32) / per_axis)
        ang = jnp.concatenate([row[:, None] * inv, col[:, None] * inv], axis=-1)
        cos, sin = jnp.cos(ang), jnp.sin(ang)
    else:
        cos = jnp.ones((l, MLA_ROPE // 2), F32)
        sin = jnp.zeros((l, MLA_ROPE // 2), F32)
    return jnp.concatenate([ones, cos, cos, pad, zeros, -sin, sin, pad], axis=-1)


def _swap_halves(w):
    half = w.shape[-1] // 2
    return jnp.concatenate([w[..., half:], w[..., :half]], axis=-1)


def _layer_weights(w_in, w_uq, w_ukv, w_out, rpb):
    d = w_in.shape[0]
    k_r_end = 256 + 128 + MLA_ROPE
    win = jnp.concatenate([w_in[:, :k_r_end], jnp.zeros((d, HEAD_SLOT - MLA_ROPE), F32),
                           w_in[:, k_r_end:]], axis=1).astype(BF16)

    rq = w_uq.shape[0]
    w3 = w_uq.reshape(rq, N_HEADS, MLA_QK)
    zpad = jnp.zeros((rq, N_HEADS, HEAD_SLOT - MLA_QK), F32)
    plain = jnp.concatenate([w3, zpad], axis=-1)
    swapped = jnp.concatenate([jnp.zeros((rq, N_HEADS, MLA_NOPE), F32),
                               _swap_halves(w3[..., MLA_NOPE:]), zpad], axis=-1)
    wq = jnp.concatenate([plain.reshape(rq, MLA_W), swapped.reshape(rq, MLA_W)], axis=1).astype(BF16)

    rkv = w_ukv.shape[0]
    u3 = w_ukv.reshape(rkv, N_HEADS, MLA_NOPE + MLA_V)
    eye = jnp.eye(MLA_ROPE, dtype=F32)
    slot_pad_r = jnp.zeros((MLA_ROPE, N_HEADS, HEAD_SLOT - MLA_QK), F32)

    def slots(top, rope_block):
        top = jnp.concatenate([top, jnp.zeros((rkv, N_HEADS, HEAD_SLOT - top.shape[-1]), F32)], axis=-1)
        mid = jnp.concatenate([jnp.zeros((MLA_ROPE, N_HEADS, MLA_NOPE), F32),
                               jnp.broadcast_to(rope_block[:, None, :], (MLA_ROPE, N_HEADS, MLA_ROPE)),
                               slot_pad_r], axis=-1)
        bot = jnp.zeros((2 * HEAD_SLOT - rkv - MLA_ROPE, N_HEADS, HEAD_SLOT), F32)
        return jnp.concatenate([top, mid, bot], axis=0).reshape(2 * HEAD_SLOT, MLA_W)

    k_plain = slots(u3[..., :MLA_NOPE], eye)
    k_swap = slots(jnp.zeros((rkv, N_HEADS, MLA_NOPE), F32), _swap_halves(eye))
    wkv = jnp.concatenate([k_plain, k_swap], axis=1).astype(BF16)

    vt3 = jnp.transpose(u3[..., MLA_NOPE:], (1, 2, 0))
    vt3 = jnp.pad(vt3, ((0, 0), (0, VT_ROWS - MLA_V), (0, 2 * HEAD_SLOT - rkv)))
    wvt = vt3.reshape(N_HEADS * VT_ROWS, 2 * HEAD_SLOT).astype(BF16)

    o3 = w_out[:N_HEADS * MLA_V].reshape(N_HEADS, MLA_V, d)
    o3 = jnp.concatenate([o3, jnp.zeros((N_HEADS, HEAD_SLOT - MLA_V, d), F32)], axis=1)
    wo = jnp.concatenate([o3.reshape(MLA_W, d), w_out[N_HEADS * MLA_V:]], axis=0).astype(BF16)

    qc = np.arange(GRID_W)[:, None]
    kc = np.arange(GRID_W)[None, :]
    ws = np.clip(qc - NA_WIN_W // 2, 0, GRID_W - NA_WIN_W)
    in_win = (kc >= ws) & (kc < ws + NA_WIN_W)
    edge = GRID_W - NA_WIN_W
    rp = jnp.pad(rpb.astype(F32), ((0, 0), (0, 0), (edge, edge)))
    toep = jnp.stack([rp[:, :, GRID_W - 1 - q:2 * GRID_W - 1 - q] for q in range(GRID_W)], axis=2)
    t2 = jnp.where(in_win[None, None], toep * LOG2_E, NEG_INF)
    bias = jnp.concatenate([t2[:, :-1], t2[:, 1:]], axis=-1)
    return win, wq, wkv, wvt, wo, bias


def _channel_dft():
    m = (np.arange(FN_GW)[:, None] * np.arange(FN_GW)[None, :]) % FN_GW
    ang = 2.0 * np.pi * m / FN_GW
    eye = np.eye(FN_GROUPS)
    return np.concatenate([np.kron(eye, np.cos(ang)), -np.kron(eye, np.sin(ang))],
                          axis=1).astype(np.float32)


def kernel(x, c, ctx, c_ctx, w_mod, b_mod, norm1_g, norm2_g, w_in, mla_q_norm, mla_w_uq, mla_kv_norm, mla_w_ukv, na_rpb, cv_w_dw, cv_b_dw, cv_ln_g, cv_ln_b, w_out, w_ff1, w_ff2, final_g):
    depth = w_mod.shape[0]
    b, s, d = x.shape
    n_ctx = ctx.shape[1]

    cc = jnp.concatenate([c, c_ctx[None, :], jnp.zeros((8 - b - 1, d), F32)], axis=0)
    mods = _mod_call(cc, w_mod, b_mod)

    cs_lat = _rope_tables(s, True)
    cs_ctx = _rope_tables(n_ctx, False)
    dc = jnp.asarray(_channel_dft()).astype(BF16)
    row = lambda p: p.reshape(1, -1)
    fg = row(final_g)

    xc = ctx
    for i in range(depth):
        last = i == depth - 1
        win, wq, wkv, wvt, wo, bias = _layer_weights(w_in[i], mla_w_uq[i], mla_w_ukv[i], w_out[i], na_rpb[i])
        w1 = w_ff1[i].astype(BF16)
        w2 = w_ff2[i].astype(BF16)
        mx = [m.reshape(b, 1, d) for m in jnp.split(mods[i, :b], 6, axis=-1)]
        mc = [jnp.broadcast_to(m.reshape(1, 1, d), (b, 1, d))
              for m in jnp.split(mods[i, b:b + 1], 6, axis=-1)]
        proj = functools.partial(_inproj_call, g1=row(norm1_g[i]), win=win, qn=row(mla_q_norm[i]),
                                 wq=wq, kvn=row(mla_kv_norm[i]), wkv=wkv, wvt=wvt, dc=dc)
        conv = functools.partial(_conv_call, w_dw=cv_w_dw[i], b_dw=row(cv_b_dw[i]),
                                 ln_g=row(cv_ln_g[i]), ln_b=row(cv_ln_b[i]))
        mlp = functools.partial(_outmlp_call, wo=wo, g2=row(norm2_g[i]), w1=w1, w2=w2, fg=fg)

        q, k, vt, nq, nk, nv, z, cy = proj(x, mx[0], mx[1], cs=cs_lat)
        cq, ck, cvt, cnq, cnk, cnv, cz, ccy = proj(xc, mc[0], mc[1], cs=cs_ctx)

        o_mla = _flash_call(q, k, vt, ck, cvt)
        o_na = _natten_call(nq, nk, nv, cnk, cnv, bias)
        o_fn = _fourier_mix(z)
        o_cv = conv(cy)
        x = mlp(x, o_mla, o_na, o_fn, o_cv, gt1=mx[2], sh2=mx[3], sc2=mx[4], gt2=mx[5], last=last)

        if not last:
            co_mla = _flash_call(cq, ck, cvt)
            co_na = _na_ctx_call(cnq, cnk, cnv)
            co_fn = _fourier_mix(cz)
            co_cv = conv(ccy)
            xc = mlp(xc, co_mla, co_na, co_fn, co_cv, gt1=mc[2], sh2=mc[3], sc2=mc[4], gt2=mc[5],
                     last=False)
    return x
```

```python
import functools

import jax
import jax.numpy as jnp
import numpy as np
from jax import lax
from jax.experimental import pallas as pl
from jax.experimental.pallas import tpu as pltpu

F32 = jnp.float32
BF16 = jnp.bfloat16

EPS = 1e-6
ROPE_BASE = 10000.0
NEG_INF = -1e30

D_MODEL = 1024
GRID_W = 64
N_HEADS = 4
GROUP_W = 256
HEAD_SLOT = 128
MLA_NOPE = 64
MLA_ROPE = 32
MLA_QK = MLA_NOPE + MLA_ROPE
MLA_V = 64
MLA_W = N_HEADS * HEAD_SLOT
VT_ROWS = 80
LOG2_E = 1.4426950408889634
NA_HD = 64
NA_WIN_H = 8
NA_WIN_W = 16
FN_GROUPS = 4
FN_GW = GROUP_W // FN_GROUPS
CV_K = 31
CV_HALO = 16
SUBLANES = 8
D_FF = 4 * D_MODEL
DFT_N2 = 128
V7X_VMEM_LIMIT = 56 * 1024 * 1024


def _cparams(n_grid):
    return pltpu.CompilerParams(dimension_semantics=("parallel",) * n_grid,
                                vmem_limit_bytes=V7X_VMEM_LIMIT)


def _const_spec(shape):
    nd = len(shape)
    return pl.BlockSpec(shape, lambda *_: (0,) * nd, pipeline_mode=pl.Buffered(1))


def _sigmoid(v):
    return 1.0 / (1.0 + jnp.exp(-v))


def _rms(v, g):
    return v * lax.rsqrt(jnp.mean(v * v, axis=-1, keepdims=True) + EPS) * g


def _dot(a, b):
    return jnp.dot(a, b, preferred_element_type=F32)


def _dot_nt(a, b):
    return lax.dot_general(a, b, (((1,), (1,)), ((), ())), preferred_element_type=F32)


def _mod_kernel(c_ref, w_ref, b_ref, o_ref):
    c = c_ref[...]
    s = (c * _sigmoid(c)).astype(BF16)
    o_ref[0] = _dot(s, w_ref[0].astype(BF16)) + b_ref[0]


def _mod_call(cc, w_mod, b_mod):
    depth, d, n = w_mod.shape
    tn = 1536
    return pl.pallas_call(
        _mod_kernel,
        grid=(depth, n // tn),
        in_specs=[pl.BlockSpec(cc.shape, lambda l, j: (0, 0)),
                  pl.BlockSpec((1, d, tn), lambda l, j: (l, 0, j)),
                  pl.BlockSpec((1, 1, tn), lambda l, j: (l, 0, j))],
        out_specs=pl.BlockSpec((1, cc.shape[0], tn), lambda l, j: (l, 0, j)),
        out_shape=jax.ShapeDtypeStruct((depth, cc.shape[0], n), F32),
        compiler_params=_cparams(2),
        name="mod",
    )(cc, w_mod, b_mod.reshape(depth, 1, n))


def _inproj_kernel(x_ref, sh_ref, sc_ref, g1_ref, win_ref, qn_ref, wq_ref, kvn_ref, wkv_ref,
                   wvt_ref, dc_ref, cs_ref,
                   q_ref, k_ref, vt_ref, nq_ref, nk_ref, nv_ref, z_ref, cy_ref):
    x = x_ref[0]
    h = _rms(x, g1_ref[...]) * (1.0 + sc_ref[0]) + sh_ref[0]
    p = _dot(h.astype(BF16), win_ref[...])

    cs = cs_ref[...]
    cos_t, sin_t = cs[:, :HEAD_SLOT], cs[:, HEAD_SLOT:]
    cos4 = jnp.concatenate([cos_t] * N_HEADS, axis=-1)
    sin4 = jnp.concatenate([sin_t] * N_HEADS, axis=-1)

    r = _rms(p[:, 0:256], qn_ref[...]).astype(BF16)
    qq = _dot(r, wq_ref[...])
    q = (qq[:, :MLA_W] * cos4 + qq[:, MLA_W:] * sin4) * (MLA_QK ** -0.5 * LOG2_E)
    q_ref[0] = q.astype(BF16)

    kvr = p[:, 256:512]
    kvn = _rms(kvr[:, :128], kvn_ref[...])
    comb = jnp.concatenate([kvn, kvr[:, 128:]], axis=-1).astype(BF16)
    kk = _dot(comb, wkv_ref[...])
    k = kk[:, :MLA_W] * cos4 + kk[:, MLA_W:] * sin4
    k_ref[0] = k.astype(BF16)
    vt = _dot_nt(wvt_ref[...], comb)
    sub = lax.broadcasted_iota(jnp.int32, (vt.shape[0], 1), 0)
    ones_row = jnp.where(sub % VT_ROWS == MLA_V, 1.0, 0.0).astype(F32)
    vt_ref[0] = (vt + ones_row).astype(BF16)

    nq_ref[0] = (p[:, 512:768] * (NA_HD ** -0.5 * LOG2_E)).astype(BF16)
    nk_ref[0] = p[:, 768:1024].astype(BF16)
    nv_ref[0] = p[:, 1024:1280].astype(BF16)

    z_ref[0] = _dot(p[:, 1280:1536].astype(BF16), dc_ref[...]).astype(BF16)

    cy_ref[0] = p[:, 1536:1792] * _sigmoid(p[:, 1792:2048])


def _inproj_call(x, sh, sc, g1, win, qn, wq, kvn, wkv, wvt, dc, cs):
    b, l, d = x.shape
    t = min(l, 512)
    tok = lambda w: pl.BlockSpec((1, t, w), lambda bi, i: (bi, i, 0))
    vec = pl.BlockSpec((1, 1, d), lambda bi, i: (bi, 0, 0))
    outs = [(MLA_W, BF16), (MLA_W, BF16), None, (GROUP_W, BF16), (GROUP_W, BF16),
            (GROUP_W, BF16), (2 * GROUP_W, BF16), (GROUP_W, F32)]
    vt_rows = N_HEADS * VT_ROWS
    out_specs = [tok(o[0]) if o else pl.BlockSpec((1, vt_rows, t), lambda bi, i: (bi, 0, i))
                 for o in outs]
    out_shape = [jax.ShapeDtypeStruct((b, l, o[0]), o[1]) if o
                 else jax.ShapeDtypeStruct((b, vt_rows, l), BF16) for o in outs]
    return pl.pallas_call(
        _inproj_kernel,
        grid=(b, l // t),
        in_specs=[tok(d), vec, vec, _const_spec(g1.shape), _const_spec(win.shape),
                  _const_spec(qn.shape), _const_spec(wq.shape), _const_spec(kvn.shape),
                  _const_spec(wkv.shape), _const_spec(wvt.shape), _const_spec(dc.shape),
                  pl.BlockSpec((t, 2 * HEAD_SLOT), lambda bi, i: (i, 0))],
        out_specs=out_specs,
        out_shape=out_shape,
        compiler_params=_cparams(2),
        name="inproj",
    )(x, sh, sc, g1, win, qn, wq, kvn, wkv, wvt, dc, cs)


def _flash_kernel(*refs, n_chunks, tk, has_extra):
    if has_extra:
        q_ref, k_ref, vt_ref, k2_ref, vt2_ref, o_ref, sa_ref, sb_ref, sx_ref = refs
    else:
        q_ref, k_ref, vt_ref, o_ref, sa_ref, sb_ref = refs
    q = q_ref[0]
    tq = q.shape[0]

    def scores(kc, s_ref):
        st = _dot_nt(kc, q)
        s_ref[...] = st
        return jnp.max(st, axis=0, keepdims=True)

    def accumulate(s_ref, mx, vtc, carry):
        m, acc = carry
        m_new = jnp.maximum(m, mx)
        p = jnp.exp2(s_ref[...] - m_new).astype(BF16)
        return m_new, jnp.exp2(m - m_new) * acc + _dot(vtc, p)

    def k_chunk(c):
        return k_ref[0, pl.ds(pl.multiple_of(c * tk, tk), tk), :]

    def vt_chunk(c):
        return vt_ref[0, :, pl.ds(pl.multiple_of(c * tk, tk), tk)]

    def pair(j, carry):
        mx_a, m, acc = carry
        mx_b = scores(k_chunk(2 * j + 1), sb_ref)
        m, acc = accumulate(sa_ref, mx_a, vt_chunk(2 * j), (m, acc))
        mx_a = scores(k_chunk(2 * j + 2), sa_ref)
        m, acc = accumulate(sb_ref, mx_b, vt_chunk(2 * j + 1), (m, acc))
        return mx_a, m, acc

    n_pairs = (n_chunks - 1) // 2
    carry = (scores(k_chunk(0), sa_ref), jnp.full((1, tq), -jnp.inf, F32),
             jnp.zeros((VT_ROWS, tq), F32))
    mx, m, acc = lax.fori_loop(0, n_pairs, pair, carry)
    pending = (sa_ref, mx, vt_chunk(2 * n_pairs))
    tail = [(k_chunk(c), vt_chunk(c), sb_ref if c % 2 else sa_ref)
            for c in range(2 * n_pairs + 1, n_chunks)]
    if has_extra:
        tail.append((k2_ref[0], vt2_ref[0], sx_ref))
    for kc, vtc, s_ref in tail:
        mx_next = scores(kc, s_ref)
        m, acc = accumulate(*pending, (m, acc))
        pending = (s_ref, mx_next, vtc)
    m, acc = accumulate(*pending, (m, acc))

    o_t = acc / acc[MLA_V:MLA_V + 1, :]
    o_t = jnp.concatenate([o_t, jnp.zeros((HEAD_SLOT - VT_ROWS, tq), F32)], axis=0)
    o_ref[0] = o_t.T.astype(BF16)


def _flash_call(q, k, vt, k2=None, vt2=None):
    b, lq, _ = q.shape
    lk = k.shape[1]
    tq = min(lq, 2048)
    tk = min(lk, 1024)
    has_extra = k2 is not None
    qspec = pl.BlockSpec((1, tq, HEAD_SLOT), lambda bi, h, i: (bi, i, h))
    kspec = lambda n: pl.BlockSpec((1, n, HEAD_SLOT), lambda bi, h, i: (bi, 0, h))
    vspec = lambda n: pl.BlockSpec((1, VT_ROWS, n), lambda bi, h, i: (bi, h, 0))
    in_specs = [qspec, kspec(lk), vspec(lk)]
    args = [q, k, vt]
    scratch = [pltpu.VMEM((tk, tq), F32), pltpu.VMEM((tk, tq), F32)]
    if has_extra:
        in_specs += [kspec(k2.shape[1]), vspec(k2.shape[1])]
        args += [k2, vt2]
        scratch.append(pltpu.VMEM((k2.shape[1], tq), F32))
    return pl.pallas_call(
        functools.partial(_flash_kernel, n_chunks=lk // tk, tk=tk, has_extra=has_extra),
        grid=(b, N_HEADS, lq // tq),
        in_specs=in_specs,
        out_specs=qspec,
        out_shape=jax.ShapeDtypeStruct((b, lq, MLA_W), BF16),
        scratch_shapes=scratch,
        compiler_params=_cparams(3),
        name="mla_attn",
    )(*args)


def _head_stack(q):
    lane = lax.broadcasted_iota(jnp.int32, q.shape, 1)
    return jnp.concatenate(
        [jnp.where(lane // NA_HD == h, q, jnp.zeros_like(q)) for h in range(N_HEADS)], axis=0)


def _head_unstack(o, n):
    lane = lax.broadcasted_iota(jnp.int32, (n, GROUP_W), 1)
    out = jnp.zeros((n, GROUP_W), F32)
    for h in range(N_HEADS):
        out = out + jnp.where(lane // NA_HD == h, o[h * n:(h + 1) * n], 0.0)
    return out


def _natten_kernel(q_ref, k_ref, v_ref, kc_ref, vc_ref, bias_ref, o_ref, *, rows_per_step, rows):
    blk = pl.program_id(1)
    kc = kc_ref[0]
    vc = vc_ref[0]
    n_loc = NA_WIN_H * GRID_W

    def body(j, _):
        r = blk * rows_per_step + j
        rs = jnp.clip(r - NA_WIN_H // 2, 0, rows - NA_WIN_H)
        d0 = rs - r + (NA_WIN_H - 1)
        qs = _head_stack(q_ref[0, pl.ds(pl.multiple_of(j * GRID_W, GRID_W), GRID_W), :])
        kstart = pl.multiple_of(rs * GRID_W, GRID_W)
        s_loc = _dot_nt(qs, k_ref[0, pl.ds(kstart, n_loc), :])
        bias = jnp.concatenate(
            [jnp.concatenate([bias_ref[h, d0 + 2 * w] for w in range(NA_WIN_H // 2)], axis=-1)
             for h in range(N_HEADS)], axis=0)
        s_loc = s_loc + bias
        s_ctx = _dot_nt(qs, kc)
        m = jnp.maximum(jnp.max(s_loc, axis=-1, keepdims=True),
                        jnp.max(s_ctx, axis=-1, keepdims=True))
        p_loc = jnp.exp2(s_loc - m)
        p_ctx = jnp.exp2(s_ctx - m)
        denom = jnp.sum(p_loc, axis=-1, keepdims=True) + jnp.sum(p_ctx, axis=-1, keepdims=True)
        o = (_dot(p_loc.astype(BF16), v_ref[0, pl.ds(kstart, n_loc), :])
             + _dot(p_ctx.astype(BF16), vc))
        o = _head_unstack(o / denom, GRID_W)
        o_ref[0, pl.ds(pl.multiple_of(j * GRID_W, GRID_W), GRID_W), :] = o.astype(BF16)
        return 0

    lax.fori_loop(0, rows_per_step, body, 0, unroll=True)


def _natten_call(q, k, v, kc, vc, bias):
    b, l, w = q.shape
    rows = l // GRID_W
    rows_per_step = 16
    t = rows_per_step * GRID_W
    full = lambda n: pl.BlockSpec((1, n, w), lambda bi, i: (bi, 0, 0))
    return pl.pallas_call(
        functools.partial(_natten_kernel, rows_per_step=rows_per_step, rows=rows),
        grid=(b, rows // rows_per_step),
        in_specs=[pl.BlockSpec((1, t, w), lambda bi, i: (bi, i, 0)),
                  full(l), full(l), full(kc.shape[1]), full(kc.shape[1]),
                  _const_spec(bias.shape)],
        out_specs=pl.BlockSpec((1, t, w), lambda bi, i: (bi, i, 0)),
        out_shape=jax.ShapeDtypeStruct((b, l, w), BF16),
        compiler_params=_cparams(2),
        name="natten",
    )(q, k, v, kc, vc, bias)


def _na_ctx_kernel(q_ref, k_ref, v_ref, o_ref):
    n = q_ref.shape[1]
    s = _dot_nt(_head_stack(q_ref[0]), k_ref[0])
    m = jnp.max(s, axis=-1, keepdims=True)
    p = jnp.exp2(s - m)
    denom = jnp.sum(p, axis=-1, keepdims=True)
    o = _dot(p.astype(BF16), v_ref[0]) / denom
    o_ref[0] = _head_unstack(o, n).astype(BF16)


def _na_ctx_call(q, k, v):
    b, n, w = q.shape
    spec = pl.BlockSpec((1, n, w), lambda bi: (bi, 0, 0))
    return pl.pallas_call(
        _na_ctx_kernel, grid=(b,), in_specs=[spec, spec, spec], out_specs=spec,
        out_shape=jax.ShapeDtypeStruct((b, n, w), BF16),
        compiler_params=_cparams(1), name="na_ctx",
    )(q, k, v)


def _fnet1_kernel(z_ref, w_ref, y_ref, *, n_inner):
    p = _dot(w_ref[...], z_ref[0])
    n1 = p.shape[0] // 2
    for j in range(n_inner):
        zr_c = p[:n1, j * 512:j * 512 + 256]
        zi_c = p[:n1, j * 512 + 256:(j + 1) * 512]
        zr_s = p[n1:, j * 512:j * 512 + 256]
        zi_s = p[n1:, j * 512 + 256:(j + 1) * 512]
        y_ref[0, 0, :, j * 256:(j + 1) * 256] = (zr_c + zi_s).astype(BF16)
        y_ref[0, 1, :, j * 256:(j + 1) * 256] = (zi_c - zr_s).astype(BF16)


def _fnet1_call(z2, w1s, n1):
    b = z2.shape[0]
    n_inner = 16
    return pl.pallas_call(
        functools.partial(_fnet1_kernel, n_inner=n_inner),
        grid=(b, DFT_N2 // n_inner),
        in_specs=[pl.BlockSpec((1, n1, n_inner * 512), lambda bi, i: (bi, 0, i)),
                  _const_spec(w1s.shape)],
        out_specs=pl.BlockSpec((1, 2, n1, n_inner * 256), lambda bi, i: (bi, 0, 0, i)),
        out_shape=jax.ShapeDtypeStruct((b, 2, n1, DFT_N2 * GROUP_W), BF16),
        compiler_params=_cparams(2),
        name="fnet_stage1",
    )(z2, w1s)


def _fnet2_kernel(y_ref, f_ref, o_ref, *, n_inner, norm):
    for j in range(n_inner):
        f = f_ref[j]
        o = _dot(f[:, :DFT_N2], y_ref[0, 0, j]) + _dot(f[:, DFT_N2:], y_ref[0, 1, j])
        o_ref[0, :, j * GROUP_W:(j + 1) * GROUP_W] = (o * norm).astype(BF16)


def _fnet2_call(y5, ftab, n1, norm):
    b = y5.shape[0]
    n_inner = 16
    return pl.pallas_call(
        functools.partial(_fnet2_kernel, n_inner=n_inner, norm=norm),
        grid=(b, n1 // n_inner),
        in_specs=[pl.BlockSpec((1, 2, n_inner, DFT_N2, GROUP_W), lambda bi, i: (bi, 0, i, 0, 0)),
                  pl.BlockSpec((n_inner, DFT_N2, 2 * DFT_N2), lambda bi, i: (i, 0, 0))],
        out_specs=pl.BlockSpec((1, DFT_N2, n_inner * GROUP_W), lambda bi, i: (bi, 0, i)),
        out_shape=jax.ShapeDtypeStruct((b, DFT_N2, n1 * GROUP_W), BF16),
        compiler_params=_cparams(2),
        name="fnet_stage2",
    )(y5, ftab)


def _dft_small_kernel(z_ref, f_ref, o_ref, *, norm):
    z = z_ref[0]
    f = f_ref[...]
    n = z.shape[0]
    o = _dot(f[:, :n], z[:, :GROUP_W]) + _dot(f[:, n:], z[:, GROUP_W:])
    o_ref[0] = (o * norm).astype(BF16)


def _dft_small_call(z, ftab, norm):
    b, n, _ = z.shape
    return pl.pallas_call(
        functools.partial(_dft_small_kernel, norm=norm),
        grid=(b,),
        in_specs=[pl.BlockSpec((1, n, 2 * GROUP_W), lambda bi: (bi, 0, 0)), _const_spec(ftab.shape)],
        out_specs=pl.BlockSpec((1, n, GROUP_W), lambda bi: (bi, 0, 0)),
        out_shape=jax.ShapeDtypeStruct((b, n, GROUP_W), BF16),
        compiler_params=_cparams(1), name="fnet_ctx",
    )(z, ftab)


def _fourier_tables(l):
    if l <= 256:
        m = (np.arange(l)[:, None] * np.arange(l)[None, :]) % l
        ang = 2.0 * np.pi * m / l
        return None, np.concatenate([np.cos(ang), np.sin(ang)], axis=1).astype(np.float32), 0
    n1 = l // DFT_N2
    m1 = (np.arange(n1)[:, None] * np.arange(n1)[None, :]) % n1
    a1 = 2.0 * np.pi * m1 / n1
    w1s = np.concatenate([np.cos(a1), np.sin(a1)], axis=0).astype(np.float32)
    kk = np.arange(n1)[:, None, None] + n1 * np.arange(DFT_N2)[None, :, None]
    m2 = (kk * np.arange(DFT_N2)[None, None, :]) % l
    a2 = 2.0 * np.pi * m2 / l
    ftab = np.concatenate([np.cos(a2), np.sin(a2)], axis=2).astype(np.float32)
    return w1s, ftab, n1


def _fourier_mix(z):
    b, l, _ = z.shape
    norm = float((l * FN_GW) ** -0.5)
    w1s, ftab, n1 = _fourier_tables(l)
    if w1s is None:
        return _dft_small_call(z, jnp.asarray(ftab).astype(BF16), norm)
    y = _fnet1_call(z.reshape(b, n1, DFT_N2 * 2 * GROUP_W), jnp.asarray(w1s).astype(BF16), n1)
    o = _fnet2_call(y.reshape(b, 2, n1, DFT_N2, GROUP_W), jnp.asarray(ftab).astype(BF16), n1, norm)
    return o.reshape(b, l, GROUP_W)


def _conv_kernel(prev_ref, cur_ref, next_ref, w_ref, b_ref, g_ref, beta_ref, o_ref, buf_ref):
    i = pl.program_id(1)
    n = pl.num_programs(1)
    t = cur_ref.shape[1]
    buf_ref[0:CV_HALO, :] = jnp.where(i > 0, prev_ref[0], 0.0)
    buf_ref[CV_HALO:CV_HALO + t, :] = cur_ref[0]
    buf_ref[CV_HALO + t:, :] = jnp.where(i < n - 1, next_ref[0], 0.0)
    w = w_ref[...]
    first = CV_HALO - CV_K // 2
    acc = None
    for res in range(SUBLANES):
        z = None
        for base in range(0, first + CV_K, SUBLANES):
            j = base + res - first
            if 0 <= j < CV_K:
                term = w[j:j + 1, :] * buf_ref[base:base + t + SUBLANES, :]
                z = term if z is None else z + term
        z = z[res:res + t, :]
        acc = z if acc is None else acc + z
    y = acc + b_ref[...]
    mu = jnp.mean(y, axis=-1, keepdims=True)
    var = jnp.mean(jnp.square(y - mu), axis=-1, keepdims=True)
    y = (y - mu) * lax.rsqrt(var + EPS) * g_ref[...] + beta_ref[...]
    o_ref[0] = (y * _sigmoid(y)).astype(BF16)


def _conv_call(cy, w_dw, b_dw, ln_g, ln_b):
    b, l, w = cy.shape
    t = min(l, 512)
    hb = t // CV_HALO
    n_halo = l // CV_HALO
    return pl.pallas_call(
        _conv_kernel,
        grid=(b, l // t),
        in_specs=[pl.BlockSpec((1, CV_HALO, w), lambda bi, i: (bi, jnp.maximum(i * hb - 1, 0), 0)),
                  pl.BlockSpec((1, t, w), lambda bi, i: (bi, i, 0)),
                  pl.BlockSpec((1, CV_HALO, w),
                               lambda bi, i: (bi, jnp.minimum((i + 1) * hb, n_halo - 1), 0)),
                  _const_spec(w_dw.shape), _const_spec(b_dw.shape), _const_spec(ln_g.shape),
                  _const_spec(ln_b.shape)],
        out_specs=pl.BlockSpec((1, t, w), lambda bi, i: (bi, i, 0)),
        out_shape=jax.ShapeDtypeStruct((b, l, w), BF16),
        scratch_shapes=[pltpu.VMEM((t + 2 * CV_HALO, w), F32)],
        compiler_params=_cparams(2),
        name="conformer_conv",
    )(cy, cy, cy, w_dw, b_dw, ln_g, ln_b)


def _outmlp_kernel(x_ref, oa_ref, oc_ref, ob_ref, od_ref, wo_ref, gt1_ref, sh2_ref, sc2_ref,
                   gt2_ref, g2_ref, w1_ref, w2_ref, fg_ref, o_ref, *, last, ff_chunk):
    y = (_dot(oa_ref[0], wo_ref[0:MLA_W, :])
         + _dot(oc_ref[0], wo_ref[MLA_W:MLA_W + GROUP_W, :])
         + _dot(ob_ref[0], wo_ref[MLA_W + GROUP_W:MLA_W + 2 * GROUP_W, :])
         + _dot(od_ref[0], wo_ref[MLA_W + 2 * GROUP_W:, :]))
    x1 = x_ref[0] + gt1_ref[0] * y
    h = (_rms(x1, g2_ref[...]) * (1.0 + sc2_ref[0]) + sh2_ref[0]).astype(BF16)
    ff = jnp.zeros(x1.shape, F32)
    for c in range(D_FF // ff_chunk):
        u = jnp.maximum(_dot(h, w1_ref[0, :, c * ff_chunk:(c + 1) * ff_chunk]), 0.0)
        ff = ff + _dot((u * u).astype(BF16), w2_ref[0, c * ff_chunk:(c + 1) * ff_chunk, :])
    x2 = x1 + gt2_ref[0] * ff
    if last:
        x2 = _rms(x2, fg_ref[...])
    o_ref[0] = x2


def _outmlp_call(x, o_mla, o_na, o_fn, o_cv, wo, gt1, sh2, sc2, gt2, g2, w1, w2, fg, last, layer):
    b, l, d = x.shape
    t = min(l, 512)
    tok = lambda w: pl.BlockSpec((1, t, w), lambda bi, i: (bi, i, 0))
    vec = pl.BlockSpec((1, 1, d), lambda bi, i: (bi, 0, 0))
    slab = lambda w: pl.BlockSpec((1,) + w.shape[1:], lambda bi, i: (layer, 0, 0),
                                  pipeline_mode=pl.Buffered(1))
    return pl.pallas_call(
        functools.partial(_outmlp_kernel, last=last, ff_chunk=1024),
        grid=(b, l // t),
        in_specs=[tok(d), tok(MLA_W), tok(GROUP_W), tok(GROUP_W), tok(GROUP_W),
                  _const_spec(wo.shape), vec, vec, vec, vec, _const_spec(g2.shape),
                  slab(w1), slab(w2), _const_spec(fg.shape)],
        out_specs=tok(d),
        out_shape=jax.ShapeDtypeStruct((b, l, d), F32),
        compiler_params=_cparams(2),
        name="outproj_mlp",
    )(x, o_mla, o_na, o_fn, o_cv, wo, gt1, sh2, sc2, gt2, g2, w1, w2, fg)


def _rope_tables(l, with_rope):
    per_axis = MLA_ROPE // 2
    n_freq = per_axis // 2

    def slot(nope, row_part, col_part, sign):
        n = row_part.shape[0]
        return jnp.concatenate([jnp.full((n, MLA_NOPE), nope, F32), sign * row_part, sign * col_part,
                                row_part, col_part, jnp.zeros((n, HEAD_SLOT - MLA_QK), F32)], axis=-1)

    if not with_rope:
        one, zero = jnp.ones((l, n_freq), F32), jnp.zeros((l, n_freq), F32)
        return jnp.concatenate([slot(1.0, one, one, 1.0), slot(0.0, zero, zero, -1.0)], axis=-1)
    inv = ROPE_BASE ** (-jnp.arange(0, per_axis, 2, dtype=F32) / per_axis)
    r_ang = jnp.arange(l // GRID_W).astype(F32)[:, None] * inv
    c_ang = jnp.arange(GRID_W).astype(F32)[:, None] * inv
    zr, zc = jnp.zeros_like(r_ang), jnp.zeros_like(c_ang)
    by_row = jnp.concatenate([slot(0.0, jnp.cos(r_ang), zr, 1.0), slot(0.0, jnp.sin(r_ang), zr, -1.0)], axis=-1)
    by_col = jnp.concatenate([slot(1.0, zc, jnp.cos(c_ang), 1.0), slot(0.0, zc, jnp.sin(c_ang), -1.0)], axis=-1)
    return (by_row[:, None, :] + by_col[None, :, :]).reshape(l, 2 * HEAD_SLOT)


def _swap_halves(w):
    half = w.shape[-1] // 2
    return jnp.concatenate([w[..., half:], w[..., :half]], axis=-1)


def _layer_weights(w_in, w_uq, w_ukv, w_out, rpb):
    d = w_in.shape[0]
    k_r_end = 256 + 128 + MLA_ROPE
    win = jnp.concatenate([w_in[:, :k_r_end], jnp.zeros((d, HEAD_SLOT - MLA_ROPE), F32),
                           w_in[:, k_r_end:]], axis=1).astype(BF16)

    rq = w_uq.shape[0]
    w3 = w_uq.reshape(rq, N_HEADS, MLA_QK)
    zpad = jnp.zeros((rq, N_HEADS, HEAD_SLOT - MLA_QK), F32)
    plain = jnp.concatenate([w3, zpad], axis=-1)
    swapped = jnp.concatenate([jnp.zeros((rq, N_HEADS, MLA_NOPE), F32),
                               _swap_halves(w3[..., MLA_NOPE:]), zpad], axis=-1)
    wq = jnp.concatenate([plain.reshape(rq, MLA_W), swapped.reshape(rq, MLA_W)], axis=1).astype(BF16)

    rkv = w_ukv.shape[0]
    u3 = w_ukv.reshape(rkv, N_HEADS, MLA_NOPE + MLA_V)
    eye = jnp.eye(MLA_ROPE, dtype=F32)
    slot_pad_r = jnp.zeros((MLA_ROPE, N_HEADS, HEAD_SLOT - MLA_QK), F32)

    def slots(top, rope_block):
        top = jnp.concatenate([top, jnp.zeros((rkv, N_HEADS, HEAD_SLOT - top.shape[-1]), F32)], axis=-1)
        mid = jnp.concatenate([jnp.zeros((MLA_ROPE, N_HEADS, MLA_NOPE), F32),
                               jnp.broadcast_to(rope_block[:, None, :], (MLA_ROPE, N_HEADS, MLA_ROPE)),
                               slot_pad_r], axis=-1)
        bot = jnp.zeros((2 * HEAD_SLOT - rkv - MLA_ROPE, N_HEADS, HEAD_SLOT), F32)
        return jnp.concatenate([top, mid, bot], axis=0).reshape(2 * HEAD_SLOT, MLA_W)

    k_plain = slots(u3[..., :MLA_NOPE], eye)
    k_swap = slots(jnp.zeros((rkv, N_HEADS, MLA_NOPE), F32), _swap_halves(eye))
    wkv = jnp.concatenate([k_plain, k_swap], axis=1).astype(BF16)

    vt3 = jnp.transpose(u3[..., MLA_NOPE:], (1, 2, 0))
    vt3 = jnp.pad(vt3, ((0, 0), (0, VT_ROWS - MLA_V), (0, 2 * HEAD_SLOT - rkv)))
    wvt = vt3.reshape(N_HEADS * VT_ROWS, 2 * HEAD_SLOT).astype(BF16)

    o3 = w_out[:N_HEADS * MLA_V].reshape(N_HEADS, MLA_V, d)
    o3 = jnp.concatenate([o3, jnp.zeros((N_HEADS, HEAD_SLOT - MLA_V, d), F32)], axis=1)
    wo = jnp.concatenate([o3.reshape(MLA_W, d), w_out[N_HEADS * MLA_V:]], axis=0).astype(BF16)

    qc = np.arange(GRID_W)[:, None]
    kc = np.arange(GRID_W)[None, :]
    ws = np.clip(qc - NA_WIN_W // 2, 0, GRID_W - NA_WIN_W)
    in_win = (kc >= ws) & (kc < ws + NA_WIN_W)
    edge = GRID_W - NA_WIN_W
    rp = jnp.pad(rpb.astype(F32), ((0, 0), (0, 0), (edge, edge)))
    toep = jnp.stack([rp[:, :, GRID_W - 1 - q:2 * GRID_W - 1 - q] for q in range(GRID_W)], axis=2)
    t2 = jnp.where(in_win[None, None], toep * LOG2_E, NEG_INF)
    bias = jnp.concatenate([t2[:, :-1], t2[:, 1:]], axis=-1)
    return win, wq, wkv, wvt, wo, bias


def _channel_dft():
    m = (np.arange(FN_GW)[:, None] * np.arange(FN_GW)[None, :]) % FN_GW
    ang = 2.0 * np.pi * m / FN_GW
    eye = np.eye(FN_GROUPS)
    return np.concatenate([np.kron(eye, np.cos(ang)), -np.kron(eye, np.sin(ang))],
                          axis=1).astype(np.float32)


def kernel(x, c, ctx, c_ctx, w_mod, b_mod, norm1_g, norm2_g, w_in, mla_q_norm, mla_w_uq, mla_kv_norm, mla_w_ukv, na_rpb, cv_w_dw, cv_b_dw, cv_ln_g, cv_ln_b, w_out, w_ff1, w_ff2, final_g):
    depth = w_mod.shape[0]
    b, s, d = x.shape
    n_ctx = ctx.shape[1]

    cc = jnp.concatenate([c, c_ctx[None, :], jnp.zeros((8 - b - 1, d), F32)], axis=0)
    mods = _mod_call(cc, w_mod, b_mod)

    cs_lat = _rope_tables(s, True)
    cs_ctx = _rope_tables(n_ctx, False)
    dc = jnp.asarray(_channel_dft()).astype(BF16)
    row = lambda p: p.reshape(1, -1)
    fg = row(final_g)
    w1 = w_ff1.astype(BF16)
    w2 = w_ff2.astype(BF16)

    xc = ctx
    for i in range(depth):
        last = i == depth - 1
        win, wq, wkv, wvt, wo, bias = _layer_weights(w_in[i], mla_w_uq[i], mla_w_ukv[i], w_out[i], na_rpb[i])
        mx =[m.reshape(b, 1, d) for m in jnp.split(mods[i, :b], 6, axis=-1)]
        mc = [jnp.broadcast_to(m.reshape(1, 1, d), (b, 1, d))
              for m in jnp.split(mods[i, b:b + 1], 6, axis=-1)]
        proj = functools.partial(_inproj_call, g1=row(norm1_g[i]), win=win, qn=row(mla_q_norm[i]),
                                 wq=wq, kvn=row(mla_kv_norm[i]), wkv=wkv, wvt=wvt, dc=dc)
        conv = functools.partial(_conv_call, w_dw=cv_w_dw[i], b_dw=row(cv_b_dw[i]),
                                 ln_g=row(cv_ln_g[i]), ln_b=row(cv_ln_b[i]))
        mlp = functools.partial(_outmlp_call, wo=wo, g2=row(norm2_g[i]), w1=w1, w2=w2, fg=fg, layer=i)

        q, k, vt, nq, nk, nv, z, cy = proj(x, mx[0], mx[1], cs=cs_lat)
        cq, ck, cvt, cnq, cnk, cnv, cz, ccy = proj(xc, mc[0], mc[1], cs=cs_ctx)

        o_mla = _flash_call(q, k, vt, ck, cvt)
        o_na = _natten_call(nq, nk, nv, cnk, cnv, bias)
        o_fn = _fourier_mix(z)
        o_cv = conv(cy)
        x = mlp(x, o_mla, o_na, o_fn, o_cv, gt1=mx[2], sh2=mx[3], sc2=mx[4], gt2=mx[5], last=last)

        if not last:
            co_mla = _flash_call(cq, ck, cvt)
            co_na = _na_ctx_call(cnq, cnk, cnv)
            co_fn = _fourier_mix(cz)
            co_cv = conv(ccy)
            xc = mlp(xc, co_mla, co_na, co_fn, co_cv, gt1=mc[2], sh2=mc[3], sc2=mc[4], gt2=mc[5],
                     last=False)
    return x
```

```python
import functools

import jax
import jax.numpy as jnp
import numpy as np
from jax import lax
from jax.experimental import pallas as pl
from jax.experimental.pallas import tpu as pltpu

F32 = jnp.float32
BF16 = jnp.bfloat16

EPS = 1e-6
ROPE_BASE = 10000.0
NEG_INF = -1e30

D_MODEL = 1024
GRID_W = 64
N_HEADS = 4
GROUP_W = 256
HEAD_SLOT = 128
MLA_NOPE = 64
MLA_ROPE = 32
MLA_QK = MLA_NOPE + MLA_ROPE
MLA_V = 64
MLA_W = N_HEADS * HEAD_SLOT
VT_ROWS = 80
LOG2_E = 1.4426950408889634
NA_HD = 64
NA_WIN_H = 8
NA_WIN_W = 16
FN_GROUPS = 4
FN_GW = GROUP_W // FN_GROUPS
CV_K = 31
CV_HALO = 16
SUBLANES = 8
D_FF = 4 * D_MODEL
DFT_N2 = 128
V7X_VMEM_LIMIT = 56 * 1024 * 1024


def _cparams(n_grid):
    return pltpu.CompilerParams(dimension_semantics=("parallel",) * n_grid,
                                vmem_limit_bytes=V7X_VMEM_LIMIT)


def _const_spec(shape):
    nd = len(shape)
    return pl.BlockSpec(shape, lambda *_: (0,) * nd, pipeline_mode=pl.Buffered(1))


def _sigmoid(v):
    return 1.0 / (1.0 + jnp.exp(-v))


def _rms(v, g):
    return v * lax.rsqrt(jnp.mean(v * v, axis=-1, keepdims=True) + EPS) * g


def _dot(a, b):
    return jnp.dot(a, b, preferred_element_type=F32)


def _dot_nt(a, b):
    return lax.dot_general(a, b, (((1,), (1,)), ((), ())), preferred_element_type=F32)


def _mod_kernel(c_ref, w_ref, b_ref, o_ref):
    c = c_ref[...]
    s = (c * _sigmoid(c)).astype(BF16)
    o_ref[0] = _dot(s, w_ref[0].astype(BF16)) + b_ref[0]


def _mod_call(cc, w_mod, b_mod):
    depth, d, n = w_mod.shape
    tn = 1536
    return pl.pallas_call(
        _mod_kernel,
        grid=(depth, n // tn),
        in_specs=[pl.BlockSpec(cc.shape, lambda l, j: (0, 0)),
                  pl.BlockSpec((1, d, tn), lambda l, j: (l, 0, j)),
                  pl.BlockSpec((1, 1, tn), lambda l, j: (l, 0, j))],
        out_specs=pl.BlockSpec((1, cc.shape[0], tn), lambda l, j: (l, 0, j)),
        out_shape=jax.ShapeDtypeStruct((depth, cc.shape[0], n), F32),
        compiler_params=_cparams(2),
        name="mod",
    )(cc, w_mod, b_mod.reshape(depth, 1, n))


def _inproj_kernel(x_ref, sh_ref, sc_ref, g1_ref, win_ref, qn_ref, wq_ref, kvn_ref, wkv_ref,
                   wvt_ref, dc_ref, cs_ref,
                   q_ref, k_ref, vt_ref, nq_ref, nk_ref, nv_ref, z_ref, cy_ref):
    x = x_ref[0]
    h = _rms(x, g1_ref[...]) * (1.0 + sc_ref[0]) + sh_ref[0]
    p = _dot(h.astype(BF16), win_ref[...])

    cs = cs_ref[...]
    cos_t, sin_t = cs[:, :HEAD_SLOT], cs[:, HEAD_SLOT:]
    cos4 = jnp.concatenate([cos_t] * N_HEADS, axis=-1)
    sin4 = jnp.concatenate([sin_t] * N_HEADS, axis=-1)

    r = _rms(p[:, 0:256], qn_ref[...]).astype(BF16)
    qq = _dot(r, wq_ref[...])
    q = (qq[:, :MLA_W] * cos4 + qq[:, MLA_W:] * sin4) * (MLA_QK ** -0.5 * LOG2_E)
    q_ref[0] = q.astype(BF16)

    kvr = p[:, 256:512]
    kvn = _rms(kvr[:, :128], kvn_ref[...])
    comb = jnp.concatenate([kvn, kvr[:, 128:]], axis=-1).astype(BF16)
    kk = _dot(comb, wkv_ref[...])
    k = kk[:, :MLA_W] * cos4 + kk[:, MLA_W:] * sin4
    k_ref[0] = k.astype(BF16)
    vt = _dot_nt(wvt_ref[...], comb)
    sub = lax.broadcasted_iota(jnp.int32, (vt.shape[0], 1), 0)
    ones_row = jnp.where(sub % VT_ROWS == MLA_V, 1.0, 0.0).astype(F32)
    vt_ref[0] = (vt + ones_row).astype(BF16)

    nq_ref[0] = (p[:, 512:768] * (NA_HD ** -0.5 * LOG2_E)).astype(BF16)
    nk_ref[0] = p[:, 768:1024].astype(BF16)
    nv_ref[0] = p[:, 1024:1280].astype(BF16)

    z_ref[0] = _dot(p[:, 1280:1536].astype(BF16), dc_ref[...]).astype(BF16)

    cy_ref[0] = p[:, 1536:1792] * _sigmoid(p[:, 1792:2048])


def _inproj_call(x, sh, sc, g1, win, qn, wq, kvn, wkv, wvt, dc, cs):
    b, l, d = x.shape
    t = min(l, 1024)
    tok = lambda w: pl.BlockSpec((1, t, w), lambda bi, i: (bi, i, 0))
    vec = pl.BlockSpec((1, 1, d), lambda bi, i: (bi, 0, 0))
    outs = [(MLA_W, BF16), (MLA_W, BF16), None, (GROUP_W, BF16), (GROUP_W, BF16),
            (GROUP_W, BF16), (2 * GROUP_W, BF16), (GROUP_W, F32)]
    vt_rows = N_HEADS * VT_ROWS
    out_specs = [tok(o[0]) if o else pl.BlockSpec((1, vt_rows, t), lambda bi, i: (bi, 0, i))
                 for o in outs]
    out_shape = [jax.ShapeDtypeStruct((b, l, o[0]), o[1]) if o
                 else jax.ShapeDtypeStruct((b, vt_rows, l), BF16) for o in outs]
    return pl.pallas_call(
        _inproj_kernel,
        grid=(b, l // t),
        in_specs=[tok(d), vec, vec, _const_spec(g1.shape), _const_spec(win.shape),
                  _const_spec(qn.shape), _const_spec(wq.shape), _const_spec(kvn.shape),
                  _const_spec(wkv.shape), _const_spec(wvt.shape), _const_spec(dc.shape),
                  pl.BlockSpec((t, 2 * HEAD_SLOT), lambda bi, i: (i, 0))],
        out_specs=out_specs,
        out_shape=out_shape,
        compiler_params=_cparams(2),
        name="inproj",
    )(x, sh, sc, g1, win, qn, wq, kvn, wkv, wvt, dc, cs)


def _flash_kernel(*refs, n_chunks, tk, has_extra):
    if has_extra:
        q_ref, k_ref, vt_ref, k2_ref, vt2_ref, o_ref, sa_ref, sb_ref, sx_ref = refs
    else:
        q_ref, k_ref, vt_ref, o_ref, sa_ref, sb_ref = refs
    q = q_ref[0]
    tq = q.shape[0]

    def scores(kc, s_ref):
        st = _dot_nt(kc, q)
        s_ref[...] = st
        return jnp.max(st, axis=0, keepdims=True)

    def accumulate(s_ref, mx, vtc, carry):
        m, acc = carry
        m_new = jnp.maximum(m, mx)
        p = jnp.exp2(s_ref[...] - m_new).astype(BF16)
        return m_new, jnp.exp2(m - m_new) * acc + _dot(vtc, p)

    def k_chunk(c):
        return k_ref[0, pl.ds(pl.multiple_of(c * tk, tk), tk), :]

    def vt_chunk(c):
        return vt_ref[0, :, pl.ds(pl.multiple_of(c * tk, tk), tk)]

    def pair(j, carry):
        mx_a, m, acc = carry
        mx_b = scores(k_chunk(2 * j + 1), sb_ref)
        m, acc = accumulate(sa_ref, mx_a, vt_chunk(2 * j), (m, acc))
        mx_a = scores(k_chunk(2 * j + 2), sa_ref)
        m, acc = accumulate(sb_ref, mx_b, vt_chunk(2 * j + 1), (m, acc))
        return mx_a, m, acc

    n_pairs = (n_chunks - 1) // 2
    carry = (scores(k_chunk(0), sa_ref), jnp.full((1, tq), -jnp.inf, F32),
             jnp.zeros((VT_ROWS, tq), F32))
    mx, m, acc = lax.fori_loop(0, n_pairs, pair, carry)
    pending = (sa_ref, mx, vt_chunk(2 * n_pairs))
    tail = [(k_chunk(c), vt_chunk(c), sb_ref if c % 2 else sa_ref)
            for c in range(2 * n_pairs + 1, n_chunks)]
    if has_extra:
        tail.append((k2_ref[0], vt2_ref[0], sx_ref))
    for kc, vtc, s_ref in tail:
        mx_next = scores(kc, s_ref)
        m, acc = accumulate(*pending, (m, acc))
        pending = (s_ref, mx_next, vtc)
    m, acc = accumulate(*pending, (m, acc))

    o_t = acc / acc[MLA_V:MLA_V + 1, :]
    o_t = jnp.concatenate([o_t, jnp.zeros((HEAD_SLOT - VT_ROWS, tq), F32)], axis=0)
    o_ref[0] = o_t.T.astype(BF16)


def _flash_call(q, k, vt, k2=None, vt2=None):
    b, lq, _ = q.shape
    lk = k.shape[1]
    tq = min(lq, 2048)
    tk = min(lk, 1024)
    has_extra = k2 is not None
    qspec = pl.BlockSpec((1, tq, HEAD_SLOT), lambda bi, h, i: (bi, i, h))
    kspec = lambda n: pl.BlockSpec((1, n, HEAD_SLOT), lambda bi, h, i: (bi, 0, h))
    vspec = lambda n: pl.BlockSpec((1, VT_ROWS, n), lambda bi, h, i: (bi, h, 0))
    in_specs = [qspec, kspec(lk), vspec(lk)]
    args = [q, k, vt]
    scratch = [pltpu.VMEM((tk, tq), F32), pltpu.VMEM((tk, tq), F32)]
    if has_extra:
        in_specs += [kspec(k2.shape[1]), vspec(k2.shape[1])]
        args += [k2, vt2]
        scratch.append(pltpu.VMEM((k2.shape[1], tq), F32))
    return pl.pallas_call(
        functools.partial(_flash_kernel, n_chunks=lk // tk, tk=tk, has_extra=has_extra),
        grid=(b, N_HEADS, lq // tq),
        in_specs=in_specs,
        out_specs=qspec,
        out_shape=jax.ShapeDtypeStruct((b, lq, MLA_W), BF16),
        scratch_shapes=scratch,
        compiler_params=_cparams(3),
        name="mla_attn",
    )(*args)


def _head_stack(q):
    lane = lax.broadcasted_iota(jnp.int32, q.shape, 1)
    return jnp.concatenate(
        [jnp.where(lane // NA_HD == h, q, jnp.zeros_like(q)) for h in range(N_HEADS)], axis=0)


def _head_unstack(o, n):
    lane = lax.broadcasted_iota(jnp.int32, (n, GROUP_W), 1)
    out = jnp.zeros((n, GROUP_W), F32)
    for h in range(N_HEADS):
        out = out + jnp.where(lane // NA_HD == h, o[h * n:(h + 1) * n], 0.0)
    return out


def _natten_kernel(q_ref, k_ref, v_ref, kc_ref, vc_ref, bias_ref, o_ref, *, rows_per_step, rows):
    blk = pl.program_id(1)
    kc = kc_ref[0]
    vc = vc_ref[0]
    n_loc = NA_WIN_H * GRID_W

    def body(j, _):
        r = blk * rows_per_step + j
        rs = jnp.clip(r - NA_WIN_H // 2, 0, rows - NA_WIN_H)
        d0 = rs - r + (NA_WIN_H - 1)
        qs = _head_stack(q_ref[0, pl.ds(pl.multiple_of(j * GRID_W, GRID_W), GRID_W), :])
        kstart = pl.multiple_of(rs * GRID_W, GRID_W)
        s_loc = _dot_nt(qs, k_ref[0, pl.ds(kstart, n_loc), :])
        bias = jnp.concatenate(
            [jnp.concatenate([bias_ref[h, d0 + 2 * w] for w in range(NA_WIN_H // 2)], axis=-1)
             for h in range(N_HEADS)], axis=0)
        s_loc = s_loc + bias
        s_ctx = _dot_nt(qs, kc)
        m = jnp.maximum(jnp.max(s_loc, axis=-1, keepdims=True),
                        jnp.max(s_ctx, axis=-1, keepdims=True))
        p_loc = jnp.exp2(s_loc - m)
        p_ctx = jnp.exp2(s_ctx - m)
        denom = jnp.sum(p_loc, axis=-1, keepdims=True) + jnp.sum(p_ctx, axis=-1, keepdims=True)
        o = (_dot(p_loc.astype(BF16), v_ref[0, pl.ds(kstart, n_loc), :])
             + _dot(p_ctx.astype(BF16), vc))
        o = _head_unstack(o / denom, GRID_W)
        o_ref[0, pl.ds(pl.multiple_of(j * GRID_W, GRID_W), GRID_W), :] = o.astype(BF16)
        return 0

    lax.fori_loop(0, rows_per_step, body, 0, unroll=True)


def _natten_call(q, k, v, kc, vc, bias):
    b, l, w = q.shape
    rows = l // GRID_W
    rows_per_step = 16
    t = rows_per_step * GRID_W
    full = lambda n: pl.BlockSpec((1, n, w), lambda bi, i: (bi, 0, 0))
    return pl.pallas_call(
        functools.partial(_natten_kernel, rows_per_step=rows_per_step, rows=rows),
        grid=(b, rows // rows_per_step),
        in_specs=[pl.BlockSpec((1, t, w), lambda bi, i: (bi, i, 0)),
                  full(l), full(l), full(kc.shape[1]), full(kc.shape[1]),
                  _const_spec(bias.shape)],
        out_specs=pl.BlockSpec((1, t, w), lambda bi, i: (bi, i, 0)),
        out_shape=jax.ShapeDtypeStruct((b, l, w), BF16),
        compiler_params=_cparams(2),
        name="natten",
    )(q, k, v, kc, vc, bias)


def _na_ctx_kernel(q_ref, k_ref, v_ref, o_ref):
    n = q_ref.shape[1]
    s = _dot_nt(_head_stack(q_ref[0]), k_ref[0])
    m = jnp.max(s, axis=-1, keepdims=True)
    p = jnp.exp2(s - m)
    denom = jnp.sum(p, axis=-1, keepdims=True)
    o = _dot(p.astype(BF16), v_ref[0]) / denom
    o_ref[0] = _head_unstack(o, n).astype(BF16)


def _na_ctx_call(q, k, v):
    b, n, w = q.shape
    spec = pl.BlockSpec((1, n, w), lambda bi: (bi, 0, 0))
    return pl.pallas_call(
        _na_ctx_kernel, grid=(b,), in_specs=[spec, spec, spec], out_specs=spec,
        out_shape=jax.ShapeDtypeStruct((b, n, w), BF16),
        compiler_params=_cparams(1), name="na_ctx",
    )(q, k, v)


def _fnet1_kernel(z_ref, w_ref, y_ref, *, n_inner):
    zb = z_ref[0]
    p = _dot(w_ref[...], zb.reshape(zb.shape[0], zb.shape[1] * zb.shape[2]))
    n1 = p.shape[0] // 2
    re, im = [], []
    for j in range(n_inner):
        zr_c = p[:n1, j * 512:j * 512 + 256]
        zi_c = p[:n1, j * 512 + 256:(j + 1) * 512]
        zr_s = p[n1:, j * 512:j * 512 + 256]
        zi_s = p[n1:, j * 512 + 256:(j + 1) * 512]
        re.append(zr_c + zi_s)
        im.append(zi_c - zr_s)
    y_ref[0, 0] = jnp.concatenate(re, axis=-1).astype(BF16).reshape(n1, n_inner, GROUP_W)
    y_ref[0, 1] = jnp.concatenate(im, axis=-1).astype(BF16).reshape(n1, n_inner, GROUP_W)


def _fnet1_call(z2, w1s, n1):
    b = z2.shape[0]
    n_inner = 16
    return pl.pallas_call(
        functools.partial(_fnet1_kernel, n_inner=n_inner),
        grid=(b, DFT_N2 // n_inner),
        in_specs=[pl.BlockSpec((1, n1, n_inner, 512), lambda bi, i: (bi, 0, i, 0)),
                  _const_spec(w1s.shape)],
        out_specs=pl.BlockSpec((1, 2, n1, n_inner, GROUP_W), lambda bi, i: (bi, 0, 0, i, 0)),
        out_shape=jax.ShapeDtypeStruct((b, 2, n1, DFT_N2, GROUP_W), BF16),
        compiler_params=_cparams(2),
        name="fnet_stage1",
    )(z2, w1s)


def _fnet2_kernel(y_ref, f_ref, o_ref, *, n_inner, norm):
    outs = []
    for j in range(n_inner):
        f = f_ref[j]
        o = _dot(f[:, :DFT_N2], y_ref[0, 0, j]) + _dot(f[:, DFT_N2:], y_ref[0, 1, j])
        outs.append((o * norm).astype(BF16))
    o_ref[0] = jnp.concatenate(outs, axis=-1).reshape(DFT_N2, n_inner, GROUP_W)


def _fnet2_call(y5, ftab, n1, norm):
    b = y5.shape[0]
    n_inner = 16
    return pl.pallas_call(
        functools.partial(_fnet2_kernel, n_inner=n_inner, norm=norm),
        grid=(b, n1 // n_inner),
        in_specs=[pl.BlockSpec((1, 2, n_inner, DFT_N2, GROUP_W), lambda bi, i: (bi, 0, i, 0, 0)),
                  pl.BlockSpec((n_inner, DFT_N2, 2 * DFT_N2), lambda bi, i: (i, 0, 0))],
        out_specs=pl.BlockSpec((1, DFT_N2, n_inner, GROUP_W), lambda bi, i: (bi, 0, i, 0)),
        out_shape=jax.ShapeDtypeStruct((b, DFT_N2, n1, GROUP_W), BF16),
        compiler_params=_cparams(2),
        name="fnet_stage2",
    )(y5, ftab)


def _dft_small_kernel(z_ref, f_ref, o_ref, *, norm):
    z = z_ref[0]
    f = f_ref[...]
    n = z.shape[0]
    o = _dot(f[:, :n], z[:, :GROUP_W]) + _dot(f[:, n:], z[:, GROUP_W:])
    o_ref[0] = (o * norm).astype(BF16)


def _dft_small_call(z, ftab, norm):
    b, n, _ = z.shape
    return pl.pallas_call(
        functools.partial(_dft_small_kernel, norm=norm),
        grid=(b,),
        in_specs=[pl.BlockSpec((1, n, 2 * GROUP_W), lambda bi: (bi, 0, 0)), _const_spec(ftab.shape)],
        out_specs=pl.BlockSpec((1, n, GROUP_W), lambda bi: (bi, 0, 0)),
        out_shape=jax.ShapeDtypeStruct((b, n, GROUP_W), BF16),
        compiler_params=_cparams(1), name="fnet_ctx",
    )(z, ftab)


def _fourier_tables(l):
    if l <= 256:
        m = (np.arange(l)[:, None] * np.arange(l)[None, :]) % l
        ang = 2.0 * np.pi * m / l
        return None, np.concatenate([np.cos(ang), np.sin(ang)], axis=1).astype(np.float32), 0
    n1 = l // DFT_N2
    m1 = (np.arange(n1)[:, None] * np.arange(n1)[None, :]) % n1
    a1 = 2.0 * np.pi * m1 / n1
    w1s = np.concatenate([np.cos(a1), np.sin(a1)], axis=0).astype(np.float32)
    kk = np.arange(n1)[:, None, None] + n1 * np.arange(DFT_N2)[None, :, None]
    m2 = (kk * np.arange(DFT_N2)[None, None, :]) % l
    a2 = 2.0 * np.pi * m2 / l
    ftab = np.concatenate([np.cos(a2), np.sin(a2)], axis=2).astype(np.float32)
    return w1s, ftab, n1


def _fourier_mix(z):
    b, l, _ = z.shape
    norm = float((l * FN_GW) ** -0.5)
    w1s, ftab, n1 = _fourier_tables(l)
    if w1s is None:
        return _dft_small_call(z, jnp.asarray(ftab).astype(BF16), norm)
    y = _fnet1_call(z.reshape(b, n1, DFT_N2, 2 * GROUP_W), jnp.asarray(w1s).astype(BF16), n1)
    o = _fnet2_call(y, jnp.asarray(ftab).astype(BF16), n1, norm)
    return o.reshape(b, l, GROUP_W)


def _conv_kernel(prev_ref, cur_ref, next_ref, w_ref, b_ref, g_ref, beta_ref, o_ref, buf_ref):
    i = pl.program_id(1)
    n = pl.num_programs(1)
    t = cur_ref.shape[1]
    buf_ref[0:CV_HALO, :] = jnp.where(i > 0, prev_ref[0], 0.0)
    buf_ref[CV_HALO:CV_HALO + t, :] = cur_ref[0]
    buf_ref[CV_HALO + t:, :] = jnp.where(i < n - 1, next_ref[0], 0.0)
    w = w_ref[...]
    first = CV_HALO - CV_K // 2
    acc = None
    for res in range(SUBLANES):
        z = None
        for base in range(0, first + CV_K, SUBLANES):
            j = base + res - first
            if 0 <= j < CV_K:
                term = w[j:j + 1, :] * buf_ref[base:base + t + SUBLANES, :]
                z = term if z is None else z + term
        z = z[res:res + t, :]
        acc = z if acc is None else acc + z
    y = acc + b_ref[...]
    mu = jnp.mean(y, axis=-1, keepdims=True)
    var = jnp.mean(jnp.square(y - mu), axis=-1, keepdims=True)
    y = (y - mu) * lax.rsqrt(var + EPS) * g_ref[...] + beta_ref[...]
    o_ref[0] = (y * _sigmoid(y)).astype(BF16)


def _conv_call(cy, w_dw, b_dw, ln_g, ln_b):
    b, l, w = cy.shape
    t = min(l, 512)
    hb = t // CV_HALO
    n_halo = l // CV_HALO
    return pl.pallas_call(
        _conv_kernel,
        grid=(b, l // t),
        in_specs=[pl.BlockSpec((1, CV_HALO, w), lambda bi, i: (bi, jnp.maximum(i * hb - 1, 0), 0)),
                  pl.BlockSpec((1, t, w), lambda bi, i: (bi, i, 0)),
                  pl.BlockSpec((1, CV_HALO, w),
                               lambda bi, i: (bi, jnp.minimum((i + 1) * hb, n_halo - 1), 0)),
                  _const_spec(w_dw.shape), _const_spec(b_dw.shape), _const_spec(ln_g.shape),
                  _const_spec(ln_b.shape)],
        out_specs=pl.BlockSpec((1, t, w), lambda bi, i: (bi, i, 0)),
        out_shape=jax.ShapeDtypeStruct((b, l, w), BF16),
        scratch_shapes=[pltpu.VMEM((t + 2 * CV_HALO, w), F32)],
        compiler_params=_cparams(2),
        name="conformer_conv",
    )(cy, cy, cy, w_dw, b_dw, ln_g, ln_b)


def _outmlp_kernel(x_ref, oa_ref, oc_ref, ob_ref, od_ref, wo_ref, gt1_ref, sh2_ref, sc2_ref,
                   gt2_ref, g2_ref, w1_ref, w2_ref, fg_ref, o_ref, *, last, ff_chunk):
    y = (_dot(oa_ref[0], wo_ref[0:MLA_W, :])
         + _dot(oc_ref[0], wo_ref[MLA_W:MLA_W + GROUP_W, :])
         + _dot(ob_ref[0], wo_ref[MLA_W + GROUP_W:MLA_W + 2 * GROUP_W, :])
         + _dot(od_ref[0], wo_ref[MLA_W + 2 * GROUP_W:, :]))
    x1 = x_ref[0] + gt1_ref[0] * y
    h = (_rms(x1, g2_ref[...]) * (1.0 + sc2_ref[0]) + sh2_ref[0]).astype(BF16)
    ff = jnp.zeros(x1.shape, F32)
    for c in range(D_FF // ff_chunk):
        u = jnp.maximum(_dot(h, w1_ref[0, :, c * ff_chunk:(c + 1) * ff_chunk]), 0.0)
        ff = ff + _dot((u * u).astype(BF16), w2_ref[0, c * ff_chunk:(c + 1) * ff_chunk, :])
    x2 = x1 + gt2_ref[0] * ff
    if last:
        x2 = _rms(x2, fg_ref[...])
    o_ref[0] = x2


def _outmlp_call(x, o_mla, o_na, o_fn, o_cv, wo, gt1, sh2, sc2, gt2, g2, w1, w2, fg, last, layer):
    b, l, d = x.shape
    t = min(l, 512)
    tok = lambda w: pl.BlockSpec((1, t, w), lambda bi, i: (bi, i, 0))
    vec = pl.BlockSpec((1, 1, d), lambda bi, i: (bi, 0, 0))
    slab = lambda w: pl.BlockSpec((1,) + w.shape[1:], lambda bi, i: (layer, 0, 0),
                                  pipeline_mode=pl.Buffered(1))
    return pl.pallas_call(
        functools.partial(_outmlp_kernel, last=last, ff_chunk=1024),
        grid=(b, l // t),
        in_specs=[tok(d), tok(MLA_W), tok(GROUP_W), tok(GROUP_W), tok(GROUP_W),
                  _const_spec(wo.shape), vec, vec, vec, vec, _const_spec(g2.shape),
                  slab(w1), slab(w2), _const_spec(fg.shape)],
        out_specs=tok(d),
        out_shape=jax.ShapeDtypeStruct((b, l, d), F32),
        compiler_params=_cparams(2),
        name="outproj_mlp",
    )(x, o_mla, o_na, o_fn, o_cv, wo, gt1, sh2, sc2, gt2, g2, w1, w2, fg)


def _rope_tables(l, with_rope):
    per_axis = MLA_ROPE // 2
    n_freq = per_axis // 2

    def slot(nope, row_part, col_part, sign):
        n = row_part.shape[0]
        return jnp.concatenate([jnp.full((n, MLA_NOPE), nope, F32), sign * row_part, sign * col_part,
                                row_part, col_part, jnp.zeros((n, HEAD_SLOT - MLA_QK), F32)], axis=-1)

    if not with_rope:
        one, zero = jnp.ones((l, n_freq), F32), jnp.zeros((l, n_freq), F32)
        return jnp.concatenate([slot(1.0, one, one, 1.0), slot(0.0, zero, zero, -1.0)], axis=-1)
    inv = ROPE_BASE ** (-jnp.arange(0, per_axis, 2, dtype=F32) / per_axis)
    r_ang = jnp.arange(l // GRID_W).astype(F32)[:, None] * inv
    c_ang = jnp.arange(GRID_W).astype(F32)[:, None] * inv
    zr, zc = jnp.zeros_like(r_ang), jnp.zeros_like(c_ang)
    by_row = jnp.concatenate([slot(0.0, jnp.cos(r_ang), zr, 1.0), slot(0.0, jnp.sin(r_ang), zr, -1.0)], axis=-1)
    by_col = jnp.concatenate([slot(1.0, zc, jnp.cos(c_ang), 1.0), slot(0.0, zc, jnp.sin(c_ang), -1.0)], axis=-1)
    return (by_row[:, None, :] + by_col[None, :, :]).reshape(l, 2 * HEAD_SLOT)


def _swap_halves(w):
    half = w.shape[-1] // 2
    return jnp.concatenate([w[..., half:], w[..., :half]], axis=-1)


def _layer_weights(w_in, w_uq, w_ukv, w_out, rpb):
    d = w_in.shape[0]
    k_r_end = 256 + 128 + MLA_ROPE
    win = jnp.concatenate([w_in[:, :k_r_end], jnp.zeros((d, HEAD_SLOT - MLA_ROPE), F32),
                           w_in[:, k_r_end:]], axis=1).astype(BF16)

    rq = w_uq.shape[0]
    w3 = w_uq.reshape(rq, N_HEADS, MLA_QK)
    zpad = jnp.zeros((rq, N_HEADS, HEAD_SLOT - MLA_QK), F32)
    plain = jnp.concatenate([w3, zpad], axis=-1)
    swapped = jnp.concatenate([jnp.zeros((rq, N_HEADS, MLA_NOPE), F32),
                               _swap_halves(w3[..., MLA_NOPE:]), zpad], axis=-1)
    wq = jnp.concatenate([plain.reshape(rq, MLA_W), swapped.reshape(rq, MLA_W)], axis=1).astype(BF16)

    rkv = w_ukv.shape[0]
    u3 = w_ukv.reshape(rkv, N_HEADS, MLA_NOPE + MLA_V)
    eye = jnp.eye(MLA_ROPE, dtype=F32)
    slot_pad_r = jnp.zeros((MLA_ROPE, N_HEADS, HEAD_SLOT - MLA_QK), F32)

    def slots(top, rope_block):
        top = jnp.concatenate([top, jnp.zeros((rkv, N_HEADS, HEAD_SLOT - top.shape[-1]), F32)], axis=-1)
        mid = jnp.concatenate([jnp.zeros((MLA_ROPE, N_HEADS, MLA_NOPE), F32),
                               jnp.broadcast_to(rope_block[:, None, :], (MLA_ROPE, N_HEADS, MLA_ROPE)),
                               slot_pad_r], axis=-1)
        bot = jnp.zeros((2 * HEAD_SLOT - rkv - MLA_ROPE, N_HEADS, HEAD_SLOT), F32)
        return jnp.concatenate([top, mid, bot], axis=0).reshape(2 * HEAD_SLOT, MLA_W)

    k_plain = slots(u3[..., :MLA_NOPE], eye)
    k_swap = slots(jnp.zeros((rkv, N_HEADS, MLA_NOPE), F32), _swap_halves(eye))
    wkv = jnp.concatenate([k_plain, k_swap], axis=1).astype(BF16)

    vt3 = jnp.transpose(u3[..., MLA_NOPE:], (1, 2, 0))
    vt3 = jnp.pad(vt3, ((0, 0), (0, VT_ROWS - MLA_V), (0, 2 * HEAD_SLOT - rkv)))
    wvt = vt3.reshape(N_HEADS * VT_ROWS, 2 * HEAD_SLOT).astype(BF16)

    o3 = w_out[:N_HEADS * MLA_V].reshape(N_HEADS, MLA_V, d)
    o3 = jnp.concatenate([o3, jnp.zeros((N_HEADS, HEAD_SLOT - MLA_V, d), F32)], axis=1)
    wo = jnp.concatenate([o3.reshape(MLA_W, d), w_out[N_HEADS * MLA_V:]], axis=0).astype(BF16)

    qc = np.arange(GRID_W)[:, None]
    kc = np.arange(GRID_W)[None, :]
    ws = np.clip(qc - NA_WIN_W // 2, 0, GRID_W - NA_WIN_W)
    in_win = (kc >= ws) & (kc < ws + NA_WIN_W)
    edge = GRID_W - NA_WIN_W
    n = 2 * GRID_W - 1
    rp = jnp.pad(rpb.astype(F32), ((0, 0), (0, 0), (edge, edge)))
    tiled = jnp.tile(rp, (1, 1, GRID_W + 1))[:, :, :GRID_W * (n + 1)]
    toep = tiled.reshape(rp.shape[0], rp.shape[1], GRID_W, n + 1)[:, :, ::-1, :GRID_W]
    t2 = jnp.where(in_win[None, None], toep * LOG2_E, NEG_INF)
    bias = jnp.concatenate([t2[:, :-1], t2[:, 1:]], axis=-1)
    return win, wq, wkv, wvt, wo, bias


def _channel_dft():
    m = (np.arange(FN_GW)[:, None] * np.arange(FN_GW)[None, :]) % FN_GW
    ang = 2.0 * np.pi * m / FN_GW
    eye = np.eye(FN_GROUPS)
    return np.concatenate([np.kron(eye, np.cos(ang)), -np.kron(eye, np.sin(ang))],
                          axis=1).astype(np.float32)


def kernel(x, c, ctx, c_ctx, w_mod, b_mod, norm1_g, norm2_g, w_in, mla_q_norm, mla_w_uq, mla_kv_norm, mla_w_ukv, na_rpb, cv_w_dw, cv_b_dw, cv_ln_g, cv_ln_b, w_out, w_ff1, w_ff2, final_g):
    depth = w_mod.shape[0]
    b, s, d = x.shape
    n_ctx = ctx.shape[1]

    cc = jnp.concatenate([c, c_ctx[None, :], jnp.zeros((8 - b - 1, d), F32)], axis=0)
    mods = _mod_call(cc, w_mod, b_mod)

    cs_lat = _rope_tables(s, True)
    cs_ctx = _rope_tables(n_ctx, False)
    dc = jnp.asarray(_channel_dft()).astype(BF16)
    row = lambda p: p.reshape(1, -1)
    fg = row(final_g)
    w1 = w_ff1.astype(BF16)
    w2 = w_ff2.astype(BF16)

    xc = ctx
    for i in range(depth):
        last = i == depth - 1
        win, wq, wkv, wvt, wo, bias = _layer_weights(w_in[i], mla_w_uq[i], mla_w_ukv[i], w_out[i], na_rpb[i])
        mx =[m.reshape(b, 1, d) for m in jnp.split(mods[i, :b], 6, axis=-1)]
        mc = [jnp.broadcast_to(m.reshape(1, 1, d), (b, 1, d))
              for m in jnp.split(mods[i, b:b + 1], 6, axis=-1)]
        proj = functools.partial(_inproj_call, g1=row(norm1_g[i]), win=win, qn=row(mla_q_norm[i]),
                                 wq=wq, kvn=row(mla_kv_norm[i]), wkv=wkv, wvt=wvt, dc=dc)
        conv = functools.partial(_conv_call, w_dw=cv_w_dw[i], b_dw=row(cv_b_dw[i]),
                                 ln_g=row(cv_ln_g[i]), ln_b=row(cv_ln_b[i]))
        mlp = functools.partial(_outmlp_call, wo=wo, g2=row(norm2_g[i]), w1=w1, w2=w2, fg=fg, layer=i)

        q, k, vt, nq, nk, nv, z, cy = proj(x, mx[0], mx[1], cs=cs_lat)
        cq, ck, cvt, cnq, cnk, cnv, cz, ccy = proj(xc, mc[0], mc[1], cs=cs_ctx)

        o_mla = _flash_call(q, k, vt, ck, cvt)
        o_na = _natten_call(nq, nk, nv, cnk, cnv, bias)
        o_fn = _fourier_mix(z)
        o_cv = conv(cy)
        x = mlp(x, o_mla, o_na, o_fn, o_cv, gt1=mx[2], sh2=mx[3], sc2=mx[4], gt2=mx[5], last=last)

        if not last:
            co_mla = _flash_call(cq, ck, cvt)
            co_na = _na_ctx_call(cnq, cnk, cnv)
            co_fn = _fourier_mix(cz)
            co_cv = conv(ccy)
            xc = mlp(xc, co_mla, co_na, co_fn, co_cv, gt1=mc[2], sh2=mc[3], sc2=mc[4], gt2=mc[5],
                     last=False)
    return x
```

```python
import functools

import jax
import jax.numpy as jnp
import numpy as np
from jax import lax
from jax.experimental import pallas as pl
from jax.experimental.pallas import tpu as pltpu

F32 = jnp.float32
BF16 = jnp.bfloat16

EPS = 1e-6
ROPE_BASE = 10000.0
NEG_INF = -1e30

D_MODEL = 1024
GRID_W = 64
N_HEADS = 4
GROUP_W = 256
HEAD_SLOT = 128
MLA_NOPE = 64
MLA_ROPE = 32
MLA_QK = MLA_NOPE + MLA_ROPE
MLA_V = 64
MLA_W = N_HEADS * HEAD_SLOT
VT_ROWS = MLA_V
LOG2_E = 1.4426950408889634
NA_HD = 64
NA_WIN_H = 8
NA_WIN_W = 16
FN_GROUPS = 4
FN_GW = GROUP_W // FN_GROUPS
CV_K = 31
CV_HALO = 16
SUBLANES = 8
D_FF = 4 * D_MODEL
DFT_N2 = 128
V7X_VMEM_LIMIT = 56 * 1024 * 1024


def _cparams(n_grid):
    return pltpu.CompilerParams(dimension_semantics=("parallel",) * n_grid,
                                vmem_limit_bytes=V7X_VMEM_LIMIT)


def _const_spec(shape):
    nd = len(shape)
    return pl.BlockSpec(shape, lambda *_: (0,) * nd, pipeline_mode=pl.Buffered(1))


def _sigmoid(v):
    return 1.0 / (1.0 + jnp.exp(-v))


def _rms(v, g):
    return v * lax.rsqrt(jnp.mean(v * v, axis=-1, keepdims=True) + EPS) * g


def _dot(a, b):
    return jnp.dot(a, b, preferred_element_type=F32)


def _dot_nt(a, b):
    return lax.dot_general(a, b, (((1,), (1,)), ((), ())), preferred_element_type=F32)


def _mod_kernel(c_ref, w_ref, b_ref, o_ref):
    c = c_ref[...]
    s = (c * _sigmoid(c)).astype(BF16)
    o_ref[0] = _dot(s, w_ref[0].astype(BF16)) + b_ref[0]


def _mod_call(cc, w_mod, b_mod):
    depth, d, n = w_mod.shape
    tn = 1536
    return pl.pallas_call(
        _mod_kernel,
        grid=(depth, n // tn),
        in_specs=[pl.BlockSpec(cc.shape, lambda l, j: (0, 0)),
                  pl.BlockSpec((1, d, tn), lambda l, j: (l, 0, j)),
                  pl.BlockSpec((1, 1, tn), lambda l, j: (l, 0, j))],
        out_specs=pl.BlockSpec((1, cc.shape[0], tn), lambda l, j: (l, 0, j)),
        out_shape=jax.ShapeDtypeStruct((depth, cc.shape[0], n), F32),
        compiler_params=_cparams(2),
        name="mod",
    )(cc, w_mod, b_mod.reshape(depth, 1, n))


def _inproj_kernel(x_ref, sh_ref, sc_ref, g1_ref, win_ref, qn_ref, wq_ref, kvn_ref, wkv_ref,
                   wvt_ref, dc_ref, cs_ref,
                   q_ref, k_ref, vt_ref, nq_ref, nk_ref, nv_ref, z_ref, cy_ref):
    x = x_ref[0]
    h = _rms(x, g1_ref[...]) * (1.0 + sc_ref[0]) + sh_ref[0]
    p = _dot(h.astype(BF16), win_ref[...])

    cs = cs_ref[...]
    cos_t, sin_t = cs[:, :HEAD_SLOT], cs[:, HEAD_SLOT:]
    cos4 = jnp.concatenate([cos_t] * N_HEADS, axis=-1)
    sin4 = jnp.concatenate([sin_t] * N_HEADS, axis=-1)

    r = _rms(p[:, 0:256], qn_ref[...]).astype(BF16)
    qq = _dot(r, wq_ref[...])
    q = (qq[:, :MLA_W] * cos4 + qq[:, MLA_W:] * sin4) * (MLA_QK ** -0.5 * LOG2_E)
    q_ref[0] = q.astype(BF16)

    kvr = p[:, 256:512]
    kvn = _rms(kvr[:, :128], kvn_ref[...])
    comb = jnp.concatenate([kvn, kvr[:, 128:]], axis=-1).astype(BF16)
    kk = _dot(comb, wkv_ref[...])
    k = kk[:, :MLA_W] * cos4 + kk[:, MLA_W:] * sin4
    k_ref[0] = k.astype(BF16)
    vt_ref[0] = _dot_nt(wvt_ref[...], comb).astype(BF16)

    nq_ref[0] = (p[:, 512:768] * (NA_HD ** -0.5 * LOG2_E)).astype(BF16)
    nk_ref[0] = p[:, 768:1024].astype(BF16)
    nv_ref[0] = p[:, 1024:1280].astype(BF16)

    z_ref[0] = _dot(p[:, 1280:1536].astype(BF16), dc_ref[...]).astype(BF16)

    cy_ref[0] = p[:, 1536:1792] * _sigmoid(p[:, 1792:2048])


def _inproj_call(x, sh, sc, g1, win, qn, wq, kvn, wkv, wvt, dc, cs):
    b, l, d = x.shape
    t = min(l, 1024)
    tok = lambda w: pl.BlockSpec((1, t, w), lambda bi, i: (bi, i, 0))
    vec = pl.BlockSpec((1, 1, d), lambda bi, i: (bi, 0, 0))
    outs = [(MLA_W, BF16), (MLA_W, BF16), None, (GROUP_W, BF16), (GROUP_W, BF16),
            (GROUP_W, BF16), (2 * GROUP_W, BF16), (GROUP_W, F32)]
    vt_rows = N_HEADS * VT_ROWS
    out_specs = [tok(o[0]) if o else pl.BlockSpec((1, vt_rows, t), lambda bi, i: (bi, 0, i))
                 for o in outs]
    out_shape = [jax.ShapeDtypeStruct((b, l, o[0]), o[1]) if o
                 else jax.ShapeDtypeStruct((b, vt_rows, l), BF16) for o in outs]
    return pl.pallas_call(
        _inproj_kernel,
        grid=(b, l // t),
        in_specs=[tok(d), vec, vec, _const_spec(g1.shape), _const_spec(win.shape),
                  _const_spec(qn.shape), _const_spec(wq.shape), _const_spec(kvn.shape),
                  _const_spec(wkv.shape), _const_spec(wvt.shape), _const_spec(dc.shape),
                  pl.BlockSpec((t, 2 * HEAD_SLOT), lambda bi, i: (i, 0))],
        out_specs=out_specs,
        out_shape=out_shape,
        compiler_params=_cparams(2),
        name="inproj",
    )(x, sh, sc, g1, win, qn, wq, kvn, wkv, wvt, dc, cs)


def _flash_kernel(*refs, n_chunks, tk, has_extra):
    if has_extra:
        q_ref, k_ref, vt_ref, k2_ref, vt2_ref, o_ref, sa_ref, sb_ref, sx_ref = refs
    else:
        q_ref, k_ref, vt_ref, o_ref, sa_ref, sb_ref = refs
    q = q_ref[0]
    tq = q.shape[0]

    def scores(kc, s_ref):
        st = _dot_nt(kc, q)
        s_ref[...] = st
        return jnp.max(st, axis=0, keepdims=True)

    def accumulate(s_ref, mx, vtc, carry):
        m, (acc, l) = carry
        m_new = jnp.maximum(m, mx)
        p = jnp.exp2(s_ref[...] - m_new)
        alpha = jnp.exp2(m - m_new)
        l = alpha * l + jnp.sum(p, axis=0, keepdims=True)
        return m_new, (alpha * acc + _dot(vtc, p.astype(BF16)), l)

    def k_chunk(c):
        return k_ref[0, pl.ds(pl.multiple_of(c * tk, tk), tk), :]

    def vt_chunk(c):
        return vt_ref[0, :, pl.ds(pl.multiple_of(c * tk, tk), tk)]

    def pair(j, carry):
        mx_a, m, acc = carry
        mx_b = scores(k_chunk(2 * j + 1), sb_ref)
        m, acc = accumulate(sa_ref, mx_a, vt_chunk(2 * j), (m, acc))
        mx_a = scores(k_chunk(2 * j + 2), sa_ref)
        m, acc = accumulate(sb_ref, mx_b, vt_chunk(2 * j + 1), (m, acc))
        return mx_a, m, acc

    n_pairs = (n_chunks - 1) // 2
    carry = (scores(k_chunk(0), sa_ref), jnp.full((1, tq), -jnp.inf, F32),
             (jnp.zeros((VT_ROWS, tq), F32), jnp.zeros((1, tq), F32)))
    mx, m, acc = lax.fori_loop(0, n_pairs, pair, carry)
    pending = (sa_ref, mx, vt_chunk(2 * n_pairs))
    tail = [(k_chunk(c), vt_chunk(c), sb_ref if c % 2 else sa_ref)
            for c in range(2 * n_pairs + 1, n_chunks)]
    if has_extra:
        tail.append((k2_ref[0], vt2_ref[0], sx_ref))
    for kc, vtc, s_ref in tail:
        mx_next = scores(kc, s_ref)
        m, acc = accumulate(*pending, (m, acc))
        pending = (s_ref, mx_next, vtc)
    m, acc = accumulate(*pending, (m, acc))

    acc, l = acc
    o_t = acc / l
    o_t = jnp.concatenate([o_t, jnp.zeros((HEAD_SLOT - VT_ROWS, tq), F32)], axis=0)
    o_ref[0] = o_t.T.astype(BF16)


def _flash_call(q, k, vt, k2=None, vt2=None):
    b, lq, _ = q.shape
    lk = k.shape[1]
    tq = min(lq, 2048)
    tk = min(lk, 1024)
    has_extra = k2 is not None
    qspec = pl.BlockSpec((1, tq, HEAD_SLOT), lambda bi, h, i: (bi, i, h))
    kspec = lambda n: pl.BlockSpec((1, n, HEAD_SLOT), lambda bi, h, i: (bi, 0, h))
    vspec = lambda n: pl.BlockSpec((1, VT_ROWS, n), lambda bi, h, i: (bi, h, 0))
    in_specs = [qspec, kspec(lk), vspec(lk)]
    args = [q, k, vt]
    scratch = [pltpu.VMEM((tk, tq), F32), pltpu.VMEM((tk, tq), F32)]
    if has_extra:
        in_specs += [kspec(k2.shape[1]), vspec(k2.shape[1])]
        args += [k2, vt2]
        scratch.append(pltpu.VMEM((k2.shape[1], tq), F32))
    return pl.pallas_call(
        functools.partial(_flash_kernel, n_chunks=lk // tk, tk=tk, has_extra=has_extra),
        grid=(b, N_HEADS, lq // tq),
        in_specs=in_specs,
        out_specs=qspec,
        out_shape=jax.ShapeDtypeStruct((b, lq, MLA_W), BF16),
        scratch_shapes=scratch,
        compiler_params=_cparams(3),
        name="mla_attn",
    )(*args)


def _head_stack(q):
    lane = lax.broadcasted_iota(jnp.int32, q.shape, 1)
    return jnp.concatenate(
        [jnp.where(lane // NA_HD == h, q, jnp.zeros_like(q)) for h in range(N_HEADS)], axis=0)


def _head_unstack(o, n):
    lane = lax.broadcasted_iota(jnp.int32, (n, GROUP_W), 1)
    out = jnp.zeros((n, GROUP_W), F32)
    for h in range(N_HEADS):
        out = out + jnp.where(lane // NA_HD == h, o[h * n:(h + 1) * n], 0.0)
    return out


def _natten_kernel(q_ref, k_ref, v_ref, kc_ref, vc_ref, bias_ref, o_ref, *, rows_per_step, rows):
    blk = pl.program_id(1)
    kc = kc_ref[0]
    vc = vc_ref[0]
    n_loc = NA_WIN_H * GRID_W

    def body(j, _):
        r = blk * rows_per_step + j
        rs = jnp.clip(r - NA_WIN_H // 2, 0, rows - NA_WIN_H)
        d0 = rs - r + (NA_WIN_H - 1)
        qs = _head_stack(q_ref[0, pl.ds(pl.multiple_of(j * GRID_W, GRID_W), GRID_W), :])
        kstart = pl.multiple_of(rs * GRID_W, GRID_W)
        s_loc = _dot_nt(qs, k_ref[0, pl.ds(kstart, n_loc), :])
        bias = jnp.concatenate(
            [jnp.concatenate([bias_ref[h, d0 + 2 * w] for w in range(NA_WIN_H // 2)], axis=-1)
             for h in range(N_HEADS)], axis=0)
        s_loc = s_loc + bias
        s_ctx = _dot_nt(qs, kc)
        m = jnp.maximum(jnp.max(s_loc, axis=-1, keepdims=True),
                        jnp.max(s_ctx, axis=-1, keepdims=True))
        p_loc = jnp.exp2(s_loc - m)
        p_ctx = jnp.exp2(s_ctx - m)
        denom = jnp.sum(p_loc, axis=-1, keepdims=True) + jnp.sum(p_ctx, axis=-1, keepdims=True)
        o = (_dot(p_loc.astype(BF16), v_ref[0, pl.ds(kstart, n_loc), :])
             + _dot(p_ctx.astype(BF16), vc))
        o = _head_unstack(o / denom, GRID_W)
        o_ref[0, pl.ds(pl.multiple_of(j * GRID_W, GRID_W), GRID_W), :] = o.astype(BF16)
        return 0

    lax.fori_loop(0, rows_per_step, body, 0, unroll=True)


def _natten_call(q, k, v, kc, vc, bias):
    b, l, w = q.shape
    rows = l // GRID_W
    rows_per_step = 16
    t = rows_per_step * GRID_W
    full = lambda n: pl.BlockSpec((1, n, w), lambda bi, i: (bi, 0, 0))
    return pl.pallas_call(
        functools.partial(_natten_kernel, rows_per_step=rows_per_step, rows=rows),
        grid=(b, rows // rows_per_step),
        in_specs=[pl.BlockSpec((1, t, w), lambda bi, i: (bi, i, 0)),
                  full(l), full(l), full(kc.shape[1]), full(kc.shape[1]),
                  _const_spec(bias.shape)],
        out_specs=pl.BlockSpec((1, t, w), lambda bi, i: (bi, i, 0)),
        out_shape=jax.ShapeDtypeStruct((b, l, w), BF16),
        compiler_params=_cparams(2),
        name="natten",
    )(q, k, v, kc, vc, bias)


def _na_ctx_kernel(q_ref, k_ref, v_ref, o_ref):
    n = q_ref.shape[1]
    s = _dot_nt(_head_stack(q_ref[0]), k_ref[0])
    m = jnp.max(s, axis=-1, keepdims=True)
    p = jnp.exp2(s - m)
    denom = jnp.sum(p, axis=-1, keepdims=True)
    o = _dot(p.astype(BF16), v_ref[0]) / denom
    o_ref[0] = _head_unstack(o, n).astype(BF16)


def _na_ctx_call(q, k, v):
    b, n, w = q.shape
    spec = pl.BlockSpec((1, n, w), lambda bi: (bi, 0, 0))
    return pl.pallas_call(
        _na_ctx_kernel, grid=(b,), in_specs=[spec, spec, spec], out_specs=spec,
        out_shape=jax.ShapeDtypeStruct((b, n, w), BF16),
        compiler_params=_cparams(1), name="na_ctx",
    )(q, k, v)


def _fnet1_kernel(z_ref, w_ref, y_ref, *, n_inner):
    zb = z_ref[0]
    p = _dot(w_ref[...], zb.reshape(zb.shape[0], zb.shape[1] * zb.shape[2]))
    n1 = p.shape[0] // 2
    re, im = [], []
    for j in range(n_inner):
        zr_c = p[:n1, j * 512:j * 512 + 256]
        zi_c = p[:n1, j * 512 + 256:(j + 1) * 512]
        zr_s = p[n1:, j * 512:j * 512 + 256]
        zi_s = p[n1:, j * 512 + 256:(j + 1) * 512]
        re.append(zr_c + zi_s)
        im.append(zi_c - zr_s)
    y_ref[0, 0] = jnp.concatenate(re, axis=-1).astype(BF16).reshape(n1, n_inner, GROUP_W)
    y_ref[0, 1] = jnp.concatenate(im, axis=-1).astype(BF16).reshape(n1, n_inner, GROUP_W)


def _fnet1_call(z2, w1s, n1):
    b = z2.shape[0]
    n_inner = 16
    return pl.pallas_call(
        functools.partial(_fnet1_kernel, n_inner=n_inner),
        grid=(b, DFT_N2 // n_inner),
        in_specs=[pl.BlockSpec((1, n1, n_inner, 512), lambda bi, i: (bi, 0, i, 0)),
                  _const_spec(w1s.shape)],
        out_specs=pl.BlockSpec((1, 2, n1, n_inner, GROUP_W), lambda bi, i: (bi, 0, 0, i, 0)),
        out_shape=jax.ShapeDtypeStruct((b, 2, n1, DFT_N2, GROUP_W), BF16),
        compiler_params=_cparams(2),
        name="fnet_stage1",
    )(z2, w1s)


def _fnet2_kernel(y_ref, f_ref, o_ref, *, n_inner, norm):
    outs = []
    for j in range(n_inner):
        f = f_ref[j]
        o = _dot(f[:, :DFT_N2], y_ref[0, 0, j]) + _dot(f[:, DFT_N2:], y_ref[0, 1, j])
        outs.append((o * norm).astype(BF16))
    o_ref[0] = jnp.concatenate(outs, axis=-1).reshape(DFT_N2, n_inner, GROUP_W)


def _fnet2_call(y5, ftab, n1, norm):
    b = y5.shape[0]
    n_inner = 16
    return pl.pallas_call(
        functools.partial(_fnet2_kernel, n_inner=n_inner, norm=norm),
        grid=(b, n1 // n_inner),
        in_specs=[pl.BlockSpec((1, 2, n_inner, DFT_N2, GROUP_W), lambda bi, i: (bi, 0, i, 0, 0)),
                  pl.BlockSpec((n_inner, DFT_N2, 2 * DFT_N2), lambda bi, i: (i, 0, 0))],
        out_specs=pl.BlockSpec((1, DFT_N2, n_inner, GROUP_W), lambda bi, i: (bi, 0, i, 0)),
        out_shape=jax.ShapeDtypeStruct((b, DFT_N2, n1, GROUP_W), BF16),
        compiler_params=_cparams(2),
        name="fnet_stage2",
    )(y5, ftab)


def _dft_small_kernel(z_ref, f_ref, o_ref, *, norm):
    z = z_ref[0]
    f = f_ref[...]
    n = z.shape[0]
    o = _dot(f[:, :n], z[:, :GROUP_W]) + _dot(f[:, n:], z[:, GROUP_W:])
    o_ref[0] = (o * norm).astype(BF16)


def _dft_small_call(z, ftab, norm):
    b, n, _ = z.shape
    return pl.pallas_call(
        functools.partial(_dft_small_kernel, norm=norm),
        grid=(b,),
        in_specs=[pl.BlockSpec((1, n, 2 * GROUP_W), lambda bi: (bi, 0, 0)), _const_spec(ftab.shape)],
        out_specs=pl.BlockSpec((1, n, GROUP_W), lambda bi: (bi, 0, 0)),
        out_shape=jax.ShapeDtypeStruct((b, n, GROUP_W), BF16),
        compiler_params=_cparams(1), name="fnet_ctx",
    )(z, ftab)


def _fourier_tables(l):
    if l <= 256:
        m = (np.arange(l)[:, None] * np.arange(l)[None, :]) % l
        ang = 2.0 * np.pi * m / l
        return None, np.concatenate([np.cos(ang), np.sin(ang)], axis=1).astype(np.float32), 0
    n1 = l // DFT_N2
    m1 = (np.arange(n1)[:, None] * np.arange(n1)[None, :]) % n1
    a1 = 2.0 * np.pi * m1 / n1
    w1s = np.concatenate([np.cos(a1), np.sin(a1)], axis=0).astype(np.float32)
    kk = np.arange(n1)[:, None, None] + n1 * np.arange(DFT_N2)[None, :, None]
    m2 = (kk * np.arange(DFT_N2)[None, None, :]) % l
    a2 = 2.0 * np.pi * m2 / l
    ftab = np.concatenate([np.cos(a2), np.sin(a2)], axis=2).astype(np.float32)
    return w1s, ftab, n1


def _fourier_mix(z):
    b, l, _ = z.shape
    norm = float((l * FN_GW) ** -0.5)
    w1s, ftab, n1 = _fourier_tables(l)
    if w1s is None:
        return _dft_small_call(z, jnp.asarray(ftab).astype(BF16), norm)
    y = _fnet1_call(z.reshape(b, n1, DFT_N2, 2 * GROUP_W), jnp.asarray(w1s).astype(BF16), n1)
    o = _fnet2_call(y, jnp.asarray(ftab).astype(BF16), n1, norm)
    return o.reshape(b, l, GROUP_W)


def _conv_kernel(prev_ref, cur_ref, next_ref, w_ref, b_ref, g_ref, beta_ref, o_ref, buf_ref):
    i = pl.program_id(1)
    n = pl.num_programs(1)
    t = cur_ref.shape[1]
    buf_ref[0:CV_HALO, :] = jnp.where(i > 0, prev_ref[0], 0.0)
    buf_ref[CV_HALO:CV_HALO + t, :] = cur_ref[0]
    buf_ref[CV_HALO + t:, :] = jnp.where(i < n - 1, next_ref[0], 0.0)
    w = w_ref[...]
    first = CV_HALO - CV_K // 2
    acc = None
    for res in range(SUBLANES):
        z = None
        for base in range(0, first + CV_K, SUBLANES):
            j = base + res - first
            if 0 <= j < CV_K:
                term = w[j:j + 1, :] * buf_ref[base:base + t + SUBLANES, :]
                z = term if z is None else z + term
        z = z[res:res + t, :]
        acc = z if acc is None else acc + z
    y = acc + b_ref[...]
    mu = jnp.mean(y, axis=-1, keepdims=True)
    var = jnp.mean(jnp.square(y - mu), axis=-1, keepdims=True)
    y = (y - mu) * lax.rsqrt(var + EPS) * g_ref[...] + beta_ref[...]
    o_ref[0] = (y * _sigmoid(y)).astype(BF16)


def _conv_call(cy, w_dw, b_dw, ln_g, ln_b):
    b, l, w = cy.shape
    t = min(l, 512)
    hb = t // CV_HALO
    n_halo = l // CV_HALO
    return pl.pallas_call(
        _conv_kernel,
        grid=(b, l // t),
        in_specs=[pl.BlockSpec((1, CV_HALO, w), lambda bi, i: (bi, jnp.maximum(i * hb - 1, 0), 0)),
                  pl.BlockSpec((1, t, w), lambda bi, i: (bi, i, 0)),
                  pl.BlockSpec((1, CV_HALO, w),
                               lambda bi, i: (bi, jnp.minimum((i + 1) * hb, n_halo - 1), 0)),
                  _const_spec(w_dw.shape), _const_spec(b_dw.shape), _const_spec(ln_g.shape),
                  _const_spec(ln_b.shape)],
        out_specs=pl.BlockSpec((1, t, w), lambda bi, i: (bi, i, 0)),
        out_shape=jax.ShapeDtypeStruct((b, l, w), BF16),
        scratch_shapes=[pltpu.VMEM((t + 2 * CV_HALO, w), F32)],
        compiler_params=_cparams(2),
        name="conformer_conv",
    )(cy, cy, cy, w_dw, b_dw, ln_g, ln_b)


def _outmlp_kernel(x_ref, oa_ref, oc_ref, ob_ref, od_ref, wo_ref, gt1_ref, sh2_ref, sc2_ref,
                   gt2_ref, g2_ref, w1_ref, w2_ref, fg_ref, o_ref, *, last, ff_chunk):
    y = (_dot(oa_ref[0], wo_ref[0:MLA_W, :])
         + _dot(oc_ref[0], wo_ref[MLA_W:MLA_W + GROUP_W, :])
         + _dot(ob_ref[0], wo_ref[MLA_W + GROUP_W:MLA_W + 2 * GROUP_W, :])
         + _dot(od_ref[0], wo_ref[MLA_W + 2 * GROUP_W:, :]))
    x1 = x_ref[0] + gt1_ref[0] * y
    h = (_rms(x1, g2_ref[...]) * (1.0 + sc2_ref[0]) + sh2_ref[0]).astype(BF16)
    ff = jnp.zeros(x1.shape, F32)
    for c in range(D_FF // ff_chunk):
        u = jnp.maximum(_dot(h, w1_ref[0, :, c * ff_chunk:(c + 1) * ff_chunk]), 0.0)
        ff = ff + _dot((u * u).astype(BF16), w2_ref[0, c * ff_chunk:(c + 1) * ff_chunk, :])
    x2 = x1 + gt2_ref[0] * ff
    if last:
        x2 = _rms(x2, fg_ref[...])
    o_ref[0] = x2


def _outmlp_call(x, o_mla, o_na, o_fn, o_cv, wo, gt1, sh2, sc2, gt2, g2, w1, w2, fg, last, layer):
    b, l, d = x.shape
    t = min(l, 512)
    tok = lambda w: pl.BlockSpec((1, t, w), lambda bi, i: (bi, i, 0))
    vec = pl.BlockSpec((1, 1, d), lambda bi, i: (bi, 0, 0))
    slab = lambda w: pl.BlockSpec((1,) + w.shape[1:], lambda bi, i: (layer, 0, 0),
                                  pipeline_mode=pl.Buffered(1))
    return pl.pallas_call(
        functools.partial(_outmlp_kernel, last=last, ff_chunk=1024),
        grid=(b, l // t),
        in_specs=[tok(d), tok(MLA_W), tok(GROUP_W), tok(GROUP_W), tok(GROUP_W),
                  _const_spec(wo.shape), vec, vec, vec, vec, _const_spec(g2.shape),
                  slab(w1), slab(w2), _const_spec(fg.shape)],
        out_specs=tok(d),
        out_shape=jax.ShapeDtypeStruct((b, l, d), F32),
        compiler_params=_cparams(2),
        name="outproj_mlp",
    )(x, o_mla, o_na, o_fn, o_cv, wo, gt1, sh2, sc2, gt2, g2, w1, w2, fg)


def _rope_tables(l, with_rope):
    per_axis = MLA_ROPE // 2
    n_freq = per_axis // 2

    def slot(nope, row_part, col_part, sign):
        n = row_part.shape[0]
        return jnp.concatenate([jnp.full((n, MLA_NOPE), nope, F32), sign * row_part, sign * col_part,
                                row_part, col_part, jnp.zeros((n, HEAD_SLOT - MLA_QK), F32)], axis=-1)

    if not with_rope:
        one, zero = jnp.ones((l, n_freq), F32), jnp.zeros((l, n_freq), F32)
        return jnp.concatenate([slot(1.0, one, one, 1.0), slot(0.0, zero, zero, -1.0)], axis=-1)
    inv = ROPE_BASE ** (-jnp.arange(0, per_axis, 2, dtype=F32) / per_axis)
    r_ang = jnp.arange(l // GRID_W).astype(F32)[:, None] * inv
    c_ang = jnp.arange(GRID_W).astype(F32)[:, None] * inv
    zr, zc = jnp.zeros_like(r_ang), jnp.zeros_like(c_ang)
    by_row = jnp.concatenate([slot(0.0, jnp.cos(r_ang), zr, 1.0), slot(0.0, jnp.sin(r_ang), zr, -1.0)], axis=-1)
    by_col = jnp.concatenate([slot(1.0, zc, jnp.cos(c_ang), 1.0), slot(0.0, zc, jnp.sin(c_ang), -1.0)], axis=-1)
    return (by_row[:, None, :] + by_col[None, :, :]).reshape(l, 2 * HEAD_SLOT)


def _swap_halves(w):
    half = w.shape[-1] // 2
    return jnp.concatenate([w[..., half:], w[..., :half]], axis=-1)


def _layer_weights(w_in, w_uq, w_ukv, w_out, rpb):
    d = w_in.shape[0]
    k_r_end = 256 + 128 + MLA_ROPE
    win = jnp.concatenate([w_in[:, :k_r_end], jnp.zeros((d, HEAD_SLOT - MLA_ROPE), F32),
                           w_in[:, k_r_end:]], axis=1).astype(BF16)

    rq = w_uq.shape[0]
    w3 = w_uq.reshape(rq, N_HEADS, MLA_QK)
    zpad = jnp.zeros((rq, N_HEADS, HEAD_SLOT - MLA_QK), F32)
    plain = jnp.concatenate([w3, zpad], axis=-1)
    swapped = jnp.concatenate([jnp.zeros((rq, N_HEADS, MLA_NOPE), F32),
                               _swap_halves(w3[..., MLA_NOPE:]), zpad], axis=-1)
    wq = jnp.concatenate([plain.reshape(rq, MLA_W), swapped.reshape(rq, MLA_W)], axis=1).astype(BF16)

    rkv = w_ukv.shape[0]
    u3 = w_ukv.reshape(rkv, N_HEADS, MLA_NOPE + MLA_V)
    eye = jnp.eye(MLA_ROPE, dtype=F32)
    slot_pad_r = jnp.zeros((MLA_ROPE, N_HEADS, HEAD_SLOT - MLA_QK), F32)

    def slots(top, rope_block):
        top = jnp.concatenate([top, jnp.zeros((rkv, N_HEADS, HEAD_SLOT - top.shape[-1]), F32)], axis=-1)
        mid = jnp.concatenate([jnp.zeros((MLA_ROPE, N_HEADS, MLA_NOPE), F32),
                               jnp.broadcast_to(rope_block[:, None, :], (MLA_ROPE, N_HEADS, MLA_ROPE)),
                               slot_pad_r], axis=-1)
        bot = jnp.zeros((2 * HEAD_SLOT - rkv - MLA_ROPE, N_HEADS, HEAD_SLOT), F32)
        return jnp.concatenate([top, mid, bot], axis=0).reshape(2 * HEAD_SLOT, MLA_W)

    k_plain = slots(u3[..., :MLA_NOPE], eye)
    k_swap = slots(jnp.zeros((rkv, N_HEADS, MLA_NOPE), F32), _swap_halves(eye))
    wkv = jnp.concatenate([k_plain, k_swap], axis=1).astype(BF16)

    vt3 = jnp.transpose(u3[..., MLA_NOPE:], (1, 2, 0))
    vt3 = jnp.pad(vt3, ((0, 0), (0, VT_ROWS - MLA_V), (0, 2 * HEAD_SLOT - rkv)))
    wvt = vt3.reshape(N_HEADS * VT_ROWS, 2 * HEAD_SLOT).astype(BF16)

    o3 = w_out[:N_HEADS * MLA_V].reshape(N_HEADS, MLA_V, d)
    o3 = jnp.concatenate([o3, jnp.zeros((N_HEADS, HEAD_SLOT - MLA_V, d), F32)], axis=1)
    wo = jnp.concatenate([o3.reshape(MLA_W, d), w_out[N_HEADS * MLA_V:]], axis=0).astype(BF16)

    qc = np.arange(GRID_W)[:, None]
    kc = np.arange(GRID_W)[None, :]
    ws = np.clip(qc - NA_WIN_W // 2, 0, GRID_W - NA_WIN_W)
    in_win = (kc >= ws) & (kc < ws + NA_WIN_W)
    edge = GRID_W - NA_WIN_W
    n = 2 * GRID_W - 1
    rp = jnp.pad(rpb.astype(F32), ((0, 0), (0, 0), (edge, edge)))
    tiled = jnp.tile(rp, (1, 1, GRID_W + 1))[:, :, :GRID_W * (n + 1)]
    toep = tiled.reshape(rp.shape[0], rp.shape[1], GRID_W, n + 1)[:, :, ::-1, :GRID_W]
    t2 = jnp.where(in_win[None, None], toep * LOG2_E, NEG_INF)
    bias = jnp.concatenate([t2[:, :-1], t2[:, 1:]], axis=-1)
    return win, wq, wkv, wvt, wo, bias


def _channel_dft():
    m = (np.arange(FN_GW)[:, None] * np.arange(FN_GW)[None, :]) % FN_GW
    ang = 2.0 * np.pi * m / FN_GW
    eye = np.eye(FN_GROUPS)
    return np.concatenate([np.kron(eye, np.cos(ang)), -np.kron(eye, np.sin(ang))],
                          axis=1).astype(np.float32)


def kernel(x, c, ctx, c_ctx, w_mod, b_mod, norm1_g, norm2_g, w_in, mla_q_norm, mla_w_uq, mla_kv_norm, mla_w_ukv, na_rpb, cv_w_dw, cv_b_dw, cv_ln_g, cv_ln_b, w_out, w_ff1, w_ff2, final_g):
    depth = w_mod.shape[0]
    b, s, d = x.shape
    n_ctx = ctx.shape[1]

    cc = jnp.concatenate([c, c_ctx[None, :], jnp.zeros((8 - b - 1, d), F32)], axis=0)
    mods = _mod_call(cc, w_mod, b_mod)

    cs_lat = _rope_tables(s, True)
    cs_ctx = _rope_tables(n_ctx, False)
    dc = jnp.asarray(_channel_dft()).astype(BF16)
    row = lambda p: p.reshape(1, -1)
    fg = row(final_g)
    w1 = w_ff1.astype(BF16)
    w2 = w_ff2.astype(BF16)

    xc = ctx
    for i in range(depth):
        last = i == depth - 1
        win, wq, wkv, wvt, wo, bias = _layer_weights(w_in[i], mla_w_uq[i], mla_w_ukv[i], w_out[i], na_rpb[i])
        mx =[m.reshape(b, 1, d) for m in jnp.split(mods[i, :b], 6, axis=-1)]
        mc = [jnp.broadcast_to(m.reshape(1, 1, d), (b, 1, d))
              for m in jnp.split(mods[i, b:b + 1], 6, axis=-1)]
        proj = functools.partial(_inproj_call, g1=row(norm1_g[i]), win=win, qn=row(mla_q_norm[i]),
                                 wq=wq, kvn=row(mla_kv_norm[i]), wkv=wkv, wvt=wvt, dc=dc)
        conv = functools.partial(_conv_call, w_dw=cv_w_dw[i], b_dw=row(cv_b_dw[i]),
                                 ln_g=row(cv_ln_g[i]), ln_b=row(cv_ln_b[i]))
        mlp = functools.partial(_outmlp_call, wo=wo, g2=row(norm2_g[i]), w1=w1, w2=w2, fg=fg, layer=i)

        q, k, vt, nq, nk, nv, z, cy = proj(x, mx[0], mx[1], cs=cs_lat)
        cq, ck, cvt, cnq, cnk, cnv, cz, ccy = proj(xc, mc[0], mc[1], cs=cs_ctx)

        o_mla = _flash_call(q, k, vt, ck, cvt)
        o_na = _natten_call(nq, nk, nv, cnk, cnv, bias)
        o_fn = _fourier_mix(z)
        o_cv = conv(cy)
        x = mlp(x, o_mla, o_na, o_fn, o_cv, gt1=mx[2], sh2=mx[3], sc2=mx[4], gt2=mx[5], last=last)

        if not last:
            co_mla = _flash_call(cq, ck, cvt)
            co_na = _na_ctx_call(cnq, cnk, cnv)
            co_fn = _fourier_mix(cz)
            co_cv = conv(ccy)
            xc = mlp(xc, co_mla, co_na, co_fn, co_cv, gt1=mc[2], sh2=mc[3], sc2=mc[4], gt2=mc[5],
                     last=False)
    return x
```

```python
import functools

import jax
import jax.numpy as jnp
import numpy as np
from jax import lax
from jax.experimental import pallas as pl
from jax.experimental.pallas import tpu as pltpu

F32 = jnp.float32
BF16 = jnp.bfloat16

EPS = 1e-6
ROPE_BASE = 10000.0
NEG_INF = -1e30

D_MODEL = 1024
GRID_W = 64
N_HEADS = 4
GROUP_W = 256
HEAD_SLOT = 128
MLA_NOPE = 64
MLA_ROPE = 32
MLA_QK = MLA_NOPE + MLA_ROPE
MLA_V = 64
MLA_W = N_HEADS * HEAD_SLOT
VT_ROWS = MLA_V
LOG2_E = 1.4426950408889634
NA_HD = 64
NA_WIN_H = 8
NA_WIN_W = 16
FN_GROUPS = 4
FN_GW = GROUP_W // FN_GROUPS
CV_K = 31
CV_HALO = 16
SUBLANES = 8
D_FF = 4 * D_MODEL
DFT_N2 = 128
V7X_VMEM_LIMIT = 56 * 1024 * 1024

MOD_TILE_N = 1536
INPROJ_TILE = 1024
TOKEN_TILE = 512
ATTN_TILE_Q = 2048
ATTN_TILE_K = 1024
ATTN_HEADS_PER_STEP = 2
NATTEN_ROWS = 16
FNET_BLOCK = 16
FF_CHUNK = 1024


def _cparams(n_grid):
    return pltpu.CompilerParams(dimension_semantics=("parallel",) * n_grid,
                                vmem_limit_bytes=V7X_VMEM_LIMIT)


def _const_spec(shape):
    nd = len(shape)
    return pl.BlockSpec(shape, lambda *_: (0,) * nd, pipeline_mode=pl.Buffered(1))


def _sigmoid(v):
    return 1.0 / (1.0 + jnp.exp(-v))


def _rms(v, g):
    return v * lax.rsqrt(jnp.mean(v * v, axis=-1, keepdims=True) + EPS) * g


def _dot(a, b):
    return jnp.dot(a, b, preferred_element_type=F32)


def _dot_nt(a, b):
    return lax.dot_general(a, b, (((1,), (1,)), ((), ())), preferred_element_type=F32)


def _mod_kernel(c_ref, w_ref, b_ref, o_ref):
    c = c_ref[...]
    s = (c * _sigmoid(c)).astype(BF16)
    o_ref[0] = _dot(s, w_ref[0].astype(BF16)) + b_ref[0]


def _mod_call(cc, w_mod, b_mod):
    depth, d, n = w_mod.shape
    tn = MOD_TILE_N
    return pl.pallas_call(
        _mod_kernel,
        grid=(depth, n // tn),
        in_specs=[pl.BlockSpec(cc.shape, lambda l, j: (0, 0)),
                  pl.BlockSpec((1, d, tn), lambda l, j: (l, 0, j)),
                  pl.BlockSpec((1, 1, tn), lambda l, j: (l, 0, j))],
        out_specs=pl.BlockSpec((1, cc.shape[0], tn), lambda l, j: (l, 0, j)),
        out_shape=jax.ShapeDtypeStruct((depth, cc.shape[0], n), F32),
        compiler_params=_cparams(2),
        name="mod",
    )(cc, w_mod, b_mod.reshape(depth, 1, n))


def _inproj_kernel(x_ref, sh_ref, sc_ref, g1_ref, win_ref, qn_ref, wq_ref, kvn_ref, wkv_ref,
                   wvt_ref, dc_ref, cs_ref,
                   q_ref, k_ref, vt_ref, nq_ref, nk_ref, nv_ref, z_ref, cy_ref):
    x = x_ref[0]
    h = _rms(x, g1_ref[...]) * (1.0 + sc_ref[0]) + sh_ref[0]
    p = _dot(h.astype(BF16), win_ref[...])

    cs = cs_ref[...]
    cos_t, sin_t = cs[:, :HEAD_SLOT], cs[:, HEAD_SLOT:]
    cos4 = jnp.concatenate([cos_t] * N_HEADS, axis=-1)
    sin4 = jnp.concatenate([sin_t] * N_HEADS, axis=-1)

    r = _rms(p[:, 0:256], qn_ref[...]).astype(BF16)
    qq = _dot(r, wq_ref[...])
    q = (qq[:, :MLA_W] * cos4 + qq[:, MLA_W:] * sin4) * (MLA_QK ** -0.5 * LOG2_E)
    q_ref[0] = q.astype(BF16)

    kvr = p[:, 256:512]
    kvn = _rms(kvr[:, :128], kvn_ref[...])
    comb = jnp.concatenate([kvn, kvr[:, 128:]], axis=-1).astype(BF16)
    kk = _dot(comb, wkv_ref[...])
    k = kk[:, :MLA_W] * cos4 + kk[:, MLA_W:] * sin4
    k_ref[0] = k.astype(BF16)
    vt_ref[0] = _dot_nt(wvt_ref[...], comb).astype(BF16)

    nq_ref[0] = (p[:, 512:768] * (NA_HD ** -0.5 * LOG2_E)).astype(BF16)
    nk_ref[0] = p[:, 768:1024].astype(BF16)
    nv_ref[0] = p[:, 1024:1280].astype(BF16)

    z_ref[0] = _dot(p[:, 1280:1536].astype(BF16), dc_ref[...]).astype(BF16)

    cy_ref[0] = p[:, 1536:1792] * _sigmoid(p[:, 1792:2048])


def _inproj_call(x, sh, sc, g1, win, qn, wq, kvn, wkv, wvt, dc, cs):
    b, l, d = x.shape
    t = min(l, INPROJ_TILE)
    tok = lambda w: pl.BlockSpec((1, t, w), lambda bi, i: (bi, i, 0))
    vec = pl.BlockSpec((1, 1, d), lambda bi, i: (bi, 0, 0))
    outs = [(MLA_W, BF16), (MLA_W, BF16), None, (GROUP_W, BF16), (GROUP_W, BF16),
            (GROUP_W, BF16), (2 * GROUP_W, BF16), (GROUP_W, F32)]
    vt_rows = N_HEADS * VT_ROWS
    out_specs = [tok(o[0]) if o else pl.BlockSpec((1, vt_rows, t), lambda bi, i: (bi, 0, i))
                 for o in outs]
    out_shape = [jax.ShapeDtypeStruct((b, l, o[0]), o[1]) if o
                 else jax.ShapeDtypeStruct((b, vt_rows, l), BF16) for o in outs]
    return pl.pallas_call(
        _inproj_kernel,
        grid=(b, l // t),
        in_specs=[tok(d), vec, vec, _const_spec(g1.shape), _const_spec(win.shape),
                  _const_spec(qn.shape), _const_spec(wq.shape), _const_spec(kvn.shape),
                  _const_spec(wkv.shape), _const_spec(wvt.shape), _const_spec(dc.shape),
                  pl.BlockSpec((t, 2 * HEAD_SLOT), lambda bi, i: (i, 0))],
        out_specs=out_specs,
        out_shape=out_shape,
        compiler_params=_cparams(2),
        name="inproj",
    )(x, sh, sc, g1, win, qn, wq, kvn, wkv, wvt, dc, cs)


def _flash_kernel(*refs, n_chunks, tk, has_extra):
    if has_extra:
        q_ref, k_ref, vt_ref, k2_ref, vt2_ref, o_ref, sa_ref, sb_ref, sx_ref = refs
    else:
        q_ref, k_ref, vt_ref, o_ref, sa_ref, sb_ref = refs
    tq = q_ref.shape[1]

    def one_head(hh):
        lanes = slice(hh * HEAD_SLOT, (hh + 1) * HEAD_SLOT)
        vrows = slice(hh * VT_ROWS, (hh + 1) * VT_ROWS)
        q = q_ref[0, :, lanes]

        def scores(kc, s_ref):
            st = _dot_nt(kc, q)
            s_ref[...] = st
            return jnp.max(st, axis=0, keepdims=True)

        def accumulate(s_ref, mx, vtc, carry):
            m, (acc, l) = carry
            m_new = jnp.maximum(m, mx)
            p = jnp.exp2(s_ref[...] - m_new)
            alpha = jnp.exp2(m - m_new)
            l = alpha * l + jnp.sum(p, axis=0, keepdims=True)
            return m_new, (alpha * acc + _dot(vtc, p.astype(BF16)), l)

        def k_chunk(c):
            return k_ref[0, pl.ds(pl.multiple_of(c * tk, tk), tk), lanes]

        def vt_chunk(c):
            return vt_ref[0, vrows, pl.ds(pl.multiple_of(c * tk, tk), tk)]

        def pair(j, carry):
            mx_a, m, acc = carry
            mx_b = scores(k_chunk(2 * j + 1), sb_ref)
            m, acc = accumulate(sa_ref, mx_a, vt_chunk(2 * j), (m, acc))
            mx_a = scores(k_chunk(2 * j + 2), sa_ref)
            m, acc = accumulate(sb_ref, mx_b, vt_chunk(2 * j + 1), (m, acc))
            return mx_a, m, acc

        n_pairs = (n_chunks - 1) // 2
        carry = (scores(k_chunk(0), sa_ref), jnp.full((1, tq), -jnp.inf, F32),
                 (jnp.zeros((VT_ROWS, tq), F32), jnp.zeros((1, tq), F32)))
        mx, m, acc = lax.fori_loop(0, n_pairs, pair, carry)
        pending = (sa_ref, mx, vt_chunk(2 * n_pairs))
        tail = [(k_chunk(c), vt_chunk(c), sb_ref if c % 2 else sa_ref)
                for c in range(2 * n_pairs + 1, n_chunks)]
        if has_extra:
            tail.append((k2_ref[0, :, lanes], vt2_ref[0, vrows, :], sx_ref))
        for kc, vtc, s_ref in tail:
            mx_next = scores(kc, s_ref)
            m, acc = accumulate(*pending, (m, acc))
            pending = (s_ref, mx_next, vtc)
        m, (acc, l) = accumulate(*pending, (m, acc))
        return acc / l

    o_t = jnp.concatenate([one_head(hh) for hh in range(ATTN_HEADS_PER_STEP)], axis=0)
    o_ref[0] = o_t.T.astype(BF16)


def _flash_call(q, k, vt, k2=None, vt2=None):
    b, lq, _ = q.shape
    lk = k.shape[1]
    tq = min(lq, ATTN_TILE_Q)
    tk = min(lk, ATTN_TILE_K)
    has_extra = k2 is not None
    hps = ATTN_HEADS_PER_STEP
    qspec = pl.BlockSpec((1, tq, hps * HEAD_SLOT), lambda bi, h, i: (bi, i, h))
    kspec = lambda n: pl.BlockSpec((1, n, hps * HEAD_SLOT), lambda bi, h, i: (bi, 0, h))
    vspec = lambda n: pl.BlockSpec((1, hps * VT_ROWS, n), lambda bi, h, i: (bi, h, 0))
    ospec = pl.BlockSpec((1, tq, hps * MLA_V), lambda bi, h, i: (bi, i, h))
    in_specs = [qspec, kspec(lk), vspec(lk)]
    args = [q, k, vt]
    scratch = [pltpu.VMEM((tk, tq), F32), pltpu.VMEM((tk, tq), F32)]
    if has_extra:
        in_specs += [kspec(k2.shape[1]), vspec(k2.shape[1])]
        args += [k2, vt2]
        scratch.append(pltpu.VMEM((k2.shape[1], tq), F32))
    return pl.pallas_call(
        functools.partial(_flash_kernel, n_chunks=lk // tk, tk=tk, has_extra=has_extra),
        grid=(b, N_HEADS // hps, lq // tq),
        in_specs=in_specs,
        out_specs=ospec,
        out_shape=jax.ShapeDtypeStruct((b, lq, N_HEADS * MLA_V), BF16),
        scratch_shapes=scratch,
        compiler_params=_cparams(3),
        name="mla_attn",
    )(*args)


def _head_stack(q):
    lane = lax.broadcasted_iota(jnp.int32, q.shape, 1)
    return jnp.concatenate(
        [jnp.where(lane // NA_HD == h, q, jnp.zeros_like(q)) for h in range(N_HEADS)], axis=0)


def _head_unstack(o, n):
    lane = lax.broadcasted_iota(jnp.int32, (n, GROUP_W), 1)
    out = jnp.zeros((n, GROUP_W), F32)
    for h in range(N_HEADS):
        out = out + jnp.where(lane // NA_HD == h, o[h * n:(h + 1) * n], 0.0)
    return out


def _natten_kernel(q_ref, k_ref, v_ref, kc_ref, vc_ref, bias_ref, o_ref, *, rows_per_step, rows):
    blk = pl.program_id(1)
    kc = kc_ref[0]
    vc = vc_ref[0]
    n_loc = NA_WIN_H * GRID_W

    def scores(j):
        r = blk * rows_per_step + j
        rs = jnp.clip(r - NA_WIN_H // 2, 0, rows - NA_WIN_H)
        d0 = rs - r + (NA_WIN_H - 1)
        qs = _head_stack(q_ref[0, j * GRID_W:(j + 1) * GRID_W, :])
        kstart = pl.multiple_of(rs * GRID_W, GRID_W)
        s_loc = _dot_nt(qs, k_ref[0, pl.ds(kstart, n_loc), :])
        bias = jnp.concatenate(
            [jnp.concatenate([bias_ref[h, d0 + 2 * w] for w in range(NA_WIN_H // 2)], axis=-1)
             for h in range(N_HEADS)], axis=0)
        return s_loc + bias, _dot_nt(qs, kc), kstart

    def finish(j, s_loc, s_ctx, kstart):
        m = jnp.maximum(jnp.max(s_loc, axis=-1, keepdims=True),
                        jnp.max(s_ctx, axis=-1, keepdims=True))
        p_loc = jnp.exp2(s_loc - m)
        p_ctx = jnp.exp2(s_ctx - m)
        denom = jnp.sum(p_loc, axis=-1, keepdims=True) + jnp.sum(p_ctx, axis=-1, keepdims=True)
        o = (_dot(p_loc.astype(BF16), v_ref[0, pl.ds(kstart, n_loc), :])
             + _dot(p_ctx.astype(BF16), vc))
        o = _head_unstack(o / denom, GRID_W)
        o_ref[0, j * GRID_W:(j + 1) * GRID_W, :] = o.astype(BF16)

    pending = scores(0)
    for j in range(rows_per_step):
        nxt = scores(j + 1) if j + 1 < rows_per_step else None
        finish(j, *pending)
        pending = nxt


def _natten_call(q, k, v, kc, vc, bias):
    b, l, w = q.shape
    rows = l // GRID_W
    rows_per_step = NATTEN_ROWS
    t = rows_per_step * GRID_W
    full = lambda n: pl.BlockSpec((1, n, w), lambda bi, i: (bi, 0, 0))
    return pl.pallas_call(
        functools.partial(_natten_kernel, rows_per_step=rows_per_step, rows=rows),
        grid=(b, rows // rows_per_step),
        in_specs=[pl.BlockSpec((1, t, w), lambda bi, i: (bi, i, 0)),
                  full(l), full(l), full(kc.shape[1]), full(kc.shape[1]),
                  _const_spec(bias.shape)],
        out_specs=pl.BlockSpec((1, t, w), lambda bi, i: (bi, i, 0)),
        out_shape=jax.ShapeDtypeStruct((b, l, w), BF16),
        compiler_params=_cparams(2),
        name="natten",
    )(q, k, v, kc, vc, bias)


def _na_ctx_kernel(q_ref, k_ref, v_ref, o_ref):
    n = q_ref.shape[1]
    s = _dot_nt(_head_stack(q_ref[0]), k_ref[0])
    m = jnp.max(s, axis=-1, keepdims=True)
    p = jnp.exp2(s - m)
    denom = jnp.sum(p, axis=-1, keepdims=True)
    o = _dot(p.astype(BF16), v_ref[0]) / denom
    o_ref[0] = _head_unstack(o, n).astype(BF16)


def _na_ctx_call(q, k, v):
    b, n, w = q.shape
    spec = pl.BlockSpec((1, n, w), lambda bi: (bi, 0, 0))
    return pl.pallas_call(
        _na_ctx_kernel, grid=(b,), in_specs=[spec, spec, spec], out_specs=spec,
        out_shape=jax.ShapeDtypeStruct((b, n, w), BF16),
        compiler_params=_cparams(1), name="na_ctx",
    )(q, k, v)


def _fnet1_kernel(z_ref, w_ref, y_ref, *, n_inner):
    zb = z_ref[0]
    p = _dot(w_ref[...], zb.reshape(zb.shape[0], zb.shape[1] * zb.shape[2]))
    n1 = p.shape[0] // 2
    re, im = [], []
    for j in range(n_inner):
        zr_c = p[:n1, j * 512:j * 512 + 256]
        zi_c = p[:n1, j * 512 + 256:(j + 1) * 512]
        zr_s = p[n1:, j * 512:j * 512 + 256]
        zi_s = p[n1:, j * 512 + 256:(j + 1) * 512]
        re.append(zr_c + zi_s)
        im.append(zi_c - zr_s)
    y_ref[0, 0] = jnp.concatenate(re, axis=-1).astype(BF16).reshape(n1, n_inner, GROUP_W)
    y_ref[0, 1] = jnp.concatenate(im, axis=-1).astype(BF16).reshape(n1, n_inner, GROUP_W)


def _fnet1_call(z2, w1s, n1):
    b = z2.shape[0]
    n_inner = FNET_BLOCK
    return pl.pallas_call(
        functools.partial(_fnet1_kernel, n_inner=n_inner),
        grid=(b, DFT_N2 // n_inner),
        in_specs=[pl.BlockSpec((1, n1, n_inner, 512), lambda bi, i: (bi, 0, i, 0)),
                  _const_spec(w1s.shape)],
        out_specs=pl.BlockSpec((1, 2, n1, n_inner, GROUP_W), lambda bi, i: (bi, 0, 0, i, 0)),
        out_shape=jax.ShapeDtypeStruct((b, 2, n1, DFT_N2, GROUP_W), BF16),
        compiler_params=_cparams(2),
        name="fnet_stage1",
    )(z2, w1s)


def _fnet2_kernel(y_ref, f_ref, o_ref, *, n_inner, norm):
    outs = []
    for j in range(n_inner):
        f = f_ref[j]
        o = _dot(f[:, :DFT_N2], y_ref[0, 0, j]) + _dot(f[:, DFT_N2:], y_ref[0, 1, j])
        outs.append((o * norm).astype(BF16))
    o_ref[0] = jnp.concatenate(outs, axis=-1).reshape(DFT_N2, n_inner, GROUP_W)


def _fnet2_call(y5, ftab, n1, norm):
    b = y5.shape[0]
    n_inner = FNET_BLOCK
    return pl.pallas_call(
        functools.partial(_fnet2_kernel, n_inner=n_inner, norm=norm),
        grid=(b, n1 // n_inner),
        in_specs=[pl.BlockSpec((1, 2, n_inner, DFT_N2, GROUP_W), lambda bi, i: (bi, 0, i, 0, 0)),
                  pl.BlockSpec((n_inner, DFT_N2, 2 * DFT_N2), lambda bi, i: (i, 0, 0))],
        out_specs=pl.BlockSpec((1, DFT_N2, n_inner, GROUP_W), lambda bi, i: (bi, 0, i, 0)),
        out_shape=jax.ShapeDtypeStruct((b, DFT_N2, n1, GROUP_W), BF16),
        compiler_params=_cparams(2),
        name="fnet_stage2",
    )(y5, ftab)


def _dft_small_kernel(z_ref, f_ref, o_ref, *, norm):
    z = z_ref[0]
    f = f_ref[...]
    n = z.shape[0]
    o = _dot(f[:, :n], z[:, :GROUP_W]) + _dot(f[:, n:], z[:, GROUP_W:])
    o_ref[0] = (o * norm).astype(BF16)


def _dft_small_call(z, ftab, norm):
    b, n, _ = z.shape
    return pl.pallas_call(
        functools.partial(_dft_small_kernel, norm=norm),
        grid=(b,),
        in_specs=[pl.BlockSpec((1, n, 2 * GROUP_W), lambda bi: (bi, 0, 0)), _const_spec(ftab.shape)],
        out_specs=pl.BlockSpec((1, n, GROUP_W), lambda bi: (bi, 0, 0)),
        out_shape=jax.ShapeDtypeStruct((b, n, GROUP_W), BF16),
        compiler_params=_cparams(1), name="fnet_ctx",
    )(z, ftab)


def _fourier_tables(l):
    if l <= 256:
        m = (np.arange(l)[:, None] * np.arange(l)[None, :]) % l
        ang = 2.0 * np.pi * m / l
        return None, np.concatenate([np.cos(ang), np.sin(ang)], axis=1).astype(np.float32), 0
    n1 = l // DFT_N2
    m1 = (np.arange(n1)[:, None] * np.arange(n1)[None, :]) % n1
    a1 = 2.0 * np.pi * m1 / n1
    w1s = np.concatenate([np.cos(a1), np.sin(a1)], axis=0).astype(np.float32)
    kk = np.arange(n1)[:, None, None] + n1 * np.arange(DFT_N2)[None, :, None]
    m2 = (kk * np.arange(DFT_N2)[None, None, :]) % l
    a2 = 2.0 * np.pi * m2 / l
    ftab = np.concatenate([np.cos(a2), np.sin(a2)], axis=2).astype(np.float32)
    return w1s, ftab, n1


def _fourier_mix(z):
    b, l, _ = z.shape
    norm = float((l * FN_GW) ** -0.5)
    w1s, ftab, n1 = _fourier_tables(l)
    if w1s is None:
        return _dft_small_call(z, jnp.asarray(ftab).astype(BF16), norm)
    y = _fnet1_call(z.reshape(b, n1, DFT_N2, 2 * GROUP_W), jnp.asarray(w1s).astype(BF16), n1)
    o = _fnet2_call(y, jnp.asarray(ftab).astype(BF16), n1, norm)
    return o.reshape(b, l, GROUP_W)


def _conv_kernel(prev_ref, cur_ref, next_ref, w_ref, b_ref, g_ref, beta_ref, o_ref, buf_ref):
    i = pl.program_id(1)
    n = pl.num_programs(1)
    t = cur_ref.shape[1]
    buf_ref[0:CV_HALO, :] = jnp.where(i > 0, prev_ref[0], 0.0)
    buf_ref[CV_HALO:CV_HALO + t, :] = cur_ref[0]
    buf_ref[CV_HALO + t:, :] = jnp.where(i < n - 1, next_ref[0], 0.0)
    w = w_ref[...]
    first = CV_HALO - CV_K // 2
    acc = None
    for res in range(SUBLANES):
        z = None
        for base in range(0, first + CV_K, SUBLANES):
            j = base + res - first
            if 0 <= j < CV_K:
                term = w[j:j + 1, :] * buf_ref[base:base + t + SUBLANES, :]
                z = term if z is None else z + term
        z = z[res:res + t, :]
        acc = z if acc is None else acc + z
    y = acc + b_ref[...]
    mu = jnp.mean(y, axis=-1, keepdims=True)
    var = jnp.mean(jnp.square(y - mu), axis=-1, keepdims=True)
    y = (y - mu) * lax.rsqrt(var + EPS) * g_ref[...] + beta_ref[...]
    o_ref[0] = (y * _sigmoid(y)).astype(BF16)


def _conv_call(cy, w_dw, b_dw, ln_g, ln_b):
    b, l, w = cy.shape
    t = min(l, TOKEN_TILE)
    hb = t // CV_HALO
    n_halo = l // CV_HALO
    return pl.pallas_call(
        _conv_kernel,
        grid=(b, l // t),
        in_specs=[pl.BlockSpec((1, CV_HALO, w), lambda bi, i: (bi, jnp.maximum(i * hb - 1, 0), 0)),
                  pl.BlockSpec((1, t, w), lambda bi, i: (bi, i, 0)),
                  pl.BlockSpec((1, CV_HALO, w),
                               lambda bi, i: (bi, jnp.minimum((i + 1) * hb, n_halo - 1), 0)),
                  _const_spec(w_dw.shape), _const_spec(b_dw.shape), _const_spec(ln_g.shape),
                  _const_spec(ln_b.shape)],
        out_specs=pl.BlockSpec((1, t, w), lambda bi, i: (bi, i, 0)),
        out_shape=jax.ShapeDtypeStruct((b, l, w), BF16),
        scratch_shapes=[pltpu.VMEM((t + 2 * CV_HALO, w), F32)],
        compiler_params=_cparams(2),
        name="conformer_conv",
    )(cy, cy, cy, w_dw, b_dw, ln_g, ln_b)


def _outmlp_kernel(x_ref, oa_ref, oc_ref, ob_ref, od_ref, wo_ref, gt1_ref, sh2_ref, sc2_ref,
                   gt2_ref, g2_ref, w1_ref, w2_ref, fg_ref, o_ref, *, last, ff_chunk):
    y = (_dot(oa_ref[0], wo_ref[0:GROUP_W, :])
         + _dot(oc_ref[0], wo_ref[GROUP_W:2 * GROUP_W, :])
         + _dot(ob_ref[0], wo_ref[2 * GROUP_W:3 * GROUP_W, :])
         + _dot(od_ref[0], wo_ref[3 * GROUP_W:, :]))
    x1 = x_ref[0] + gt1_ref[0] * y
    h = (_rms(x1, g2_ref[...]) * (1.0 + sc2_ref[0]) + sh2_ref[0]).astype(BF16)
    ff = jnp.zeros(x1.shape, F32)
    for c in range(D_FF // ff_chunk):
        u = jnp.maximum(_dot(h, w1_ref[0, :, c * ff_chunk:(c + 1) * ff_chunk]), 0.0)
        ff = ff + _dot((u * u).astype(BF16), w2_ref[0, c * ff_chunk:(c + 1) * ff_chunk, :])
    x2 = x1 + gt2_ref[0] * ff
    if last:
        x2 = _rms(x2, fg_ref[...])
    o_ref[0] = x2


def _outmlp_call(x, o_mla, o_na, o_fn, o_cv, wo, gt1, sh2, sc2, gt2, g2, w1, w2, fg, last, layer):
    b, l, d = x.shape
    t = min(l, TOKEN_TILE)
    tok = lambda w: pl.BlockSpec((1, t, w), lambda bi, i: (bi, i, 0))
    vec = pl.BlockSpec((1, 1, d), lambda bi, i: (bi, 0, 0))
    slab = lambda w: pl.BlockSpec((1,) + w.shape[1:], lambda bi, i: (layer, 0, 0),
                                  pipeline_mode=pl.Buffered(1))
    return pl.pallas_call(
        functools.partial(_outmlp_kernel, last=last, ff_chunk=FF_CHUNK),
        grid=(b, l // t),
        in_specs=[tok(d), tok(GROUP_W), tok(GROUP_W), tok(GROUP_W), tok(GROUP_W),
                  _const_spec(wo.shape), vec, vec, vec, vec, _const_spec(g2.shape),
                  slab(w1), slab(w2), _const_spec(fg.shape)],
        out_specs=tok(d),
        out_shape=jax.ShapeDtypeStruct((b, l, d), F32),
        compiler_params=_cparams(2),
        name="outproj_mlp",
    )(x, o_mla, o_na, o_fn, o_cv, wo, gt1, sh2, sc2, gt2, g2, w1, w2, fg)


def _rope_tables(l, with_rope):
    per_axis = MLA_ROPE // 2
    n_freq = per_axis // 2

    def slot(nope, row_part, col_part, sign):
        n = row_part.shape[0]
        return jnp.concatenate([jnp.full((n, MLA_NOPE), nope, F32), sign * row_part, sign * col_part,
                                row_part, col_part, jnp.zeros((n, HEAD_SLOT - MLA_QK), F32)], axis=-1)

    if not with_rope:
        one, zero = jnp.ones((l, n_freq), F32), jnp.zeros((l, n_freq), F32)
        return jnp.concatenate([slot(1.0, one, one, 1.0), slot(0.0, zero, zero, -1.0)], axis=-1)
    inv = ROPE_BASE ** (-jnp.arange(0, per_axis, 2, dtype=F32) / per_axis)
    r_ang = jnp.arange(l // GRID_W).astype(F32)[:, None] * inv
    c_ang = jnp.arange(GRID_W).astype(F32)[:, None] * inv
    zr, zc = jnp.zeros_like(r_ang), jnp.zeros_like(c_ang)
    by_row = jnp.concatenate([slot(0.0, jnp.cos(r_ang), zr, 1.0), slot(0.0, jnp.sin(r_ang), zr, -1.0)], axis=-1)
    by_col = jnp.concatenate([slot(1.0, zc, jnp.cos(c_ang), 1.0), slot(0.0, zc, jnp.sin(c_ang), -1.0)], axis=-1)
    return (by_row[:, None, :] + by_col[None, :, :]).reshape(l, 2 * HEAD_SLOT)


def _swap_halves(w):
    half = w.shape[-1] // 2
    return jnp.concatenate([w[..., half:], w[..., :half]], axis=-1)


def _layer_weights(w_in, w_uq, w_ukv, w_out, rpb):
    d = w_in.shape[0]
    k_r_end = 256 + 128 + MLA_ROPE
    win = jnp.concatenate([w_in[:, :k_r_end], jnp.zeros((d, HEAD_SLOT - MLA_ROPE), F32),
                           w_in[:, k_r_end:]], axis=1).astype(BF16)

    rq = w_uq.shape[0]
    w3 = w_uq.reshape(rq, N_HEADS, MLA_QK)
    zpad = jnp.zeros((rq, N_HEADS, HEAD_SLOT - MLA_QK), F32)
    plain = jnp.concatenate([w3, zpad], axis=-1)
    swapped = jnp.concatenate([jnp.zeros((rq, N_HEADS, MLA_NOPE), F32),
                               _swap_halves(w3[..., MLA_NOPE:]), zpad], axis=-1)
    wq = jnp.concatenate([plain.reshape(rq, MLA_W), swapped.reshape(rq, MLA_W)], axis=1).astype(BF16)

    rkv = w_ukv.shape[0]
    u3 = w_ukv.reshape(rkv, N_HEADS, MLA_NOPE + MLA_V)
    eye = jnp.eye(MLA_ROPE, dtype=F32)
    slot_pad_r = jnp.zeros((MLA_ROPE, N_HEADS, HEAD_SLOT - MLA_QK), F32)

    def slots(top, rope_block):
        top = jnp.concatenate([top, jnp.zeros((rkv, N_HEADS, HEAD_SLOT - top.shape[-1]), F32)], axis=-1)
        mid = jnp.concatenate([jnp.zeros((MLA_ROPE, N_HEADS, MLA_NOPE), F32),
                               jnp.broadcast_to(rope_block[:, None, :], (MLA_ROPE, N_HEADS, MLA_ROPE)),
                               slot_pad_r], axis=-1)
        bot = jnp.zeros((2 * HEAD_SLOT - rkv - MLA_ROPE, N_HEADS, HEAD_SLOT), F32)
        return jnp.concatenate([top, mid, bot], axis=0).reshape(2 * HEAD_SLOT, MLA_W)

    k_plain = slots(u3[..., :MLA_NOPE], eye)
    k_swap = slots(jnp.zeros((rkv, N_HEADS, MLA_NOPE), F32), _swap_halves(eye))
    wkv = jnp.concatenate([k_plain, k_swap], axis=1).astype(BF16)

    vt3 = jnp.transpose(u3[..., MLA_NOPE:], (1, 2, 0))
    vt3 = jnp.pad(vt3, ((0, 0), (0, VT_ROWS - MLA_V), (0, 2 * HEAD_SLOT - rkv)))
    wvt = vt3.reshape(N_HEADS * VT_ROWS, 2 * HEAD_SLOT).astype(BF16)

    wo = w_out.astype(BF16)

    qc = np.arange(GRID_W)[:, None]
    kc = np.arange(GRID_W)[None, :]
    ws = np.clip(qc - NA_WIN_W // 2, 0, GRID_W - NA_WIN_W)
    in_win = (kc >= ws) & (kc < ws + NA_WIN_W)
    edge = GRID_W - NA_WIN_W
    n = 2 * GRID_W - 1
    rp = jnp.pad(rpb.astype(F32), ((0, 0), (0, 0), (edge, edge)))
    tiled = jnp.tile(rp, (1, 1, GRID_W + 1))[:, :, :GRID_W * (n + 1)]
    toep = tiled.reshape(rp.shape[0], rp.shape[1], GRID_W, n + 1)[:, :, ::-1, :GRID_W]
    t2 = jnp.where(in_win[None, None], toep * LOG2_E, NEG_INF)
    bias = jnp.concatenate([t2[:, :-1], t2[:, 1:]], axis=-1)
    return win, wq, wkv, wvt, wo, bias


def _channel_dft():
    m = (np.arange(FN_GW)[:, None] * np.arange(FN_GW)[None, :]) % FN_GW
    ang = 2.0 * np.pi * m / FN_GW
    eye = np.eye(FN_GROUPS)
    return np.concatenate([np.kron(eye, np.cos(ang)), -np.kron(eye, np.sin(ang))],
                          axis=1).astype(np.float32)


def kernel(x, c, ctx, c_ctx, w_mod, b_mod, norm1_g, norm2_g, w_in, mla_q_norm, mla_w_uq, mla_kv_norm, mla_w_ukv, na_rpb, cv_w_dw, cv_b_dw, cv_ln_g, cv_ln_b, w_out, w_ff1, w_ff2, final_g):
    depth = w_mod.shape[0]
    b, s, d = x.shape
    n_ctx = ctx.shape[1]

    cc = jnp.concatenate([c, c_ctx[None, :], jnp.zeros((8 - b - 1, d), F32)], axis=0)
    mods = _mod_call(cc, w_mod, b_mod)

    cs_lat = _rope_tables(s, True)
    cs_ctx = _rope_tables(n_ctx, False)
    dc = jnp.asarray(_channel_dft()).astype(BF16)
    row = lambda p: p.reshape(1, -1)
    fg = row(final_g)
    w1 = w_ff1.astype(BF16)
    w2 = w_ff2.astype(BF16)

    xc = ctx
    for i in range(depth):
        last = i == depth - 1
        win, wq, wkv, wvt, wo, bias = _layer_weights(w_in[i], mla_w_uq[i], mla_w_ukv[i], w_out[i], na_rpb[i])
        mx =[m.reshape(b, 1, d) for m in jnp.split(mods[i, :b], 6, axis=-1)]
        mc = [jnp.broadcast_to(m.reshape(1, 1, d), (b, 1, d))
              for m in jnp.split(mods[i, b:b + 1], 6, axis=-1)]
        proj = functools.partial(_inproj_call, g1=row(norm1_g[i]), win=win, qn=row(mla_q_norm[i]),
                                 wq=wq, kvn=row(mla_kv_norm[i]), wkv=wkv, wvt=wvt, dc=dc)
        conv = functools.partial(_conv_call, w_dw=cv_w_dw[i], b_dw=row(cv_b_dw[i]),
                                 ln_g=row(cv_ln_g[i]), ln_b=row(cv_ln_b[i]))
        mlp = functools.partial(_outmlp_call, wo=wo, g2=row(norm2_g[i]), w1=w1, w2=w2, fg=fg, layer=i)

        q, k, vt, nq, nk, nv, z, cy = proj(x, mx[0], mx[1], cs=cs_lat)
        cq, ck, cvt, cnq, cnk, cnv, cz, ccy = proj(xc, mc[0], mc[1], cs=cs_ctx)

        o_mla = _flash_call(q, k, vt, ck, cvt)
        o_na = _natten_call(nq, nk, nv, cnk, cnv, bias)
        o_fn = _fourier_mix(z)
        o_cv = conv(cy)
        x = mlp(x, o_mla, o_na, o_fn, o_cv, gt1=mx[2], sh2=mx[3], sc2=mx[4], gt2=mx[5], last=last)

        if not last:
            co_mla = _flash_call(cq, ck, cvt)
            co_na = _na_ctx_call(cnq, cnk, cnv)
            co_fn = _fourier_mix(cz)
            co_cv = conv(ccy)
            xc = mlp(xc, co_mla, co_na, co_fn, co_cv, gt1=mc[2], sh2=mc[3], sc2=mc[4], gt2=mc[5],
                     last=False)
    return x
```

```python
import functools

import jax
import jax.numpy as jnp
import numpy as np
from jax import lax
from jax.experimental import pallas as pl
from jax.experimental.pallas import tpu as pltpu

F32 = jnp.float32
BF16 = jnp.bfloat16

EPS = 1e-6
ROPE_BASE = 10000.0
NEG_INF = -1e30

D_MODEL = 1024
GRID_W = 64
N_HEADS = 4
GROUP_W = 256
HEAD_SLOT = 128
MLA_NOPE = 64
MLA_ROPE = 32
MLA_QK = MLA_NOPE + MLA_ROPE
MLA_V = 64
MLA_W = N_HEADS * HEAD_SLOT
VT_ROWS = MLA_V
LOG2_E = 1.4426950408889634
NA_HD = 64
NA_WIN_H = 8
NA_WIN_W = 16
FN_GROUPS = 4
FN_GW = GROUP_W // FN_GROUPS
CV_K = 31
CV_HALO = 16
SUBLANES = 8
D_FF = 4 * D_MODEL
DFT_N2 = 128
V7X_VMEM_LIMIT = 56 * 1024 * 1024

MOD_TILE_N = 1536
INPROJ_TILE = 1024
TOKEN_TILE = 512
CONV_TILE = 1024
ATTN_TILE_Q = 2048
ATTN_TILE_K = 1024
ATTN_HEADS_PER_STEP = 2
NATTEN_ROWS = 16
FNET_BLOCK = 16
FF_CHUNK = 1024


def _cparams(n_grid):
    return pltpu.CompilerParams(dimension_semantics=("parallel",) * n_grid,
                                vmem_limit_bytes=V7X_VMEM_LIMIT)


def _const_spec(shape):
    nd = len(shape)
    return pl.BlockSpec(shape, lambda *_: (0,) * nd, pipeline_mode=pl.Buffered(1))


def _sigmoid(v):
    return 1.0 / (1.0 + jnp.exp(-v))


def _rms(v, g):
    return v * lax.rsqrt(jnp.mean(v * v, axis=-1, keepdims=True) + EPS) * g


def _dot(a, b):
    return jnp.dot(a, b, preferred_element_type=F32)


def _dot_nt(a, b):
    return lax.dot_general(a, b, (((1,), (1,)), ((), ())), preferred_element_type=F32)


def _mod_kernel(c_ref, w_ref, b_ref, o_ref):
    c = c_ref[...]
    s = (c * _sigmoid(c)).astype(BF16)
    o_ref[0] = _dot(s, w_ref[0].astype(BF16)) + b_ref[0]


def _mod_call(cc, w_mod, b_mod):
    depth, d, n = w_mod.shape
    tn = MOD_TILE_N
    return pl.pallas_call(
        _mod_kernel,
        grid=(depth, n // tn),
        in_specs=[pl.BlockSpec(cc.shape, lambda l, j: (0, 0)),
                  pl.BlockSpec((1, d, tn), lambda l, j: (l, 0, j)),
                  pl.BlockSpec((1, 1, tn), lambda l, j: (l, 0, j))],
        out_specs=pl.BlockSpec((1, cc.shape[0], tn), lambda l, j: (l, 0, j)),
        out_shape=jax.ShapeDtypeStruct((depth, cc.shape[0], n), F32),
        compiler_params=_cparams(2),
        name="mod",
    )(cc, w_mod, b_mod.reshape(depth, 1, n))


def _inproj_kernel(x_ref, sh_ref, sc_ref, g1_ref, win_ref, qn_ref, wq_ref, kvn_ref, wkv_ref,
                   wvt_ref, dc_ref, cs_ref,
                   q_ref, k_ref, vt_ref, nq_ref, nk_ref, nv_ref, z_ref, cy_ref):
    x = x_ref[0]
    h = _rms(x, g1_ref[...]) * (1.0 + sc_ref[0]) + sh_ref[0]
    p = _dot(h.astype(BF16), win_ref[...])

    cs = cs_ref[...]
    cos_t, sin_t = cs[:, :HEAD_SLOT], cs[:, HEAD_SLOT:]
    cos4 = jnp.concatenate([cos_t] * N_HEADS, axis=-1)
    sin4 = jnp.concatenate([sin_t] * N_HEADS, axis=-1)

    r = _rms(p[:, 0:256], qn_ref[...]).astype(BF16)
    qq = _dot(r, wq_ref[...])
    q = (qq[:, :MLA_W] * cos4 + qq[:, MLA_W:] * sin4) * (MLA_QK ** -0.5 * LOG2_E)
    q_ref[0] = q.astype(BF16)

    kvr = p[:, 256:512]
    kvn = _rms(kvr[:, :128], kvn_ref[...])
    comb = jnp.concatenate([kvn, kvr[:, 128:]], axis=-1).astype(BF16)
    kk = _dot(comb, wkv_ref[...])
    k = kk[:, :MLA_W] * cos4 + kk[:, MLA_W:] * sin4
    k_ref[0] = k.astype(BF16)
    vt_ref[0] = _dot_nt(wvt_ref[...], comb).astype(BF16)

    nq_ref[0] = (p[:, 512:768] * (NA_HD ** -0.5 * LOG2_E)).astype(BF16)
    nk_ref[0] = p[:, 768:1024].astype(BF16)
    nv_ref[0] = p[:, 1024:1280].astype(BF16)

    z_ref[0] = _dot(p[:, 1280:1536].astype(BF16), dc_ref[...]).astype(BF16)

    cy_ref[0] = p[:, 1536:1792] * _sigmoid(p[:, 1792:2048])


def _inproj_call(x, sh, sc, g1, win, qn, wq, kvn, wkv, wvt, dc, cs):
    b, l, d = x.shape
    t = min(l, INPROJ_TILE)
    tok = lambda w: pl.BlockSpec((1, t, w), lambda bi, i: (bi, i, 0))
    vec = pl.BlockSpec((1, 1, d), lambda bi, i: (bi, 0, 0))
    outs = [(MLA_W, BF16), (MLA_W, BF16), None, (GROUP_W, BF16), (GROUP_W, BF16),
            (GROUP_W, BF16), (2 * GROUP_W, BF16), (GROUP_W, F32)]
    vt_rows = N_HEADS * VT_ROWS
    out_specs = [tok(o[0]) if o else pl.BlockSpec((1, vt_rows, t), lambda bi, i: (bi, 0, i))
                 for o in outs]
    out_shape = [jax.ShapeDtypeStruct((b, l, o[0]), o[1]) if o
                 else jax.ShapeDtypeStruct((b, vt_rows, l), BF16) for o in outs]
    return pl.pallas_call(
        _inproj_kernel,
        grid=(b, l // t),
        in_specs=[tok(d), vec, vec, _const_spec(g1.shape), _const_spec(win.shape),
                  _const_spec(qn.shape), _const_spec(wq.shape), _const_spec(kvn.shape),
                  _const_spec(wkv.shape), _const_spec(wvt.shape), _const_spec(dc.shape),
                  pl.BlockSpec((t, 2 * HEAD_SLOT), lambda bi, i: (i, 0))],
        out_specs=out_specs,
        out_shape=out_shape,
        compiler_params=_cparams(2),
        name="inproj",
    )(x, sh, sc, g1, win, qn, wq, kvn, wkv, wvt, dc, cs)


def _flash_kernel(*refs, n_chunks, tk, has_extra):
    if has_extra:
        q_ref, k_ref, vt_ref, k2_ref, vt2_ref, o_ref, sa_ref, sb_ref, sx_ref = refs
    else:
        q_ref, k_ref, vt_ref, o_ref, sa_ref, sb_ref = refs
    tq = q_ref.shape[1]

    def one_head(hh):
        lanes = slice(hh * HEAD_SLOT, (hh + 1) * HEAD_SLOT)
        vrows = slice(hh * VT_ROWS, (hh + 1) * VT_ROWS)
        q = q_ref[0, :, lanes]

        def scores(kc, s_ref):
            st = _dot_nt(kc, q)
            s_ref[...] = st
            return jnp.max(st, axis=0, keepdims=True)

        def accumulate(s_ref, mx, vtc, carry):
            m, (acc, l) = carry
            m_new = jnp.maximum(m, mx)
            p = jnp.exp2(s_ref[...] - m_new)
            alpha = jnp.exp2(m - m_new)
            l = alpha * l + jnp.sum(p, axis=0, keepdims=True)
            return m_new, (alpha * acc + _dot(vtc, p.astype(BF16)), l)

        def k_chunk(c):
            return k_ref[0, pl.ds(pl.multiple_of(c * tk, tk), tk), lanes]

        def vt_chunk(c):
            return vt_ref[0, vrows, pl.ds(pl.multiple_of(c * tk, tk), tk)]

        def pair(j, carry):
            mx_a, m, acc = carry
            mx_b = scores(k_chunk(2 * j + 1), sb_ref)
            m, acc = accumulate(sa_ref, mx_a, vt_chunk(2 * j), (m, acc))
            mx_a = scores(k_chunk(2 * j + 2), sa_ref)
            m, acc = accumulate(sb_ref, mx_b, vt_chunk(2 * j + 1), (m, acc))
            return mx_a, m, acc

        n_pairs = (n_chunks - 1) // 2
        carry = (scores(k_chunk(0), sa_ref), jnp.full((1, tq), -jnp.inf, F32),
                 (jnp.zeros((VT_ROWS, tq), F32), jnp.zeros((1, tq), F32)))
        mx, m, acc = lax.fori_loop(0, n_pairs, pair, carry)
        pending = (sa_ref, mx, vt_chunk(2 * n_pairs))
        tail = [(k_chunk(c), vt_chunk(c), sb_ref if c % 2 else sa_ref)
                for c in range(2 * n_pairs + 1, n_chunks)]
        if has_extra:
            tail.append((k2_ref[0, :, lanes], vt2_ref[0, vrows, :], sx_ref))
        for kc, vtc, s_ref in tail:
            mx_next = scores(kc, s_ref)
            m, acc = accumulate(*pending, (m, acc))
            pending = (s_ref, mx_next, vtc)
        m, (acc, l) = accumulate(*pending, (m, acc))
        return acc / l

    o_t = jnp.concatenate([one_head(hh) for hh in range(ATTN_HEADS_PER_STEP)], axis=0)
    o_ref[0] = o_t.T.astype(BF16)


def _flash_call(q, k, vt, k2=None, vt2=None):
    b, lq, _ = q.shape
    lk = k.shape[1]
    tq = min(lq, ATTN_TILE_Q)
    tk = min(lk, ATTN_TILE_K)
    has_extra = k2 is not None
    hps = ATTN_HEADS_PER_STEP
    qspec = pl.BlockSpec((1, tq, hps * HEAD_SLOT), lambda bi, h, i: (bi, i, h))
    kspec = lambda n: pl.BlockSpec((1, n, hps * HEAD_SLOT), lambda bi, h, i: (bi, 0, h))
    vspec = lambda n: pl.BlockSpec((1, hps * VT_ROWS, n), lambda bi, h, i: (bi, h, 0))
    ospec = pl.BlockSpec((1, tq, hps * MLA_V), lambda bi, h, i: (bi, i, h))
    in_specs = [qspec, kspec(lk), vspec(lk)]
    args = [q, k, vt]
    scratch = [pltpu.VMEM((tk, tq), F32), pltpu.VMEM((tk, tq), F32)]
    if has_extra:
        in_specs += [kspec(k2.shape[1]), vspec(k2.shape[1])]
        args += [k2, vt2]
        scratch.append(pltpu.VMEM((k2.shape[1], tq), F32))
    return pl.pallas_call(
        functools.partial(_flash_kernel, n_chunks=lk // tk, tk=tk, has_extra=has_extra),
        grid=(b, N_HEADS // hps, lq // tq),
        in_specs=in_specs,
        out_specs=ospec,
        out_shape=jax.ShapeDtypeStruct((b, lq, N_HEADS * MLA_V), BF16),
        scratch_shapes=scratch,
        compiler_params=_cparams(3),
        name="mla_attn",
    )(*args)


def _head_stack(q):
    lane = lax.broadcasted_iota(jnp.int32, q.shape, 1)
    return jnp.concatenate(
        [jnp.where(lane // NA_HD == h, q, jnp.zeros_like(q)) for h in range(N_HEADS)], axis=0)


def _head_unstack(o, n):
    lane = lax.broadcasted_iota(jnp.int32, (n, GROUP_W), 1)
    out = jnp.zeros((n, GROUP_W), F32)
    for h in range(N_HEADS):
        out = out + jnp.where(lane // NA_HD == h, o[h * n:(h + 1) * n], 0.0)
    return out


def _natten_kernel(q_ref, k_ref, v_ref, kc_ref, vc_ref, bias_ref, o_ref, *, rows_per_step, rows):
    blk = pl.program_id(1)
    kc = kc_ref[0]
    vc = vc_ref[0]
    n_loc = NA_WIN_H * GRID_W

    def body(j, _):
        r = blk * rows_per_step + j
        rs = jnp.clip(r - NA_WIN_H // 2, 0, rows - NA_WIN_H)
        d0 = rs - r + (NA_WIN_H - 1)
        qs = _head_stack(q_ref[0, pl.ds(pl.multiple_of(j * GRID_W, GRID_W), GRID_W), :])
        kstart = pl.multiple_of(rs * GRID_W, GRID_W)
        s_loc = _dot_nt(qs, k_ref[0, pl.ds(kstart, n_loc), :])
        bias = jnp.concatenate(
            [jnp.concatenate([bias_ref[h, d0 + 2 * w] for w in range(NA_WIN_H // 2)], axis=-1)
             for h in range(N_HEADS)], axis=0)
        s_loc = s_loc + bias
        s_ctx = _dot_nt(qs, kc)
        m = jnp.maximum(jnp.max(s_loc, axis=-1, keepdims=True),
                        jnp.max(s_ctx, axis=-1, keepdims=True))
        p_loc = jnp.exp2(s_loc - m)
        p_ctx = jnp.exp2(s_ctx - m)
        denom = jnp.sum(p_loc, axis=-1, keepdims=True) + jnp.sum(p_ctx, axis=-1, keepdims=True)
        o = (_dot(p_loc.astype(BF16), v_ref[0, pl.ds(kstart, n_loc), :])
             + _dot(p_ctx.astype(BF16), vc))
        o = _head_unstack(o / denom, GRID_W)
        o_ref[0, pl.ds(pl.multiple_of(j * GRID_W, GRID_W), GRID_W), :] = o.astype(BF16)
        return 0

    lax.fori_loop(0, rows_per_step, body, 0, unroll=True)


def _natten_call(q, k, v, kc, vc, bias):
    b, l, w = q.shape
    rows = l // GRID_W
    rows_per_step = NATTEN_ROWS
    t = rows_per_step * GRID_W
    full = lambda n: pl.BlockSpec((1, n, w), lambda bi, i: (bi, 0, 0))
    return pl.pallas_call(
        functools.partial(_natten_kernel, rows_per_step=rows_per_step, rows=rows),
        grid=(b, rows // rows_per_step),
        in_specs=[pl.BlockSpec((1, t, w), lambda bi, i: (bi, i, 0)),
                  full(l), full(l), full(kc.shape[1]), full(kc.shape[1]),
                  _const_spec(bias.shape)],
        out_specs=pl.BlockSpec((1, t, w), lambda bi, i: (bi, i, 0)),
        out_shape=jax.ShapeDtypeStruct((b, l, w), BF16),
        compiler_params=_cparams(2),
        name="natten",
    )(q, k, v, kc, vc, bias)


def _na_ctx_kernel(q_ref, k_ref, v_ref, o_ref):
    n = q_ref.shape[1]
    s = _dot_nt(_head_stack(q_ref[0]), k_ref[0])
    m = jnp.max(s, axis=-1, keepdims=True)
    p = jnp.exp2(s - m)
    denom = jnp.sum(p, axis=-1, keepdims=True)
    o = _dot(p.astype(BF16), v_ref[0]) / denom
    o_ref[0] = _head_unstack(o, n).astype(BF16)


def _na_ctx_call(q, k, v):
    b, n, w = q.shape
    spec = pl.BlockSpec((1, n, w), lambda bi: (bi, 0, 0))
    return pl.pallas_call(
        _na_ctx_kernel, grid=(b,), in_specs=[spec, spec, spec], out_specs=spec,
        out_shape=jax.ShapeDtypeStruct((b, n, w), BF16),
        compiler_params=_cparams(1), name="na_ctx",
    )(q, k, v)


def _fnet1_kernel(z_ref, w_ref, y_ref, *, n_inner):
    zb = z_ref[0]
    p = _dot(w_ref[...], zb.reshape(zb.shape[0], zb.shape[1] * zb.shape[2]))
    n1 = p.shape[0] // 2
    re, im = [], []
    for j in range(n_inner):
        zr_c = p[:n1, j * 512:j * 512 + 256]
        zi_c = p[:n1, j * 512 + 256:(j + 1) * 512]
        zr_s = p[n1:, j * 512:j * 512 + 256]
        zi_s = p[n1:, j * 512 + 256:(j + 1) * 512]
        re.append(zr_c + zi_s)
        im.append(zi_c - zr_s)
    y_ref[0, 0] = jnp.concatenate(re, axis=-1).astype(BF16).reshape(n1, n_inner, GROUP_W)
    y_ref[0, 1] = jnp.concatenate(im, axis=-1).astype(BF16).reshape(n1, n_inner, GROUP_W)


def _fnet1_call(z2, w1s, n1):
    b = z2.shape[0]
    n_inner = FNET_BLOCK
    return pl.pallas_call(
        functools.partial(_fnet1_kernel, n_inner=n_inner),
        grid=(b, DFT_N2 // n_inner),
        in_specs=[pl.BlockSpec((1, n1, n_inner, 512), lambda bi, i: (bi, 0, i, 0)),
                  _const_spec(w1s.shape)],
        out_specs=pl.BlockSpec((1, 2, n1, n_inner, GROUP_W), lambda bi, i: (bi, 0, 0, i, 0)),
        out_shape=jax.ShapeDtypeStruct((b, 2, n1, DFT_N2, GROUP_W), BF16),
        compiler_params=_cparams(2),
        name="fnet_stage1",
    )(z2, w1s)


def _fnet2_kernel(y_ref, f_ref, o_ref, *, n_inner, norm):
    outs = []
    for j in range(n_inner):
        f = f_ref[j]
        o = _dot(f[:, :DFT_N2], y_ref[0, 0, j]) + _dot(f[:, DFT_N2:], y_ref[0, 1, j])
        outs.append((o * norm).astype(BF16))
    o_ref[0] = jnp.concatenate(outs, axis=-1).reshape(DFT_N2, n_inner, GROUP_W)


def _fnet2_call(y5, ftab, n1, norm):
    b = y5.shape[0]
    n_inner = FNET_BLOCK
    return pl.pallas_call(
        functools.partial(_fnet2_kernel, n_inner=n_inner, norm=norm),
        grid=(b, n1 // n_inner),
        in_specs=[pl.BlockSpec((1, 2, n_inner, DFT_N2, GROUP_W), lambda bi, i: (bi, 0, i, 0, 0)),
                  pl.BlockSpec((n_inner, DFT_N2, 2 * DFT_N2), lambda bi, i: (i, 0, 0))],
        out_specs=pl.BlockSpec((1, DFT_N2, n_inner, GROUP_W), lambda bi, i: (bi, 0, i, 0)),
        out_shape=jax.ShapeDtypeStruct((b, DFT_N2, n1, GROUP_W), BF16),
        compiler_params=_cparams(2),
        name="fnet_stage2",
    )(y5, ftab)


def _dft_small_kernel(z_ref, f_ref, o_ref, *, norm):
    z = z_ref[0]
    f = f_ref[...]
    n = z.shape[0]
    o = _dot(f[:, :n], z[:, :GROUP_W]) + _dot(f[:, n:], z[:, GROUP_W:])
    o_ref[0] = (o * norm).astype(BF16)


def _dft_small_call(z, ftab, norm):
    b, n, _ = z.shape
    return pl.pallas_call(
        functools.partial(_dft_small_kernel, norm=norm),
        grid=(b,),
        in_specs=[pl.BlockSpec((1, n, 2 * GROUP_W), lambda bi: (bi, 0, 0)), _const_spec(ftab.shape)],
        out_specs=pl.BlockSpec((1, n, GROUP_W), lambda bi: (bi, 0, 0)),
        out_shape=jax.ShapeDtypeStruct((b, n, GROUP_W), BF16),
        compiler_params=_cparams(1), name="fnet_ctx",
    )(z, ftab)


def _fourier_tables(l):
    if l <= 256:
        m = (np.arange(l)[:, None] * np.arange(l)[None, :]) % l
        ang = 2.0 * np.pi * m / l
        return None, np.concatenate([np.cos(ang), np.sin(ang)], axis=1).astype(np.float32), 0
    n1 = l // DFT_N2
    m1 = (np.arange(n1)[:, None] * np.arange(n1)[None, :]) % n1
    a1 = 2.0 * np.pi * m1 / n1
    w1s = np.concatenate([np.cos(a1), np.sin(a1)], axis=0).astype(np.float32)
    kk = np.arange(n1)[:, None, None] + n1 * np.arange(DFT_N2)[None, :, None]
    m2 = (kk * np.arange(DFT_N2)[None, None, :]) % l
    a2 = 2.0 * np.pi * m2 / l
    ftab = np.concatenate([np.cos(a2), np.sin(a2)], axis=2).astype(np.float32)
    return w1s, ftab, n1


def _fourier_mix(z):
    b, l, _ = z.shape
    norm = float((l * FN_GW) ** -0.5)
    w1s, ftab, n1 = _fourier_tables(l)
    if w1s is None:
        return _dft_small_call(z, jnp.asarray(ftab).astype(BF16), norm)
    y = _fnet1_call(z.reshape(b, n1, DFT_N2, 2 * GROUP_W), jnp.asarray(w1s).astype(BF16), n1)
    o = _fnet2_call(y, jnp.asarray(ftab).astype(BF16), n1, norm)
    return o.reshape(b, l, GROUP_W)


def _conv_kernel(prev_ref, cur_ref, next_ref, w_ref, b_ref, g_ref, beta_ref, o_ref, buf_ref):
    i = pl.program_id(1)
    n = pl.num_programs(1)
    t = cur_ref.shape[1]
    buf_ref[0:CV_HALO, :] = jnp.where(i > 0, prev_ref[0], 0.0)
    buf_ref[CV_HALO:CV_HALO + t, :] = cur_ref[0]
    buf_ref[CV_HALO + t:, :] = jnp.where(i < n - 1, next_ref[0], 0.0)
    w = w_ref[...]
    first = CV_HALO - CV_K // 2
    acc = None
    for res in range(SUBLANES):
        z = None
        for base in range(0, first + CV_K, SUBLANES):
            j = base + res - first
            if 0 <= j < CV_K:
                term = w[j:j + 1, :] * buf_ref[base:base + t + SUBLANES, :]
                z = term if z is None else z + term
        z = z[res:res + t, :]
        acc = z if acc is None else acc + z
    y = acc + b_ref[...]
    mu = jnp.mean(y, axis=-1, keepdims=True)
    var = jnp.mean(jnp.square(y - mu), axis=-1, keepdims=True)
    y = (y - mu) * lax.rsqrt(var + EPS) * g_ref[...] + beta_ref[...]
    o_ref[0] = (y * _sigmoid(y)).astype(BF16)


def _conv_call(cy, w_dw, b_dw, ln_g, ln_b):
    b, l, w = cy.shape
    t = min(l, CONV_TILE)
    hb = t // CV_HALO
    n_halo = l // CV_HALO
    return pl.pallas_call(
        _conv_kernel,
        grid=(b, l // t),
        in_specs=[pl.BlockSpec((1, CV_HALO, w), lambda bi, i: (bi, jnp.maximum(i * hb - 1, 0), 0)),
                  pl.BlockSpec((1, t, w), lambda bi, i: (bi, i, 0)),
                  pl.BlockSpec((1, CV_HALO, w),
                               lambda bi, i: (bi, jnp.minimum((i + 1) * hb, n_halo - 1), 0)),
                  _const_spec(w_dw.shape), _const_spec(b_dw.shape), _const_spec(ln_g.shape),
                  _const_spec(ln_b.shape)],
        out_specs=pl.BlockSpec((1, t, w), lambda bi, i: (bi, i, 0)),
        out_shape=jax.ShapeDtypeStruct((b, l, w), BF16),
        scratch_shapes=[pltpu.VMEM((t + 2 * CV_HALO, w), F32)],
        compiler_params=_cparams(2),
        name="conformer_conv",
    )(cy, cy, cy, w_dw, b_dw, ln_g, ln_b)


def _outmlp_kernel(x_ref, oa_ref, oc_ref, ob_ref, od_ref, wo_ref, gt1_ref, sh2_ref, sc2_ref,
                   gt2_ref, g2_ref, w1_ref, w2_ref, fg_ref, o_ref, *, last, ff_chunk):
    y = (_dot(oa_ref[0], wo_ref[0:GROUP_W, :])
         + _dot(oc_ref[0], wo_ref[GROUP_W:2 * GROUP_W, :])
         + _dot(ob_ref[0], wo_ref[2 * GROUP_W:3 * GROUP_W, :])
         + _dot(od_ref[0], wo_ref[3 * GROUP_W:, :]))
    x1 = x_ref[0] + gt1_ref[0] * y
    h = (_rms(x1, g2_ref[...]) * (1.0 + sc2_ref[0]) + sh2_ref[0]).astype(BF16)
    ff = jnp.zeros(x1.shape, F32)
    for c in range(D_FF // ff_chunk):
        u = jnp.maximum(_dot(h, w1_ref[0, :, c * ff_chunk:(c + 1) * ff_chunk]), 0.0)
        ff = ff + _dot((u * u).astype(BF16), w2_ref[0, c * ff_chunk:(c + 1) * ff_chunk, :])
    x2 = x1 + gt2_ref[0] * ff
    if last:
        x2 = _rms(x2, fg_ref[...])
    o_ref[0] = x2


def _outmlp_call(x, o_mla, o_na, o_fn, o_cv, wo, gt1, sh2, sc2, gt2, g2, w1, w2, fg, last, layer):
    b, l, d = x.shape
    t = min(l, TOKEN_TILE)
    tok = lambda w: pl.BlockSpec((1, t, w), lambda bi, i: (bi, i, 0))
    vec = pl.BlockSpec((1, 1, d), lambda bi, i: (bi, 0, 0))
    slab = lambda w: pl.BlockSpec((1,) + w.shape[1:], lambda bi, i: (layer, 0, 0),
                                  pipeline_mode=pl.Buffered(1))
    return pl.pallas_call(
        functools.partial(_outmlp_kernel, last=last, ff_chunk=FF_CHUNK),
        grid=(b, l // t),
        in_specs=[tok(d), tok(GROUP_W), tok(GROUP_W), tok(GROUP_W), tok(GROUP_W),
                  _const_spec(wo.shape), vec, vec, vec, vec, _const_spec(g2.shape),
                  slab(w1), slab(w2), _const_spec(fg.shape)],
        out_specs=tok(d),
        out_shape=jax.ShapeDtypeStruct((b, l, d), F32),
        compiler_params=_cparams(2),
        name="outproj_mlp",
    )(x, o_mla, o_na, o_fn, o_cv, wo, gt1, sh2, sc2, gt2, g2, w1, w2, fg)


def _rope_tables(l, with_rope):
    per_axis = MLA_ROPE // 2
    n_freq = per_axis // 2

    def slot(nope, row_part, col_part, sign):
        n = row_part.shape[0]
        return jnp.concatenate([jnp.full((n, MLA_NOPE), nope, F32), sign * row_part, sign * col_part,
                                row_part, col_part, jnp.zeros((n, HEAD_SLOT - MLA_QK), F32)], axis=-1)

    if not with_rope:
        one, zero = jnp.ones((l, n_freq), F32), jnp.zeros((l, n_freq), F32)
        return jnp.concatenate([slot(1.0, one, one, 1.0), slot(0.0, zero, zero, -1.0)], axis=-1)
    inv = ROPE_BASE ** (-jnp.arange(0, per_axis, 2, dtype=F32) / per_axis)
    r_ang = jnp.arange(l // GRID_W).astype(F32)[:, None] * inv
    c_ang = jnp.arange(GRID_W).astype(F32)[:, None] * inv
    zr, zc = jnp.zeros_like(r_ang), jnp.zeros_like(c_ang)
    by_row = jnp.concatenate([slot(0.0, jnp.cos(r_ang), zr, 1.0), slot(0.0, jnp.sin(r_ang), zr, -1.0)], axis=-1)
    by_col = jnp.concatenate([slot(1.0, zc, jnp.cos(c_ang), 1.0), slot(0.0, zc, jnp.sin(c_ang), -1.0)], axis=-1)
    return (by_row[:, None, :] + by_col[None, :, :]).reshape(l, 2 * HEAD_SLOT)


def _swap_halves(w):
    half = w.shape[-1] // 2
    return jnp.concatenate([w[..., half:], w[..., :half]], axis=-1)


def _layer_weights(w_in, w_uq, w_ukv, w_out, rpb):
    d = w_in.shape[0]
    k_r_end = 256 + 128 + MLA_ROPE
    win = jnp.concatenate([w_in[:, :k_r_end], jnp.zeros((d, HEAD_SLOT - MLA_ROPE), F32),
                           w_in[:, k_r_end:]], axis=1).astype(BF16)

    rq = w_uq.shape[0]
    w3 = w_uq.reshape(rq, N_HEADS, MLA_QK)
    zpad = jnp.zeros((rq, N_HEADS, HEAD_SLOT - MLA_QK), F32)
    plain = jnp.concatenate([w3, zpad], axis=-1)
    swapped = jnp.concatenate([jnp.zeros((rq, N_HEADS, MLA_NOPE), F32),
                               _swap_halves(w3[..., MLA_NOPE:]), zpad], axis=-1)
    wq = jnp.concatenate([plain.reshape(rq, MLA_W), swapped.reshape(rq, MLA_W)], axis=1).astype(BF16)

    rkv = w_ukv.shape[0]
    u3 = w_ukv.reshape(rkv, N_HEADS, MLA_NOPE + MLA_V)
    eye = jnp.eye(MLA_ROPE, dtype=F32)
    slot_pad_r = jnp.zeros((MLA_ROPE, N_HEADS, HEAD_SLOT - MLA_QK), F32)

    def slots(top, rope_block):
        top = jnp.concatenate([top, jnp.zeros((rkv, N_HEADS, HEAD_SLOT - top.shape[-1]), F32)], axis=-1)
        mid = jnp.concatenate([jnp.zeros((MLA_ROPE, N_HEADS, MLA_NOPE), F32),
                               jnp.broadcast_to(rope_block[:, None, :], (MLA_ROPE, N_HEADS, MLA_ROPE)),
                               slot_pad_r], axis=-1)
        bot = jnp.zeros((2 * HEAD_SLOT - rkv - MLA_ROPE, N_HEADS, HEAD_SLOT), F32)
        return jnp.concatenate([top, mid, bot], axis=0).reshape(2 * HEAD_SLOT, MLA_W)

    k_plain = slots(u3[..., :MLA_NOPE], eye)
    k_swap = slots(jnp.zeros((rkv, N_HEADS, MLA_NOPE), F32), _swap_halves(eye))
    wkv = jnp.concatenate([k_plain, k_swap], axis=1).astype(BF16)

    vt3 = jnp.transpose(u3[..., MLA_NOPE:], (1, 2, 0))
    vt3 = jnp.pad(vt3, ((0, 0), (0, VT_ROWS - MLA_V), (0, 2 * HEAD_SLOT - rkv)))
    wvt = vt3.reshape(N_HEADS * VT_ROWS, 2 * HEAD_SLOT).astype(BF16)

    wo = w_out.astype(BF16)

    qc = np.arange(GRID_W)[:, None]
    kc = np.arange(GRID_W)[None, :]
    ws = np.clip(qc - NA_WIN_W // 2, 0, GRID_W - NA_WIN_W)
    in_win = (kc >= ws) & (kc < ws + NA_WIN_W)
    edge = GRID_W - NA_WIN_W
    n = 2 * GRID_W - 1
    rp = jnp.pad(rpb.astype(F32), ((0, 0), (0, 0), (edge, edge)))
    tiled = jnp.tile(rp, (1, 1, GRID_W + 1))[:, :, :GRID_W * (n + 1)]
    toep = tiled.reshape(rp.shape[0], rp.shape[1], GRID_W, n + 1)[:, :, ::-1, :GRID_W]
    t2 = jnp.where(in_win[None, None], toep * LOG2_E, NEG_INF)
    bias = jnp.concatenate([t2[:, :-1], t2[:, 1:]], axis=-1)
    return win, wq, wkv, wvt, wo, bias


def _channel_dft():
    m = (np.arange(FN_GW)[:, None] * np.arange(FN_GW)[None, :]) % FN_GW
    ang = 2.0 * np.pi * m / FN_GW
    eye = np.eye(FN_GROUPS)
    return np.concatenate([np.kron(eye, np.cos(ang)), -np.kron(eye, np.sin(ang))],
                          axis=1).astype(np.float32)


def kernel(x, c, ctx, c_ctx, w_mod, b_mod, norm1_g, norm2_g, w_in, mla_q_norm, mla_w_uq, mla_kv_norm, mla_w_ukv, na_rpb, cv_w_dw, cv_b_dw, cv_ln_g, cv_ln_b, w_out, w_ff1, w_ff2, final_g):
    depth = w_mod.shape[0]
    b, s, d = x.shape
    n_ctx = ctx.shape[1]

    cc = jnp.concatenate([c, c_ctx[None, :], jnp.zeros((8 - b - 1, d), F32)], axis=0)
    mods = _mod_call(cc, w_mod, b_mod)

    cs_lat = _rope_tables(s, True)
    cs_ctx = _rope_tables(n_ctx, False)
    dc = jnp.asarray(_channel_dft()).astype(BF16)
    row = lambda p: p.reshape(1, -1)
    fg = row(final_g)
    w1 = w_ff1.astype(BF16)
    w2 = w_ff2.astype(BF16)

    xc = ctx
    for i in range(depth):
        last = i == depth - 1
        win, wq, wkv, wvt, wo, bias = _layer_weights(w_in[i], mla_w_uq[i], mla_w_ukv[i], w_out[i], na_rpb[i])
        mx =[m.reshape(b, 1, d) for m in jnp.split(mods[i, :b], 6, axis=-1)]
        mc = [jnp.broadcast_to(m.reshape(1, 1, d), (b, 1, d))
              for m in jnp.split(mods[i, b:b + 1], 6, axis=-1)]
        proj = functools.partial(_inproj_call, g1=row(norm1_g[i]), win=win, qn=row(mla_q_norm[i]),
                                 wq=wq, kvn=row(mla_kv_norm[i]), wkv=wkv, wvt=wvt, dc=dc)
        conv = functools.partial(_conv_call, w_dw=cv_w_dw[i], b_dw=row(cv_b_dw[i]),
                                 ln_g=row(cv_ln_g[i]), ln_b=row(cv_ln_b[i]))
        mlp = functools.partial(_outmlp_call, wo=wo, g2=row(norm2_g[i]), w1=w1, w2=w2, fg=fg, layer=i)

        q, k, vt, nq, nk, nv, z, cy = proj(x, mx[0], mx[1], cs=cs_lat)
        cq, ck, cvt, cnq, cnk, cnv, cz, ccy = proj(xc, mc[0], mc[1], cs=cs_ctx)

        o_mla = _flash_call(q, k, vt, ck, cvt)
        o_na = _natten_call(nq, nk, nv, cnk, cnv, bias)
        o_fn = _fourier_mix(z)
        o_cv = conv(cy)
        x = mlp(x, o_mla, o_na, o_fn, o_cv, gt1=mx[2], sh2=mx[3], sc2=mx[4], gt2=mx[5], last=last)

        if not last:
            co_mla = _flash_call(cq, ck, cvt)
            co_na = _na_ctx_call(cnq, cnk, cnv)
            co_fn = _fourier_mix(cz)
            co_cv = conv(ccy)
            xc = mlp(xc, co_mla, co_na, co_fn, co_cv, gt1=mc[2], sh2=mc[3], sc2=mc[4], gt2=mc[5],
                     last=False)
    return x
```

```python
import functools

import jax
import jax.numpy as jnp
import numpy as np
from jax import lax
from jax.experimental import pallas as pl
from jax.experimental.pallas import tpu as pltpu

F32 = jnp.float32
BF16 = jnp.bfloat16

EPS = 1e-6
ROPE_BASE = 10000.0
NEG_INF = -1e30

D_MODEL = 1024
GRID_W = 64
N_HEADS = 4
GROUP_W = 256
HEAD_SLOT = 128
MLA_NOPE = 64
MLA_ROPE = 32
MLA_QK = MLA_NOPE + MLA_ROPE
MLA_V = 64
MLA_W = N_HEADS * HEAD_SLOT
VT_ROWS = MLA_V
LOG2_E = 1.4426950408889634
NA_HD = 64
NA_WIN_H = 8
NA_WIN_W = 16
FN_GROUPS = 4
FN_GW = GROUP_W // FN_GROUPS
CV_K = 31
CV_HALO = 16
SUBLANES = 8
D_FF = 4 * D_MODEL
DFT_N2 = 128
V7X_VMEM_LIMIT = 56 * 1024 * 1024

MOD_TILE_N = 1536
INPROJ_TILE = 1024
TOKEN_TILE = 512
CONV_TILE = 1024
ATTN_TILE_Q = 2048
ATTN_TILE_K = 1024
ATTN_HEADS_PER_STEP = 2
NATTEN_ROWS = 16
FNET_BLOCK = 16
FF_CHUNK = 1024


def _cparams(n_grid):
    return pltpu.CompilerParams(dimension_semantics=("parallel",) * n_grid,
                                vmem_limit_bytes=V7X_VMEM_LIMIT)


def _const_spec(shape):
    nd = len(shape)
    return pl.BlockSpec(shape, lambda *_: (0,) * nd, pipeline_mode=pl.Buffered(1))


def _sigmoid(v):
    return 1.0 / (1.0 + jnp.exp(-v))


def _rms(v, g):
    return v * lax.rsqrt(jnp.mean(v * v, axis=-1, keepdims=True) + EPS) * g


def _dot(a, b):
    return jnp.dot(a, b, preferred_element_type=F32)


def _dot_nt(a, b):
    return lax.dot_general(a, b, (((1,), (1,)), ((), ())), preferred_element_type=F32)


def _mod_kernel(c_ref, w_ref, b_ref, o_ref):
    c = c_ref[...]
    s = (c * _sigmoid(c)).astype(BF16)
    o_ref[0] = _dot(s, w_ref[0].astype(BF16)) + b_ref[0]


def _mod_call(cc, w_mod, b_mod):
    depth, d, n = w_mod.shape
    tn = MOD_TILE_N
    return pl.pallas_call(
        _mod_kernel,
        grid=(depth, n // tn),
        in_specs=[pl.BlockSpec(cc.shape, lambda l, j: (0, 0)),
                  pl.BlockSpec((1, d, tn), lambda l, j: (l, 0, j)),
                  pl.BlockSpec((1, 1, tn), lambda l, j: (l, 0, j))],
        out_specs=pl.BlockSpec((1, cc.shape[0], tn), lambda l, j: (l, 0, j)),
        out_shape=jax.ShapeDtypeStruct((depth, cc.shape[0], n), F32),
        compiler_params=_cparams(2),
        name="mod",
    )(cc, w_mod, b_mod.reshape(depth, 1, n))


def _inproj_kernel(x_ref, sh_ref, sc_ref, g1_ref, win_ref, qn_ref, wq_ref, kvn_ref, wkv_ref,
                   wvt_ref, dc_ref, cs_ref,
                   q_ref, k_ref, vt_ref, nq_ref, nk_ref, nv_ref, z_ref, cy_ref):
    x = x_ref[0]
    h = _rms(x, g1_ref[...]) * (1.0 + sc_ref[0]) + sh_ref[0]
    p = _dot(h.astype(BF16), win_ref[0])

    cs = cs_ref[...]
    cos_t, sin_t = cs[:, :HEAD_SLOT], cs[:, HEAD_SLOT:]
    cos4 = jnp.concatenate([cos_t] * N_HEADS, axis=-1)
    sin4 = jnp.concatenate([sin_t] * N_HEADS, axis=-1)

    r = _rms(p[:, 0:256], qn_ref[...]).astype(BF16)
    qq = _dot(r, wq_ref[...])
    q = (qq[:, :MLA_W] * cos4 + qq[:, MLA_W:] * sin4) * (MLA_QK ** -0.5 * LOG2_E)
    q_ref[0] = q.astype(BF16)

    kvr = p[:, 256:512]
    kvn = _rms(kvr[:, :128], kvn_ref[...])
    comb = jnp.concatenate([kvn, kvr[:, 128:]], axis=-1).astype(BF16)
    kk = _dot(comb, wkv_ref[...])
    k = kk[:, :MLA_W] * cos4 + kk[:, MLA_W:] * sin4
    k_ref[0] = k.astype(BF16)
    vt_ref[0] = _dot_nt(wvt_ref[...], comb).astype(BF16)

    nq_ref[0] = (p[:, 512:768] * (NA_HD ** -0.5 * LOG2_E)).astype(BF16)
    nk_ref[0] = p[:, 768:1024].astype(BF16)
    nv_ref[0] = p[:, 1024:1280].astype(BF16)

    z_ref[0] = _dot(p[:, 1280:1536].astype(BF16), dc_ref[...]).astype(BF16)

    cy_ref[0] = p[:, 1536:1792] * _sigmoid(p[:, 1792:2048])


def _inproj_call(x, sh, sc, g1, win, qn, wq, kvn, wkv, wvt, dc, cs, layer):
    b, l, d = x.shape
    t = min(l, INPROJ_TILE)
    tok = lambda w: pl.BlockSpec((1, t, w), lambda bi, i: (bi, i, 0))
    vec = pl.BlockSpec((1, 1, d), lambda bi, i: (bi, 0, 0))
    outs = [(MLA_W, BF16), (MLA_W, BF16), None, (GROUP_W, BF16), (GROUP_W, BF16),
            (GROUP_W, BF16), (2 * GROUP_W, BF16), (GROUP_W, F32)]
    vt_rows = N_HEADS * VT_ROWS
    out_specs = [tok(o[0]) if o else pl.BlockSpec((1, vt_rows, t), lambda bi, i: (bi, 0, i))
                 for o in outs]
    out_shape = [jax.ShapeDtypeStruct((b, l, o[0]), o[1]) if o
                 else jax.ShapeDtypeStruct((b, vt_rows, l), BF16) for o in outs]
    return pl.pallas_call(
        _inproj_kernel,
        grid=(b, l // t),
        in_specs=[tok(d), vec, vec, _const_spec(g1.shape),
                  pl.BlockSpec((1,) + win.shape[1:], lambda bi, i: (layer, 0, 0),
                               pipeline_mode=pl.Buffered(1)),
                  _const_spec(qn.shape), _const_spec(wq.shape), _const_spec(kvn.shape),
                  _const_spec(wkv.shape), _const_spec(wvt.shape), _const_spec(dc.shape),
                  pl.BlockSpec((t, 2 * HEAD_SLOT), lambda bi, i: (i, 0))],
        out_specs=out_specs,
        out_shape=out_shape,
        compiler_params=_cparams(2),
        name="inproj",
    )(x, sh, sc, g1, win, qn, wq, kvn, wkv, wvt, dc, cs)


def _flash_kernel(*refs, n_chunks, tk, has_extra):
    if has_extra:
        q_ref, k_ref, vt_ref, k2_ref, vt2_ref, o_ref, sa_ref, sb_ref, sx_ref = refs
    else:
        q_ref, k_ref, vt_ref, o_ref, sa_ref, sb_ref = refs
    tq = q_ref.shape[1]

    def head_lanes(hh):
        return slice(hh * HEAD_SLOT, (hh + 1) * HEAD_SLOT)

    def head_scores(hh, kc, s_ref):
        st = _dot_nt(kc, q_ref[0, :, head_lanes(hh)])
        s_ref[...] = st
        return jnp.max(st, axis=0, keepdims=True)

    def first_scores(hh):
        return head_scores(hh, k_ref[0, 0:tk, head_lanes(hh)], sa_ref)

    def one_head(hh, mx_first, start_next):
        lanes = head_lanes(hh)
        vrows = slice(hh * VT_ROWS, (hh + 1) * VT_ROWS)
        scores = functools.partial(head_scores, hh)

        def accumulate(s_ref, mx, vtc, carry):
            m, (acc, l) = carry
            m_new = jnp.maximum(m, mx)
            p = jnp.exp2(s_ref[...] - m_new)
            alpha = jnp.exp2(m - m_new)
            l = alpha * l + jnp.sum(p, axis=0, keepdims=True)
            return m_new, (alpha * acc + _dot(vtc, p.astype(BF16)), l)

        def k_chunk(c):
            return k_ref[0, pl.ds(pl.multiple_of(c * tk, tk), tk), lanes]

        def vt_chunk(c):
            return vt_ref[0, vrows, pl.ds(pl.multiple_of(c * tk, tk), tk)]

        def pair(j, carry):
            mx_a, m, acc = carry
            mx_b = scores(k_chunk(2 * j + 1), sb_ref)
            m, acc = accumulate(sa_ref, mx_a, vt_chunk(2 * j), (m, acc))
            mx_a = scores(k_chunk(2 * j + 2), sa_ref)
            m, acc = accumulate(sb_ref, mx_b, vt_chunk(2 * j + 1), (m, acc))
            return mx_a, m, acc

        n_pairs = (n_chunks - 1) // 2
        carry = (mx_first, jnp.full((1, tq), -jnp.inf, F32),
                 (jnp.zeros((VT_ROWS, tq), F32), jnp.zeros((1, tq), F32)))
        mx, m, acc = lax.fori_loop(0, n_pairs, pair, carry)
        pending = (sa_ref, mx, vt_chunk(2 * n_pairs))
        tail = [(k_chunk(c), vt_chunk(c), sb_ref if c % 2 else sa_ref)
                for c in range(2 * n_pairs + 1, n_chunks)]
        if has_extra:
            tail.append((k2_ref[0, :, lanes], vt2_ref[0, vrows, :], sx_ref))
        for kc, vtc, s_ref in tail:
            mx_next = scores(kc, s_ref)
            m, acc = accumulate(*pending, (m, acc))
            pending = (s_ref, mx_next, vtc)
        early = start_next is not None and pending[0] is not sa_ref
        mx_next_head = start_next() if early else None
        m, (acc, l) = accumulate(*pending, (m, acc))
        if start_next is not None and not early:
            mx_next_head = start_next()
        return acc / l, mx_next_head

    outs = []
    mx_first = first_scores(0)
    for hh in range(ATTN_HEADS_PER_STEP):
        nxt = functools.partial(first_scores, hh + 1) if hh + 1 < ATTN_HEADS_PER_STEP else None
        o_h, mx_first = one_head(hh, mx_first, nxt)
        outs.append(o_h)
    o_ref[0] = jnp.concatenate(outs, axis=0).T.astype(BF16)


def _flash_call(q, k, vt, k2=None, vt2=None):
    b, lq, _ = q.shape
    lk = k.shape[1]
    tq = min(lq, ATTN_TILE_Q)
    tk = min(lk, ATTN_TILE_K)
    has_extra = k2 is not None
    hps = ATTN_HEADS_PER_STEP
    qspec = pl.BlockSpec((1, tq, hps * HEAD_SLOT), lambda bi, h, i: (bi, i, h))
    kspec = lambda n: pl.BlockSpec((1, n, hps * HEAD_SLOT), lambda bi, h, i: (bi, 0, h))
    vspec = lambda n: pl.BlockSpec((1, hps * VT_ROWS, n), lambda bi, h, i: (bi, h, 0))
    ospec = pl.BlockSpec((1, tq, hps * MLA_V), lambda bi, h, i: (bi, i, h))
    in_specs = [qspec, kspec(lk), vspec(lk)]
    args = [q, k, vt]
    scratch = [pltpu.VMEM((tk, tq), F32), pltpu.VMEM((tk, tq), F32)]
    if has_extra:
        in_specs += [kspec(k2.shape[1]), vspec(k2.shape[1])]
        args += [k2, vt2]
        scratch.append(pltpu.VMEM((k2.shape[1], tq), F32))
    return pl.pallas_call(
        functools.partial(_flash_kernel, n_chunks=lk // tk, tk=tk, has_extra=has_extra),
        grid=(b, N_HEADS // hps, lq // tq),
        in_specs=in_specs,
        out_specs=ospec,
        out_shape=jax.ShapeDtypeStruct((b, lq, N_HEADS * MLA_V), BF16),
        scratch_shapes=scratch,
        compiler_params=_cparams(3),
        name="mla_attn",
    )(*args)


def _head_stack(q):
    lane = lax.broadcasted_iota(jnp.int32, q.shape, 1)
    return jnp.concatenate(
        [jnp.where(lane // NA_HD == h, q, jnp.zeros_like(q)) for h in range(N_HEADS)], axis=0)


def _head_unstack(o, n):
    lane = lax.broadcasted_iota(jnp.int32, (n, GROUP_W), 1)
    out = jnp.zeros((n, GROUP_W), F32)
    for h in range(N_HEADS):
        out = out + jnp.where(lane // NA_HD == h, o[h * n:(h + 1) * n], 0.0)
    return out


def _natten_kernel(q_ref, k_ref, v_ref, kc_ref, vc_ref, bias_ref, o_ref, *, rows_per_step, rows):
    blk = pl.program_id(1)
    kc = kc_ref[0]
    vc = vc_ref[0]
    n_loc = NA_WIN_H * GRID_W

    def body(j, _):
        r = blk * rows_per_step + j
        rs = jnp.clip(r - NA_WIN_H // 2, 0, rows - NA_WIN_H)
        d0 = rs - r + (NA_WIN_H - 1)
        qs = _head_stack(q_ref[0, pl.ds(pl.multiple_of(j * GRID_W, GRID_W), GRID_W), :])
        kstart = pl.multiple_of(rs * GRID_W, GRID_W)
        s_loc = _dot_nt(qs, k_ref[0, pl.ds(kstart, n_loc), :])
        bias = jnp.concatenate(
            [jnp.concatenate([bias_ref[h, d0 + 2 * w] for w in range(NA_WIN_H // 2)], axis=-1)
             for h in range(N_HEADS)], axis=0)
        s_loc = s_loc + bias
        s_ctx = _dot_nt(qs, kc)
        m = jnp.maximum(jnp.max(s_loc, axis=-1, keepdims=True),
                        jnp.max(s_ctx, axis=-1, keepdims=True))
        p_loc = jnp.exp2(s_loc - m)
        p_ctx = jnp.exp2(s_ctx - m)
        denom = jnp.sum(p_loc, axis=-1, keepdims=True) + jnp.sum(p_ctx, axis=-1, keepdims=True)
        o = (_dot(p_loc.astype(BF16), v_ref[0, pl.ds(kstart, n_loc), :])
             + _dot(p_ctx.astype(BF16), vc))
        o = _head_unstack(o / denom, GRID_W)
        o_ref[0, pl.ds(pl.multiple_of(j * GRID_W, GRID_W), GRID_W), :] = o.astype(BF16)
        return 0

    lax.fori_loop(0, rows_per_step, body, 0, unroll=True)


def _natten_call(q, k, v, kc, vc, bias):
    b, l, w = q.shape
    rows = l // GRID_W
    rows_per_step = NATTEN_ROWS
    t = rows_per_step * GRID_W
    full = lambda n: pl.BlockSpec((1, n, w), lambda bi, i: (bi, 0, 0))
    return pl.pallas_call(
        functools.partial(_natten_kernel, rows_per_step=rows_per_step, rows=rows),
        grid=(b, rows // rows_per_step),
        in_specs=[pl.BlockSpec((1, t, w), lambda bi, i: (bi, i, 0)),
                  full(l), full(l), full(kc.shape[1]), full(kc.shape[1]),
                  _const_spec(bias.shape)],
        out_specs=pl.BlockSpec((1, t, w), lambda bi, i: (bi, i, 0)),
        out_shape=jax.ShapeDtypeStruct((b, l, w), BF16),
        compiler_params=_cparams(2),
        name="natten",
    )(q, k, v, kc, vc, bias)


def _na_ctx_kernel(q_ref, k_ref, v_ref, o_ref):
    n = q_ref.shape[1]
    s = _dot_nt(_head_stack(q_ref[0]), k_ref[0])
    m = jnp.max(s, axis=-1, keepdims=True)
    p = jnp.exp2(s - m)
    denom = jnp.sum(p, axis=-1, keepdims=True)
    o = _dot(p.astype(BF16), v_ref[0]) / denom
    o_ref[0] = _head_unstack(o, n).astype(BF16)


def _na_ctx_call(q, k, v):
    b, n, w = q.shape
    spec = pl.BlockSpec((1, n, w), lambda bi: (bi, 0, 0))
    return pl.pallas_call(
        _na_ctx_kernel, grid=(b,), in_specs=[spec, spec, spec], out_specs=spec,
        out_shape=jax.ShapeDtypeStruct((b, n, w), BF16),
        compiler_params=_cparams(1), name="na_ctx",
    )(q, k, v)


def _fnet1_kernel(z_ref, w_ref, y_ref, *, n_inner):
    zb = z_ref[0]
    p = _dot(w_ref[...], zb.reshape(zb.shape[0], zb.shape[1] * zb.shape[2]))
    n1 = p.shape[0] // 2
    re, im = [], []
    for j in range(n_inner):
        zr_c = p[:n1, j * 512:j * 512 + 256]
        zi_c = p[:n1, j * 512 + 256:(j + 1) * 512]
        zr_s = p[n1:, j * 512:j * 512 + 256]
        zi_s = p[n1:, j * 512 + 256:(j + 1) * 512]
        re.append(zr_c + zi_s)
        im.append(zi_c - zr_s)
    y_ref[0, 0] = jnp.concatenate(re, axis=-1).astype(BF16).reshape(n1, n_inner, GROUP_W)
    y_ref[0, 1] = jnp.concatenate(im, axis=-1).astype(BF16).reshape(n1, n_inner, GROUP_W)


def _fnet1_call(z2, w1s, n1):
    b = z2.shape[0]
    n_inner = FNET_BLOCK
    return pl.pallas_call(
        functools.partial(_fnet1_kernel, n_inner=n_inner),
        grid=(b, DFT_N2 // n_inner),
        in_specs=[pl.BlockSpec((1, n1, n_inner, 512), lambda bi, i: (bi, 0, i, 0)),
                  _const_spec(w1s.shape)],
        out_specs=pl.BlockSpec((1, 2, n1, n_inner, GROUP_W), lambda bi, i: (bi, 0, 0, i, 0)),
        out_shape=jax.ShapeDtypeStruct((b, 2, n1, DFT_N2, GROUP_W), BF16),
        compiler_params=_cparams(2),
        name="fnet_stage1",
    )(z2, w1s)


def _fnet2_kernel(y_ref, f_ref, o_ref, *, n_inner, norm):
    outs = []
    for j in range(n_inner):
        f = f_ref[j]
        o = _dot(f[:, :DFT_N2], y_ref[0, 0, j]) + _dot(f[:, DFT_N2:], y_ref[0, 1, j])
        outs.append((o * norm).astype(BF16))
    o_ref[0] = jnp.concatenate(outs, axis=-1).reshape(DFT_N2, n_inner, GROUP_W)


def _fnet2_call(y5, ftab, n1, norm):
    b = y5.shape[0]
    n_inner = FNET_BLOCK
    return pl.pallas_call(
        functools.partial(_fnet2_kernel, n_inner=n_inner, norm=norm),
        grid=(b, n1 // n_inner),
        in_specs=[pl.BlockSpec((1, 2, n_inner, DFT_N2, GROUP_W), lambda bi, i: (bi, 0, i, 0, 0)),
                  pl.BlockSpec((n_inner, DFT_N2, 2 * DFT_N2), lambda bi, i: (i, 0, 0))],
        out_specs=pl.BlockSpec((1, DFT_N2, n_inner, GROUP_W), lambda bi, i: (bi, 0, i, 0)),
        out_shape=jax.ShapeDtypeStruct((b, DFT_N2, n1, GROUP_W), BF16),
        compiler_params=_cparams(2),
        name="fnet_stage2",
    )(y5, ftab)


def _dft_small_kernel(z_ref, f_ref, o_ref, *, norm):
    z = z_ref[0]
    f = f_ref[...]
    n = z.shape[0]
    o = _dot(f[:, :n], z[:, :GROUP_W]) + _dot(f[:, n:], z[:, GROUP_W:])
    o_ref[0] = (o * norm).astype(BF16)


def _dft_small_call(z, ftab, norm):
    b, n, _ = z.shape
    return pl.pallas_call(
        functools.partial(_dft_small_kernel, norm=norm),
        grid=(b,),
        in_specs=[pl.BlockSpec((1, n, 2 * GROUP_W), lambda bi: (bi, 0, 0)), _const_spec(ftab.shape)],
        out_specs=pl.BlockSpec((1, n, GROUP_W), lambda bi: (bi, 0, 0)),
        out_shape=jax.ShapeDtypeStruct((b, n, GROUP_W), BF16),
        compiler_params=_cparams(1), name="fnet_ctx",
    )(z, ftab)


def _fourier_tables(l):
    if l <= 256:
        m = (np.arange(l)[:, None] * np.arange(l)[None, :]) % l
        ang = 2.0 * np.pi * m / l
        return None, np.concatenate([np.cos(ang), np.sin(ang)], axis=1).astype(np.float32), 0
    n1 = l // DFT_N2
    m1 = (np.arange(n1)[:, None] * np.arange(n1)[None, :]) % n1
    a1 = 2.0 * np.pi * m1 / n1
    w1s = np.concatenate([np.cos(a1), np.sin(a1)], axis=0).astype(np.float32)
    kk = np.arange(n1)[:, None, None] + n1 * np.arange(DFT_N2)[None, :, None]
    m2 = (kk * np.arange(DFT_N2)[None, None, :]) % l
    a2 = 2.0 * np.pi * m2 / l
    ftab = np.concatenate([np.cos(a2), np.sin(a2)], axis=2).astype(np.float32)
    return w1s, ftab, n1


def _fourier_mix(z):
    b, l, _ = z.shape
    norm = float((l * FN_GW) ** -0.5)
    w1s, ftab, n1 = _fourier_tables(l)
    if w1s is None:
        return _dft_small_call(z, jnp.asarray(ftab).astype(BF16), norm)
    y = _fnet1_call(z.reshape(b, n1, DFT_N2, 2 * GROUP_W), jnp.asarray(w1s).astype(BF16), n1)
    o = _fnet2_call(y, jnp.asarray(ftab).astype(BF16), n1, norm)
    return o.reshape(b, l, GROUP_W)


def _conv_kernel(prev_ref, cur_ref, next_ref, w_ref, b_ref, g_ref, beta_ref, o_ref, buf_ref):
    i = pl.program_id(1)
    n = pl.num_programs(1)
    t = cur_ref.shape[1]
    buf_ref[0:CV_HALO, :] = jnp.where(i > 0, prev_ref[0], 0.0)
    buf_ref[CV_HALO:CV_HALO + t, :] = cur_ref[0]
    buf_ref[CV_HALO + t:, :] = jnp.where(i < n - 1, next_ref[0], 0.0)
    w = w_ref[...]
    first = CV_HALO - CV_K // 2
    acc = None
    for res in range(SUBLANES):
        z = None
        for base in range(0, first + CV_K, SUBLANES):
            j = base + res - first
            if 0 <= j < CV_K:
                term = w[j:j + 1, :] * buf_ref[base:base + t + SUBLANES, :]
                z = term if z is None else z + term
        z = z[res:res + t, :]
        acc = z if acc is None else acc + z
    y = acc + b_ref[...]
    mu = jnp.mean(y, axis=-1, keepdims=True)
    var = jnp.mean(jnp.square(y - mu), axis=-1, keepdims=True)
    y = (y - mu) * lax.rsqrt(var + EPS) * g_ref[...] + beta_ref[...]
    o_ref[0] = (y * _sigmoid(y)).astype(BF16)


def _conv_call(cy, w_dw, b_dw, ln_g, ln_b):
    b, l, w = cy.shape
    t = min(l, CONV_TILE)
    hb = t // CV_HALO
    n_halo = l // CV_HALO
    return pl.pallas_call(
        _conv_kernel,
        grid=(b, l // t),
        in_specs=[pl.BlockSpec((1, CV_HALO, w), lambda bi, i: (bi, jnp.maximum(i * hb - 1, 0), 0)),
                  pl.BlockSpec((1, t, w), lambda bi, i: (bi, i, 0)),
                  pl.BlockSpec((1, CV_HALO, w),
                               lambda bi, i: (bi, jnp.minimum((i + 1) * hb, n_halo - 1), 0)),
                  _const_spec(w_dw.shape), _const_spec(b_dw.shape), _const_spec(ln_g.shape),
                  _const_spec(ln_b.shape)],
        out_specs=pl.BlockSpec((1, t, w), lambda bi, i: (bi, i, 0)),
        out_shape=jax.ShapeDtypeStruct((b, l, w), BF16),
        scratch_shapes=[pltpu.VMEM((t + 2 * CV_HALO, w), F32)],
        compiler_params=_cparams(2),
        name="conformer_conv",
    )(cy, cy, cy, w_dw, b_dw, ln_g, ln_b)


def _outmlp_kernel(x_ref, oa_ref, oc_ref, ob_ref, od_ref, wo_ref, gt1_ref, sh2_ref, sc2_ref,
                   gt2_ref, g2_ref, w1_ref, w2_ref, fg_ref, o_ref, *, last, ff_chunk):
    y = (_dot(oa_ref[0], wo_ref[0:GROUP_W, :])
         + _dot(oc_ref[0], wo_ref[GROUP_W:2 * GROUP_W, :])
         + _dot(ob_ref[0], wo_ref[2 * GROUP_W:3 * GROUP_W, :])
         + _dot(od_ref[0], wo_ref[3 * GROUP_W:, :]))
    x1 = x_ref[0] + gt1_ref[0] * y
    h = (_rms(x1, g2_ref[...]) * (1.0 + sc2_ref[0]) + sh2_ref[0]).astype(BF16)
    ff = jnp.zeros(x1.shape, F32)
    for c in range(D_FF // ff_chunk):
        u = jnp.maximum(_dot(h, w1_ref[0, :, c * ff_chunk:(c + 1) * ff_chunk]), 0.0)
        ff = ff + _dot((u * u).astype(BF16), w2_ref[0, c * ff_chunk:(c + 1) * ff_chunk, :])
    x2 = x1 + gt2_ref[0] * ff
    if last:
        x2 = _rms(x2, fg_ref[...])
    o_ref[0] = x2


def _outmlp_call(x, o_mla, o_na, o_fn, o_cv, wo, gt1, sh2, sc2, gt2, g2, w1, w2, fg, last, layer):
    b, l, d = x.shape
    t = min(l, TOKEN_TILE)
    tok = lambda w: pl.BlockSpec((1, t, w), lambda bi, i: (bi, i, 0))
    vec = pl.BlockSpec((1, 1, d), lambda bi, i: (bi, 0, 0))
    slab = lambda w: pl.BlockSpec((1,) + w.shape[1:], lambda bi, i: (layer, 0, 0),
                                  pipeline_mode=pl.Buffered(1))
    return pl.pallas_call(
        functools.partial(_outmlp_kernel, last=last, ff_chunk=FF_CHUNK),
        grid=(b, l // t),
        in_specs=[tok(d), tok(GROUP_W), tok(GROUP_W), tok(GROUP_W), tok(GROUP_W),
                  _const_spec(wo.shape), vec, vec, vec, vec, _const_spec(g2.shape),
                  slab(w1), slab(w2), _const_spec(fg.shape)],
        out_specs=tok(d),
        out_shape=jax.ShapeDtypeStruct((b, l, d), F32),
        compiler_params=_cparams(2),
        name="outproj_mlp",
    )(x, o_mla, o_na, o_fn, o_cv, wo, gt1, sh2, sc2, gt2, g2, w1, w2, fg)


def _rope_tables(l, with_rope):
    per_axis = MLA_ROPE // 2
    n_freq = per_axis // 2

    def slot(nope, row_part, col_part, sign):
        n = row_part.shape[0]
        return jnp.concatenate([jnp.full((n, MLA_NOPE), nope, F32), sign * row_part, sign * col_part,
                                row_part, col_part, jnp.zeros((n, HEAD_SLOT - MLA_QK), F32)], axis=-1)

    if not with_rope:
        one, zero = jnp.ones((l, n_freq), F32), jnp.zeros((l, n_freq), F32)
        return jnp.concatenate([slot(1.0, one, one, 1.0), slot(0.0, zero, zero, -1.0)], axis=-1)
    inv = ROPE_BASE ** (-jnp.arange(0, per_axis, 2, dtype=F32) / per_axis)
    r_ang = jnp.arange(l // GRID_W).astype(F32)[:, None] * inv
    c_ang = jnp.arange(GRID_W).astype(F32)[:, None] * inv
    zr, zc = jnp.zeros_like(r_ang), jnp.zeros_like(c_ang)
    by_row = jnp.concatenate([slot(0.0, jnp.cos(r_ang), zr, 1.0), slot(0.0, jnp.sin(r_ang), zr, -1.0)], axis=-1)
    by_col = jnp.concatenate([slot(1.0, zc, jnp.cos(c_ang), 1.0), slot(0.0, zc, jnp.sin(c_ang), -1.0)], axis=-1)
    return (by_row[:, None, :] + by_col[None, :, :]).reshape(l, 2 * HEAD_SLOT)


def _swap_halves(w):
    half = w.shape[-1] // 2
    return jnp.concatenate([w[..., half:], w[..., :half]], axis=-1)


def _pad_w_in(w_in):
    k_r_end = 256 + 128 + MLA_ROPE
    pad = jnp.zeros(w_in.shape[:-1] + (HEAD_SLOT - MLA_ROPE,), F32)
    return jnp.concatenate([w_in[..., :k_r_end], pad, w_in[..., k_r_end:]], axis=-1).astype(BF16)


def _layer_weights(w_uq, w_ukv, w_out, rpb):
    rq = w_uq.shape[0]
    w3 = w_uq.reshape(rq, N_HEADS, MLA_QK)
    zpad = jnp.zeros((rq, N_HEADS, HEAD_SLOT - MLA_QK), F32)
    plain = jnp.concatenate([w3, zpad], axis=-1)
    swapped = jnp.concatenate([jnp.zeros((rq, N_HEADS, MLA_NOPE), F32),
                               _swap_halves(w3[..., MLA_NOPE:]), zpad], axis=-1)
    wq = jnp.concatenate([plain.reshape(rq, MLA_W), swapped.reshape(rq, MLA_W)], axis=1).astype(BF16)

    rkv = w_ukv.shape[0]
    u3 = w_ukv.reshape(rkv, N_HEADS, MLA_NOPE + MLA_V)
    eye = jnp.eye(MLA_ROPE, dtype=F32)
    slot_pad_r = jnp.zeros((MLA_ROPE, N_HEADS, HEAD_SLOT - MLA_QK), F32)

    def slots(top, rope_block):
        top = jnp.concatenate([top, jnp.zeros((rkv, N_HEADS, HEAD_SLOT - top.shape[-1]), F32)], axis=-1)
        mid = jnp.concatenate([jnp.zeros((MLA_ROPE, N_HEADS, MLA_NOPE), F32),
                               jnp.broadcast_to(rope_block[:, None, :], (MLA_ROPE, N_HEADS, MLA_ROPE)),
                               slot_pad_r], axis=-1)
        bot = jnp.zeros((2 * HEAD_SLOT - rkv - MLA_ROPE, N_HEADS, HEAD_SLOT), F32)
        return jnp.concatenate([top, mid, bot], axis=0).reshape(2 * HEAD_SLOT, MLA_W)

    k_plain = slots(u3[..., :MLA_NOPE], eye)
    k_swap = slots(jnp.zeros((rkv, N_HEADS, MLA_NOPE), F32), _swap_halves(eye))
    wkv = jnp.concatenate([k_plain, k_swap], axis=1).astype(BF16)

    vt3 = jnp.transpose(u3[..., MLA_NOPE:], (1, 2, 0))
    vt3 = jnp.pad(vt3, ((0, 0), (0, VT_ROWS - MLA_V), (0, 2 * HEAD_SLOT - rkv)))
    wvt = vt3.reshape(N_HEADS * VT_ROWS, 2 * HEAD_SLOT).astype(BF16)

    wo = w_out.astype(BF16)

    qc = np.arange(GRID_W)[:, None]
    kc = np.arange(GRID_W)[None, :]
    ws = np.clip(qc - NA_WIN_W // 2, 0, GRID_W - NA_WIN_W)
    in_win = (kc >= ws) & (kc < ws + NA_WIN_W)
    edge = GRID_W - NA_WIN_W
    n = 2 * GRID_W - 1
    rp = jnp.pad(rpb.astype(F32), ((0, 0), (0, 0), (edge, edge)))
    tiled = jnp.tile(rp, (1, 1, GRID_W + 1))[:, :, :GRID_W * (n + 1)]
    toep = tiled.reshape(rp.shape[0], rp.shape[1], GRID_W, n + 1)[:, :, ::-1, :GRID_W]
    t2 = jnp.where(in_win[None, None], toep * LOG2_E, NEG_INF)
    bias = jnp.concatenate([t2[:, :-1], t2[:, 1:]], axis=-1)
    return wq, wkv, wvt, wo, bias


def _channel_dft():
    m = (np.arange(FN_GW)[:, None] * np.arange(FN_GW)[None, :]) % FN_GW
    ang = 2.0 * np.pi * m / FN_GW
    eye = np.eye(FN_GROUPS)
    return np.concatenate([np.kron(eye, np.cos(ang)), -np.kron(eye, np.sin(ang))],
                          axis=1).astype(np.float32)


def kernel(x, c, ctx, c_ctx, w_mod, b_mod, norm1_g, norm2_g, w_in, mla_q_norm, mla_w_uq, mla_kv_norm, mla_w_ukv, na_rpb, cv_w_dw, cv_b_dw, cv_ln_g, cv_ln_b, w_out, w_ff1, w_ff2, final_g):
    depth = w_mod.shape[0]
    b, s, d = x.shape
    n_ctx = ctx.shape[1]

    cc = jnp.concatenate([c, c_ctx[None, :], jnp.zeros((8 - b - 1, d), F32)], axis=0)
    mods = _mod_call(cc, w_mod, b_mod)

    cs_lat = _rope_tables(s, True)
    cs_ctx = _rope_tables(n_ctx, False)
    dc = jnp.asarray(_channel_dft()).astype(BF16)
    row = lambda p: p.reshape(1, -1)
    fg = row(final_g)
    w1 = w_ff1.astype(BF16)
    w2 = w_ff2.astype(BF16)
    win = _pad_w_in(w_in)

    xc = ctx
    for i in range(depth):
        last = i == depth - 1
        wq, wkv, wvt, wo, bias = _layer_weights(mla_w_uq[i], mla_w_ukv[i], w_out[i], na_rpb[i])
        mx =[m.reshape(b, 1, d) for m in jnp.split(mods[i, :b], 6, axis=-1)]
        mc = [jnp.broadcast_to(m.reshape(1, 1, d), (b, 1, d))
              for m in jnp.split(mods[i, b:b + 1], 6, axis=-1)]
        proj = functools.partial(_inproj_call, g1=row(norm1_g[i]), win=win, qn=row(mla_q_norm[i]),
                                 wq=wq, kvn=row(mla_kv_norm[i]), wkv=wkv, wvt=wvt, dc=dc, layer=i)
        conv = functools.partial(_conv_call, w_dw=cv_w_dw[i], b_dw=row(cv_b_dw[i]),
                                 ln_g=row(cv_ln_g[i]), ln_b=row(cv_ln_b[i]))
        mlp = functools.partial(_outmlp_call, wo=wo, g2=row(norm2_g[i]), w1=w1, w2=w2, fg=fg, layer=i)

        q, k, vt, nq, nk, nv, z, cy = proj(x, mx[0], mx[1], cs=cs_lat)
        cq, ck, cvt, cnq, cnk, cnv, cz, ccy = proj(xc, mc[0], mc[1], cs=cs_ctx)

        o_mla = _flash_call(q, k, vt, ck, cvt)
        o_na = _natten_call(nq, nk, nv, cnk, cnv, bias)
        o_fn = _fourier_mix(z)
        o_cv = conv(cy)
        x = mlp(x, o_mla, o_na, o_fn, o_cv, gt1=mx[2], sh2=mx[3], sc2=mx[4], gt2=mx[5], last=last)

        if not last:
            co_mla = _flash_call(cq, ck, cvt)
            co_na = _na_ctx_call(cnq, cnk, cnv)
            co_fn = _fourier_mix(cz)
            co_cv = conv(ccy)
            xc = mlp(xc, co_mla, co_na, co_fn, co_cv, gt1=mc[2], sh2=mc[3], sc2=mc[4], gt2=mc[5],
                     last=False)
    return x
```

```python
import functools

import jax
import jax.numpy as jnp
import numpy as np
from jax import lax
from jax.experimental import pallas as pl
from jax.experimental.pallas import tpu as pltpu

F32 = jnp.float32
BF16 = jnp.bfloat16

EPS = 1e-6
ROPE_BASE = 10000.0
NEG_INF = -1e30

D_MODEL = 1024
GRID_W = 64
N_HEADS = 4
GROUP_W = 256
HEAD_SLOT = 128
MLA_NOPE = 64
MLA_ROPE = 32
MLA_QK = MLA_NOPE + MLA_ROPE
MLA_V = 64
MLA_W = N_HEADS * HEAD_SLOT
VT_ROWS = MLA_V
LOG2_E = 1.4426950408889634
NA_HD = 64
NA_WIN_H = 8
NA_WIN_W = 16
FN_GROUPS = 4
FN_GW = GROUP_W // FN_GROUPS
CV_K = 31
CV_HALO = 16
SUBLANES = 8
D_FF = 4 * D_MODEL
DFT_N2 = 128
V7X_VMEM_LIMIT = 56 * 1024 * 1024

MOD_TILE_N = 1536
INPROJ_TILE = 1024
TOKEN_TILE = 512
CONV_TILE = 1024
ATTN_TILE_Q = 2048
ATTN_TILE_K = 1024
ATTN_HEADS_PER_STEP = 2
NATTEN_ROWS = 16
FNET_BLOCK = 16
FF_CHUNK = 1024


def _cparams(n_grid):
    return pltpu.CompilerParams(dimension_semantics=("parallel",) * n_grid,
                                vmem_limit_bytes=V7X_VMEM_LIMIT)


def _const_spec(shape):
    nd = len(shape)
    return pl.BlockSpec(shape, lambda *_: (0,) * nd, pipeline_mode=pl.Buffered(1))


def _sigmoid(v):
    return 1.0 / (1.0 + jnp.exp(-v))


def _rms(v, g):
    return v * lax.rsqrt(jnp.mean(v * v, axis=-1, keepdims=True) + EPS) * g


def _dot(a, b):
    return jnp.dot(a, b, preferred_element_type=F32)


def _dot_nt(a, b):
    return lax.dot_general(a, b, (((1,), (1,)), ((), ())), preferred_element_type=F32)


def _mod_kernel(c_ref, w_ref, b_ref, o_ref):
    c = c_ref[...]
    s = (c * _sigmoid(c)).astype(BF16)
    o_ref[0] = _dot(s, w_ref[0].astype(BF16)) + b_ref[0]


def _mod_call(cc, w_mod, b_mod):
    depth, d, n = w_mod.shape
    tn = MOD_TILE_N
    return pl.pallas_call(
        _mod_kernel,
        grid=(depth, n // tn),
        in_specs=[pl.BlockSpec(cc.shape, lambda l, j: (0, 0)),
                  pl.BlockSpec((1, d, tn), lambda l, j: (l, 0, j)),
                  pl.BlockSpec((1, 1, tn), lambda l, j: (l, 0, j))],
        out_specs=pl.BlockSpec((1, cc.shape[0], tn), lambda l, j: (l, 0, j)),
        out_shape=jax.ShapeDtypeStruct((depth, cc.shape[0], n), F32),
        compiler_params=_cparams(2),
        name="mod",
    )(cc, w_mod, b_mod.reshape(depth, 1, n))


def _inproj_kernel(x_ref, sh_ref, sc_ref, g1_ref, win_ref, qn_ref, wq_ref, kvn_ref, wkv_ref,
                   wvt_ref, dc_ref, cs_ref, cst_ref,
                   qt_ref, k_ref, vt_ref, nq_ref, nk_ref, nv_ref, z_ref, cy_ref):
    x = x_ref[0]
    h = _rms(x, g1_ref[...]) * (1.0 + sc_ref[0]) + sh_ref[0]
    p = _dot(h.astype(BF16), win_ref[0])

    cs = cs_ref[...]
    cos_t, sin_t = cs[:, :HEAD_SLOT], cs[:, HEAD_SLOT:]
    cos4 = jnp.concatenate([cos_t] * N_HEADS, axis=-1)
    sin4 = jnp.concatenate([sin_t] * N_HEADS, axis=-1)

    r = _rms(p[:, 0:256], qn_ref[...]).astype(BF16)
    qq_t = _dot_nt(wq_ref[...], r)
    cst = cst_ref[...]
    cos4_t = jnp.concatenate([cst[:HEAD_SLOT]] * N_HEADS, axis=0)
    sin4_t = jnp.concatenate([cst[HEAD_SLOT:]] * N_HEADS, axis=0)
    q_t = (qq_t[:MLA_W] * cos4_t + qq_t[MLA_W:] * sin4_t) * (MLA_QK ** -0.5 * LOG2_E)
    qt_ref[0] = q_t.astype(BF16)

    kvr = p[:, 256:512]
    kvn = _rms(kvr[:, :128], kvn_ref[...])
    comb = jnp.concatenate([kvn, kvr[:, 128:]], axis=-1).astype(BF16)
    kk = _dot(comb, wkv_ref[...])
    k = kk[:, :MLA_W] * cos4 + kk[:, MLA_W:] * sin4
    k_ref[0] = k.astype(BF16)
    vt_ref[0] = _dot_nt(wvt_ref[...], comb).astype(BF16)

    nq_ref[0] = (p[:, 512:768] * (NA_HD ** -0.5 * LOG2_E)).astype(BF16)
    nk_ref[0] = p[:, 768:1024].astype(BF16)
    nv_ref[0] = p[:, 1024:1280].astype(BF16)

    z_ref[0] = _dot(p[:, 1280:1536].astype(BF16), dc_ref[...]).astype(BF16)

    cy_ref[0] = p[:, 1536:1792] * _sigmoid(p[:, 1792:2048])


def _inproj_call(x, sh, sc, g1, win, qn, wq, kvn, wkv, wvt, dc, cs, layer):
    b, l, d = x.shape
    t = min(l, INPROJ_TILE)
    tok = lambda w: pl.BlockSpec((1, t, w), lambda bi, i: (bi, i, 0))
    vec = pl.BlockSpec((1, 1, d), lambda bi, i: (bi, 0, 0))
    outs = [MLA_W, (MLA_W, BF16), N_HEADS * VT_ROWS, (GROUP_W, BF16), (GROUP_W, BF16),
            (GROUP_W, BF16), (2 * GROUP_W, BF16), (GROUP_W, F32)]
    transposed = lambda rows: pl.BlockSpec((1, rows, t), lambda bi, i: (bi, 0, i))
    out_specs = [tok(o[0]) if isinstance(o, tuple) else transposed(o) for o in outs]
    out_shape = [jax.ShapeDtypeStruct((b, l, o[0]), o[1]) if isinstance(o, tuple)
                 else jax.ShapeDtypeStruct((b, o, l), BF16) for o in outs]
    return pl.pallas_call(
        _inproj_kernel,
        grid=(b, l // t),
        in_specs=[tok(d), vec, vec, _const_spec(g1.shape),
                  pl.BlockSpec((1,) + win.shape[1:], lambda bi, i: (layer, 0, 0),
                               pipeline_mode=pl.Buffered(1)),
                  _const_spec(qn.shape), _const_spec(wq.shape), _const_spec(kvn.shape),
                  _const_spec(wkv.shape), _const_spec(wvt.shape), _const_spec(dc.shape),
                  pl.BlockSpec((t, 2 * HEAD_SLOT), lambda bi, i: (i, 0)),
                  pl.BlockSpec((2 * HEAD_SLOT, t), lambda bi, i: (0, i))],
        out_specs=out_specs,
        out_shape=out_shape,
        compiler_params=_cparams(2),
        name="inproj",
    )(x, sh, sc, g1, win, qn, wq, kvn, wkv, wvt, dc, cs, cs.T)


def _flash_kernel(*refs, n_chunks, tk, has_extra):
    if has_extra:
        q_ref, k_ref, vt_ref, k2_ref, vt2_ref, o_ref, sa_ref, sb_ref, sx_ref = refs
    else:
        q_ref, k_ref, vt_ref, o_ref, sa_ref, sb_ref = refs
    tq = q_ref.shape[2]

    def head_lanes(hh):
        return slice(hh * HEAD_SLOT, (hh + 1) * HEAD_SLOT)

    def head_scores(hh, kc, s_ref):
        st = _dot(kc, q_ref[0, head_lanes(hh), :])
        s_ref[...] = st
        return jnp.max(st, axis=0, keepdims=True)

    def first_scores(hh):
        return head_scores(hh, k_ref[0, 0:tk, head_lanes(hh)], sa_ref)

    def one_head(hh, mx_first, start_next):
        lanes = head_lanes(hh)
        vrows = slice(hh * VT_ROWS, (hh + 1) * VT_ROWS)
        scores = functools.partial(head_scores, hh)

        def accumulate(s_ref, mx, vtc, carry):
            m, (acc, l) = carry
            m_new = jnp.maximum(m, mx)
            p = jnp.exp2(s_ref[...] - m_new)
            alpha = jnp.exp2(m - m_new)
            l = alpha * l + jnp.sum(p, axis=0, keepdims=True)
            return m_new, (alpha * acc + _dot(vtc, p.astype(BF16)), l)

        def k_chunk(c):
            return k_ref[0, pl.ds(pl.multiple_of(c * tk, tk), tk), lanes]

        def vt_chunk(c):
            return vt_ref[0, vrows, pl.ds(pl.multiple_of(c * tk, tk), tk)]

        def pair(j, carry):
            mx_a, m, acc = carry
            mx_b = scores(k_chunk(2 * j + 1), sb_ref)
            m, acc = accumulate(sa_ref, mx_a, vt_chunk(2 * j), (m, acc))
            mx_a = scores(k_chunk(2 * j + 2), sa_ref)
            m, acc = accumulate(sb_ref, mx_b, vt_chunk(2 * j + 1), (m, acc))
            return mx_a, m, acc

        n_pairs = (n_chunks - 1) // 2
        carry = (mx_first, jnp.full((1, tq), -jnp.inf, F32),
                 (jnp.zeros((VT_ROWS, tq), F32), jnp.zeros((1, tq), F32)))
        mx, m, acc = lax.fori_loop(0, n_pairs, pair, carry)
        pending = (sa_ref, mx, vt_chunk(2 * n_pairs))
        tail = [(k_chunk(c), vt_chunk(c), sb_ref if c % 2 else sa_ref)
                for c in range(2 * n_pairs + 1, n_chunks)]
        if has_extra:
            tail.append((k2_ref[0, :, lanes], vt2_ref[0, vrows, :], sx_ref))
        for kc, vtc, s_ref in tail:
            mx_next = scores(kc, s_ref)
            m, acc = accumulate(*pending, (m, acc))
            pending = (s_ref, mx_next, vtc)
        early = start_next is not None and pending[0] is not sa_ref
        mx_next_head = start_next() if early else None
        m, (acc, l) = accumulate(*pending, (m, acc))
        if start_next is not None and not early:
            mx_next_head = start_next()
        return acc / l, mx_next_head

    outs = []
    mx_first = first_scores(0)
    for hh in range(ATTN_HEADS_PER_STEP):
        nxt = functools.partial(first_scores, hh + 1) if hh + 1 < ATTN_HEADS_PER_STEP else None
        o_h, mx_first = one_head(hh, mx_first, nxt)
        outs.append(o_h)
    o_ref[0] = jnp.concatenate(outs, axis=0).T.astype(BF16)


def _flash_call(qt, k, vt, k2=None, vt2=None):
    b, _, lq = qt.shape
    lk = k.shape[1]
    tq = min(lq, ATTN_TILE_Q)
    tk = min(lk, ATTN_TILE_K)
    has_extra = k2 is not None
    hps = ATTN_HEADS_PER_STEP
    qspec = pl.BlockSpec((1, hps * HEAD_SLOT, tq), lambda bi, h, i: (bi, h, i))
    kspec = lambda n: pl.BlockSpec((1, n, hps * HEAD_SLOT), lambda bi, h, i: (bi, 0, h))
    vspec = lambda n: pl.BlockSpec((1, hps * VT_ROWS, n), lambda bi, h, i: (bi, h, 0))
    ospec = pl.BlockSpec((1, tq, hps * MLA_V), lambda bi, h, i: (bi, i, h))
    in_specs = [qspec, kspec(lk), vspec(lk)]
    args = [qt, k, vt]
    scratch = [pltpu.VMEM((tk, tq), F32), pltpu.VMEM((tk, tq), F32)]
    if has_extra:
        in_specs += [kspec(k2.shape[1]), vspec(k2.shape[1])]
        args += [k2, vt2]
        scratch.append(pltpu.VMEM((k2.shape[1], tq), F32))
    return pl.pallas_call(
        functools.partial(_flash_kernel, n_chunks=lk // tk, tk=tk, has_extra=has_extra),
        grid=(b, N_HEADS // hps, lq // tq),
        in_specs=in_specs,
        out_specs=ospec,
        out_shape=jax.ShapeDtypeStruct((b, lq, N_HEADS * MLA_V), BF16),
        scratch_shapes=scratch,
        compiler_params=_cparams(3),
        name="mla_attn",
    )(*args)


def _head_stack(q):
    lane = lax.broadcasted_iota(jnp.int32, q.shape, 1)
    return jnp.concatenate(
        [jnp.where(lane // NA_HD == h, q, jnp.zeros_like(q)) for h in range(N_HEADS)], axis=0)


def _head_unstack(o, n):
    lane = lax.broadcasted_iota(jnp.int32, (n, GROUP_W), 1)
    out = jnp.zeros((n, GROUP_W), F32)
    for h in range(N_HEADS):
        out = out + jnp.where(lane // NA_HD == h, o[h * n:(h + 1) * n], 0.0)
    return out


def _natten_kernel(q_ref, k_ref, v_ref, kc_ref, vc_ref, bias_ref, o_ref, *, rows_per_step, rows):
    blk = pl.program_id(1)
    kc = kc_ref[0]
    vc = vc_ref[0]
    n_loc = NA_WIN_H * GRID_W

    def body(j, _):
        r = blk * rows_per_step + j
        rs = jnp.clip(r - NA_WIN_H // 2, 0, rows - NA_WIN_H)
        d0 = rs - r + (NA_WIN_H - 1)
        qs = _head_stack(q_ref[0, pl.ds(pl.multiple_of(j * GRID_W, GRID_W), GRID_W), :])
        kstart = pl.multiple_of(rs * GRID_W, GRID_W)
        s_loc = _dot_nt(qs, k_ref[0, pl.ds(kstart, n_loc), :])
        bias = jnp.concatenate(
            [jnp.concatenate([bias_ref[h, d0 + 2 * w] for w in range(NA_WIN_H // 2)], axis=-1)
             for h in range(N_HEADS)], axis=0)
        s_loc = s_loc + bias
        s_ctx = _dot_nt(qs, kc)
        m = jnp.maximum(jnp.max(s_loc, axis=-1, keepdims=True),
                        jnp.max(s_ctx, axis=-1, keepdims=True))
        p_loc = jnp.exp2(s_loc - m)
        p_ctx = jnp.exp2(s_ctx - m)
        denom = jnp.sum(p_loc, axis=-1, keepdims=True) + jnp.sum(p_ctx, axis=-1, keepdims=True)
        o = (_dot(p_loc.astype(BF16), v_ref[0, pl.ds(kstart, n_loc), :])
             + _dot(p_ctx.astype(BF16), vc))
        o = _head_unstack(o / denom, GRID_W)
        o_ref[0, pl.ds(pl.multiple_of(j * GRID_W, GRID_W), GRID_W), :] = o.astype(BF16)
        return 0

    lax.fori_loop(0, rows_per_step, body, 0, unroll=True)


def _natten_call(q, k, v, kc, vc, bias):
    b, l, w = q.shape
    rows = l // GRID_W
    rows_per_step = NATTEN_ROWS
    t = rows_per_step * GRID_W
    full = lambda n: pl.BlockSpec((1, n, w), lambda bi, i: (bi, 0, 0))
    return pl.pallas_call(
        functools.partial(_natten_kernel, rows_per_step=rows_per_step, rows=rows),
        grid=(b, rows // rows_per_step),
        in_specs=[pl.BlockSpec((1, t, w), lambda bi, i: (bi, i, 0)),
                  full(l), full(l), full(kc.shape[1]), full(kc.shape[1]),
                  _const_spec(bias.shape)],
        out_specs=pl.BlockSpec((1, t, w), lambda bi, i: (bi, i, 0)),
        out_shape=jax.ShapeDtypeStruct((b, l, w), BF16),
        compiler_params=_cparams(2),
        name="natten",
    )(q, k, v, kc, vc, bias)


def _na_ctx_kernel(q_ref, k_ref, v_ref, o_ref):
    n = q_ref.shape[1]
    s = _dot_nt(_head_stack(q_ref[0]), k_ref[0])
    m = jnp.max(s, axis=-1, keepdims=True)
    p = jnp.exp2(s - m)
    denom = jnp.sum(p, axis=-1, keepdims=True)
    o = _dot(p.astype(BF16), v_ref[0]) / denom
    o_ref[0] = _head_unstack(o, n).astype(BF16)


def _na_ctx_call(q, k, v):
    b, n, w = q.shape
    spec = pl.BlockSpec((1, n, w), lambda bi: (bi, 0, 0))
    return pl.pallas_call(
        _na_ctx_kernel, grid=(b,), in_specs=[spec, spec, spec], out_specs=spec,
        out_shape=jax.ShapeDtypeStruct((b, n, w), BF16),
        compiler_params=_cparams(1), name="na_ctx",
    )(q, k, v)


def _fnet1_kernel(z_ref, w_ref, y_ref, *, n_inner):
    zb = z_ref[0]
    p = _dot(w_ref[...], zb.reshape(zb.shape[0], zb.shape[1] * zb.shape[2]))
    n1 = p.shape[0] // 2
    re, im = [], []
    for j in range(n_inner):
        zr_c = p[:n1, j * 512:j * 512 + 256]
        zi_c = p[:n1, j * 512 + 256:(j + 1) * 512]
        zr_s = p[n1:, j * 512:j * 512 + 256]
        zi_s = p[n1:, j * 512 + 256:(j + 1) * 512]
        re.append(zr_c + zi_s)
        im.append(zi_c - zr_s)
    y_ref[0, 0] = jnp.concatenate(re, axis=-1).astype(BF16).reshape(n1, n_inner, GROUP_W)
    y_ref[0, 1] = jnp.concatenate(im, axis=-1).astype(BF16).reshape(n1, n_inner, GROUP_W)


def _fnet1_call(z2, w1s, n1):
    b = z2.shape[0]
    n_inner = FNET_BLOCK
    return pl.pallas_call(
        functools.partial(_fnet1_kernel, n_inner=n_inner),
        grid=(b, DFT_N2 // n_inner),
        in_specs=[pl.BlockSpec((1, n1, n_inner, 512), lambda bi, i: (bi, 0, i, 0)),
                  _const_spec(w1s.shape)],
        out_specs=pl.BlockSpec((1, 2, n1, n_inner, GROUP_W), lambda bi, i: (bi, 0, 0, i, 0)),
        out_shape=jax.ShapeDtypeStruct((b, 2, n1, DFT_N2, GROUP_W), BF16),
        compiler_params=_cparams(2),
        name="fnet_stage1",
    )(z2, w1s)


def _fnet2_kernel(y_ref, f_ref, o_ref, *, n_inner, norm):
    outs = []
    for j in range(n_inner):
        f = f_ref[j]
        o = _dot(f[:, :DFT_N2], y_ref[0, 0, j]) + _dot(f[:, DFT_N2:], y_ref[0, 1, j])
        outs.append((o * norm).astype(BF16))
    o_ref[0] = jnp.concatenate(outs, axis=-1).reshape(DFT_N2, n_inner, GROUP_W)


def _fnet2_call(y5, ftab, n1, norm):
    b = y5.shape[0]
    n_inner = FNET_BLOCK
    return pl.pallas_call(
        functools.partial(_fnet2_kernel, n_inner=n_inner, norm=norm),
        grid=(b, n1 // n_inner),
        in_specs=[pl.BlockSpec((1, 2, n_inner, DFT_N2, GROUP_W), lambda bi, i: (bi, 0, i, 0, 0)),
                  pl.BlockSpec((n_inner, DFT_N2, 2 * DFT_N2), lambda bi, i: (i, 0, 0))],
        out_specs=pl.BlockSpec((1, DFT_N2, n_inner, GROUP_W), lambda bi, i: (bi, 0, i, 0)),
        out_shape=jax.ShapeDtypeStruct((b, DFT_N2, n1, GROUP_W), BF16),
        compiler_params=_cparams(2),
        name="fnet_stage2",
    )(y5, ftab)


def _dft_small_kernel(z_ref, f_ref, o_ref, *, norm):
    z = z_ref[0]
    f = f_ref[...]
    n = z.shape[0]
    o = _dot(f[:, :n], z[:, :GROUP_W]) + _dot(f[:, n:], z[:, GROUP_W:])
    o_ref[0] = (o * norm).astype(BF16)


def _dft_small_call(z, ftab, norm):
    b, n, _ = z.shape
    return pl.pallas_call(
        functools.partial(_dft_small_kernel, norm=norm),
        grid=(b,),
        in_specs=[pl.BlockSpec((1, n, 2 * GROUP_W), lambda bi: (bi, 0, 0)), _const_spec(ftab.shape)],
        out_specs=pl.BlockSpec((1, n, GROUP_W), lambda bi: (bi, 0, 0)),
        out_shape=jax.ShapeDtypeStruct((b, n, GROUP_W), BF16),
        compiler_params=_cparams(1), name="fnet_ctx",
    )(z, ftab)


def _fourier_tables(l):
    if l <= 256:
        m = (np.arange(l)[:, None] * np.arange(l)[None, :]) % l
        ang = 2.0 * np.pi * m / l
        return None, np.concatenate([np.cos(ang), np.sin(ang)], axis=1).astype(np.float32), 0
    n1 = l // DFT_N2
    m1 = (np.arange(n1)[:, None] * np.arange(n1)[None, :]) % n1
    a1 = 2.0 * np.pi * m1 / n1
    w1s = np.concatenate([np.cos(a1), np.sin(a1)], axis=0).astype(np.float32)
    kk = np.arange(n1)[:, None, None] + n1 * np.arange(DFT_N2)[None, :, None]
    m2 = (kk * np.arange(DFT_N2)[None, None, :]) % l
    a2 = 2.0 * np.pi * m2 / l
    ftab = np.concatenate([np.cos(a2), np.sin(a2)], axis=2).astype(np.float32)
    return w1s, ftab, n1


def _fourier_mix(z):
    b, l, _ = z.shape
    norm = float((l * FN_GW) ** -0.5)
    w1s, ftab, n1 = _fourier_tables(l)
    if w1s is None:
        return _dft_small_call(z, jnp.asarray(ftab).astype(BF16), norm)
    y = _fnet1_call(z.reshape(b, n1, DFT_N2, 2 * GROUP_W), jnp.asarray(w1s).astype(BF16), n1)
    o = _fnet2_call(y, jnp.asarray(ftab).astype(BF16), n1, norm)
    return o.reshape(b, l, GROUP_W)


def _conv_kernel(prev_ref, cur_ref, next_ref, w_ref, b_ref, g_ref, beta_ref, o_ref, buf_ref):
    i = pl.program_id(1)
    n = pl.num_programs(1)
    t = cur_ref.shape[1]
    buf_ref[0:CV_HALO, :] = jnp.where(i > 0, prev_ref[0], 0.0)
    buf_ref[CV_HALO:CV_HALO + t, :] = cur_ref[0]
    buf_ref[CV_HALO + t:, :] = jnp.where(i < n - 1, next_ref[0], 0.0)
    w = w_ref[...]
    first = CV_HALO - CV_K // 2
    acc = None
    for res in range(SUBLANES):
        z = None
        for base in range(0, first + CV_K, SUBLANES):
            j = base + res - first
            if 0 <= j < CV_K:
                term = w[j:j + 1, :] * buf_ref[base:base + t + SUBLANES, :]
                z = term if z is None else z + term
        z = z[res:res + t, :]
        acc = z if acc is None else acc + z
    y = acc + b_ref[...]
    mu = jnp.mean(y, axis=-1, keepdims=True)
    var = jnp.mean(jnp.square(y - mu), axis=-1, keepdims=True)
    y = (y - mu) * lax.rsqrt(var + EPS) * g_ref[...] + beta_ref[...]
    o_ref[0] = (y * _sigmoid(y)).astype(BF16)


def _conv_call(cy, w_dw, b_dw, ln_g, ln_b):
    b, l, w = cy.shape
    t = min(l, CONV_TILE)
    hb = t // CV_HALO
    n_halo = l // CV_HALO
    return pl.pallas_call(
        _conv_kernel,
        grid=(b, l // t),
        in_specs=[pl.BlockSpec((1, CV_HALO, w), lambda bi, i: (bi, jnp.maximum(i * hb - 1, 0), 0)),
                  pl.BlockSpec((1, t, w), lambda bi, i: (bi, i, 0)),
                  pl.BlockSpec((1, CV_HALO, w),
                               lambda bi, i: (bi, jnp.minimum((i + 1) * hb, n_halo - 1), 0)),
                  _const_spec(w_dw.shape), _const_spec(b_dw.shape), _const_spec(ln_g.shape),
                  _const_spec(ln_b.shape)],
        out_specs=pl.BlockSpec((1, t, w), lambda bi, i: (bi, i, 0)),
        out_shape=jax.ShapeDtypeStruct((b, l, w), BF16),
        scratch_shapes=[pltpu.VMEM((t + 2 * CV_HALO, w), F32)],
        compiler_params=_cparams(2),
        name="conformer_conv",
    )(cy, cy, cy, w_dw, b_dw, ln_g, ln_b)


def _outmlp_kernel(x_ref, oa_ref, oc_ref, ob_ref, od_ref, wo_ref, gt1_ref, sh2_ref, sc2_ref,
                   gt2_ref, g2_ref, w1_ref, w2_ref, fg_ref, o_ref, *, last, ff_chunk):
    y = (_dot(oa_ref[0], wo_ref[0:GROUP_W, :])
         + _dot(oc_ref[0], wo_ref[GROUP_W:2 * GROUP_W, :])
         + _dot(ob_ref[0], wo_ref[2 * GROUP_W:3 * GROUP_W, :])
         + _dot(od_ref[0], wo_ref[3 * GROUP_W:, :]))
    x1 = x_ref[0] + gt1_ref[0] * y
    h = (_rms(x1, g2_ref[...]) * (1.0 + sc2_ref[0]) + sh2_ref[0]).astype(BF16)
    ff = jnp.zeros(x1.shape, F32)
    for c in range(D_FF // ff_chunk):
        u = jnp.maximum(_dot(h, w1_ref[0, :, c * ff_chunk:(c + 1) * ff_chunk]), 0.0)
        ff = ff + _dot((u * u).astype(BF16), w2_ref[0, c * ff_chunk:(c + 1) * ff_chunk, :])
    x2 = x1 + gt2_ref[0] * ff
    if last:
        x2 = _rms(x2, fg_ref[...])
    o_ref[0] = x2


def _outmlp_call(x, o_mla, o_na, o_fn, o_cv, wo, gt1, sh2, sc2, gt2, g2, w1, w2, fg, last, layer):
    b, l, d = x.shape
    t = min(l, TOKEN_TILE)
    tok = lambda w: pl.BlockSpec((1, t, w), lambda bi, i: (bi, i, 0))
    vec = pl.BlockSpec((1, 1, d), lambda bi, i: (bi, 0, 0))
    slab = lambda w: pl.BlockSpec((1,) + w.shape[1:], lambda bi, i: (layer, 0, 0),
                                  pipeline_mode=pl.Buffered(1))
    return pl.pallas_call(
        functools.partial(_outmlp_kernel, last=last, ff_chunk=FF_CHUNK),
        grid=(b, l // t),
        in_specs=[tok(d), tok(GROUP_W), tok(GROUP_W), tok(GROUP_W), tok(GROUP_W),
                  _const_spec(wo.shape), vec, vec, vec, vec, _const_spec(g2.shape),
                  slab(w1), slab(w2), _const_spec(fg.shape)],
        out_specs=tok(d),
        out_shape=jax.ShapeDtypeStruct((b, l, d), F32),
        compiler_params=_cparams(2),
        name="outproj_mlp",
    )(x, o_mla, o_na, o_fn, o_cv, wo, gt1, sh2, sc2, gt2, g2, w1, w2, fg)


def _rope_tables(l, with_rope):
    per_axis = MLA_ROPE // 2
    n_freq = per_axis // 2

    def slot(nope, row_part, col_part, sign):
        n = row_part.shape[0]
        return jnp.concatenate([jnp.full((n, MLA_NOPE), nope, F32), sign * row_part, sign * col_part,
                                row_part, col_part, jnp.zeros((n, HEAD_SLOT - MLA_QK), F32)], axis=-1)

    if not with_rope:
        one, zero = jnp.ones((l, n_freq), F32), jnp.zeros((l, n_freq), F32)
        return jnp.concatenate([slot(1.0, one, one, 1.0), slot(0.0, zero, zero, -1.0)], axis=-1)
    inv = ROPE_BASE ** (-jnp.arange(0, per_axis, 2, dtype=F32) / per_axis)
    r_ang = jnp.arange(l // GRID_W).astype(F32)[:, None] * inv
    c_ang = jnp.arange(GRID_W).astype(F32)[:, None] * inv
    zr, zc = jnp.zeros_like(r_ang), jnp.zeros_like(c_ang)
    by_row = jnp.concatenate([slot(0.0, jnp.cos(r_ang), zr, 1.0), slot(0.0, jnp.sin(r_ang), zr, -1.0)], axis=-1)
    by_col = jnp.concatenate([slot(1.0, zc, jnp.cos(c_ang), 1.0), slot(0.0, zc, jnp.sin(c_ang), -1.0)], axis=-1)
    return (by_row[:, None, :] + by_col[None, :, :]).reshape(l, 2 * HEAD_SLOT)


def _swap_halves(w):
    half = w.shape[-1] // 2
    return jnp.concatenate([w[..., half:], w[..., :half]], axis=-1)


def _pad_w_in(w_in):
    k_r_end = 256 + 128 + MLA_ROPE
    pad = jnp.zeros(w_in.shape[:-1] + (HEAD_SLOT - MLA_ROPE,), F32)
    return jnp.concatenate([w_in[..., :k_r_end], pad, w_in[..., k_r_end:]], axis=-1).astype(BF16)


def _layer_weights(w_uq, w_ukv, w_out, rpb):
    rq = w_uq.shape[0]
    w3 = w_uq.reshape(rq, N_HEADS, MLA_QK)
    zpad = jnp.zeros((rq, N_HEADS, HEAD_SLOT - MLA_QK), F32)
    plain = jnp.concatenate([w3, zpad], axis=-1)
    swapped = jnp.concatenate([jnp.zeros((rq, N_HEADS, MLA_NOPE), F32),
                               _swap_halves(w3[..., MLA_NOPE:]), zpad], axis=-1)
    wq = jnp.concatenate([plain.reshape(rq, MLA_W), swapped.reshape(rq, MLA_W)], axis=1).T.astype(BF16)

    rkv = w_ukv.shape[0]
    u3 = w_ukv.reshape(rkv, N_HEADS, MLA_NOPE + MLA_V)
    eye = jnp.eye(MLA_ROPE, dtype=F32)
    slot_pad_r = jnp.zeros((MLA_ROPE, N_HEADS, HEAD_SLOT - MLA_QK), F32)

    def slots(top, rope_block):
        top = jnp.concatenate([top, jnp.zeros((rkv, N_HEADS, HEAD_SLOT - top.shape[-1]), F32)], axis=-1)
        mid = jnp.concatenate([jnp.zeros((MLA_ROPE, N_HEADS, MLA_NOPE), F32),
                               jnp.broadcast_to(rope_block[:, None, :], (MLA_ROPE, N_HEADS, MLA_ROPE)),
                               slot_pad_r], axis=-1)
        bot = jnp.zeros((2 * HEAD_SLOT - rkv - MLA_ROPE, N_HEADS, HEAD_SLOT), F32)
        return jnp.concatenate([top, mid, bot], axis=0).reshape(2 * HEAD_SLOT, MLA_W)

    k_plain = slots(u3[..., :MLA_NOPE], eye)
    k_swap = slots(jnp.zeros((rkv, N_HEADS, MLA_NOPE), F32), _swap_halves(eye))
    wkv = jnp.concatenate([k_plain, k_swap], axis=1).astype(BF16)

    vt3 = jnp.transpose(u3[..., MLA_NOPE:], (1, 2, 0))
    vt3 = jnp.pad(vt3, ((0, 0), (0, VT_ROWS - MLA_V), (0, 2 * HEAD_SLOT - rkv)))
    wvt = vt3.reshape(N_HEADS * VT_ROWS, 2 * HEAD_SLOT).astype(BF16)

    wo = w_out.astype(BF16)

    qc = np.arange(GRID_W)[:, None]
    kc = np.arange(GRID_W)[None, :]
    ws = np.clip(qc - NA_WIN_W // 2, 0, GRID_W - NA_WIN_W)
    in_win = (kc >= ws) & (kc < ws + NA_WIN_W)
    edge = GRID_W - NA_WIN_W
    n = 2 * GRID_W - 1
    rp = jnp.pad(rpb.astype(F32), ((0, 0), (0, 0), (edge, edge)))
    tiled = jnp.tile(rp, (1, 1, GRID_W + 1))[:, :, :GRID_W * (n + 1)]
    toep = tiled.reshape(rp.shape[0], rp.shape[1], GRID_W, n + 1)[:, :, ::-1, :GRID_W]
    t2 = jnp.where(in_win[None, None], toep * LOG2_E, NEG_INF)
    bias = jnp.concatenate([t2[:, :-1], t2[:, 1:]], axis=-1)
    return wq, wkv, wvt, wo, bias


def _channel_dft():
    m = (np.arange(FN_GW)[:, None] * np.arange(FN_GW)[None, :]) % FN_GW
    ang = 2.0 * np.pi * m / FN_GW
    eye = np.eye(FN_GROUPS)
    return np.concatenate([np.kron(eye, np.cos(ang)), -np.kron(eye, np.sin(ang))],
                          axis=1).astype(np.float32)


def kernel(x, c, ctx, c_ctx, w_mod, b_mod, norm1_g, norm2_g, w_in, mla_q_norm, mla_w_uq, mla_kv_norm, mla_w_ukv, na_rpb, cv_w_dw, cv_b_dw, cv_ln_g, cv_ln_b, w_out, w_ff1, w_ff2, final_g):
    depth = w_mod.shape[0]
    b, s, d = x.shape
    n_ctx = ctx.shape[1]

    cc = jnp.concatenate([c, c_ctx[None, :], jnp.zeros((8 - b - 1, d), F32)], axis=0)
    mods = _mod_call(cc, w_mod, b_mod)

    cs_lat = _rope_tables(s, True)
    cs_ctx = _rope_tables(n_ctx, False)
    dc = jnp.asarray(_channel_dft()).astype(BF16)
    row = lambda p: p.reshape(1, -1)
    fg = row(final_g)
    w1 = w_ff1.astype(BF16)
    w2 = w_ff2.astype(BF16)
    win = _pad_w_in(w_in)

    xc = ctx
    for i in range(depth):
        last = i == depth - 1
        wq, wkv, wvt, wo, bias = _layer_weights(mla_w_uq[i], mla_w_ukv[i], w_out[i], na_rpb[i])
        mx =[m.reshape(b, 1, d) for m in jnp.split(mods[i, :b], 6, axis=-1)]
        mc = [jnp.broadcast_to(m.reshape(1, 1, d), (b, 1, d))
              for m in jnp.split(mods[i, b:b + 1], 6, axis=-1)]
        proj = functools.partial(_inproj_call, g1=row(norm1_g[i]), win=win, qn=row(mla_q_norm[i]),
                                 wq=wq, kvn=row(mla_kv_norm[i]), wkv=wkv, wvt=wvt, dc=dc, layer=i)
        conv = functools.partial(_conv_call, w_dw=cv_w_dw[i], b_dw=row(cv_b_dw[i]),
                                 ln_g=row(cv_ln_g[i]), ln_b=row(cv_ln_b[i]))
        mlp = functools.partial(_outmlp_call, wo=wo, g2=row(norm2_g[i]), w1=w1, w2=w2, fg=fg, layer=i)

        qt, k, vt, nq, nk, nv, z, cy = proj(x, mx[0], mx[1], cs=cs_lat)
        cqt, ck, cvt, cnq, cnk, cnv, cz, ccy = proj(xc, mc[0], mc[1], cs=cs_ctx)

        o_mla = _flash_call(qt, k, vt, ck, cvt)
        o_na = _natten_call(nq, nk, nv, cnk, cnv, bias)
        o_fn = _fourier_mix(z)
        o_cv = conv(cy)
        x = mlp(x, o_mla, o_na, o_fn, o_cv, gt1=mx[2], sh2=mx[3], sc2=mx[4], gt2=mx[5], last=last)

        if not last:
            co_mla = _flash_call(cqt, ck, cvt)
            co_na = _na_ctx_call(cnq, cnk, cnv)
            co_fn = _fourier_mix(cz)
            co_cv = conv(ccy)
            xc = mlp(xc, co_mla, co_na, co_fn, co_cv, gt1=mc[2], sh2=mc[3], sc2=mc[4], gt2=mc[5],
                     last=False)
    return x
```

```python
import functools

import jax
import jax.numpy as jnp
import numpy as np
from jax import lax
from jax.experimental import pallas as pl
from jax.experimental.pallas import tpu as pltpu

F32 = jnp.float32
BF16 = jnp.bfloat16

EPS = 1e-6
ROPE_BASE = 10000.0
NEG_INF = -1e30

D_MODEL = 1024
GRID_W = 64
N_HEADS = 4
GROUP_W = 256
HEAD_SLOT = 128
MLA_NOPE = 64
MLA_ROPE = 32
MLA_QK = MLA_NOPE + MLA_ROPE
MLA_V = 64
MLA_W = N_HEADS * HEAD_SLOT
VT_ROWS = MLA_V
LOG2_E = 1.4426950408889634
NA_HD = 64
NA_WIN_H = 8
NA_WIN_W = 16
FN_GROUPS = 4
FN_GW = GROUP_W // FN_GROUPS
CV_K = 31
CV_HALO = 16
SUBLANES = 8
D_FF = 4 * D_MODEL
DFT_N2 = 128
V7X_VMEM_LIMIT = 56 * 1024 * 1024

MOD_TILE_N = 1536
INPROJ_TILE = 1024
TOKEN_TILE = 512
CONV_TILE = 2048
ATTN_TILE_Q = 2048
ATTN_TILE_K = 1024
ATTN_HEADS_PER_STEP = 2
NATTEN_ROWS = 16
FNET_BLOCK = 32
FF_CHUNK = 1024


def _cparams(n_grid):
    return pltpu.CompilerParams(dimension_semantics=("parallel",) * n_grid,
                                vmem_limit_bytes=V7X_VMEM_LIMIT)


def _const_spec(shape):
    nd = len(shape)
    return pl.BlockSpec(shape, lambda *_: (0,) * nd, pipeline_mode=pl.Buffered(1))


def _sigmoid(v):
    return 1.0 / (1.0 + jnp.exp(-v))


def _rms(v, g):
    return v * lax.rsqrt(jnp.mean(v * v, axis=-1, keepdims=True) + EPS) * g


def _dot(a, b):
    return jnp.dot(a, b, preferred_element_type=F32)


def _dot_nt(a, b):
    return lax.dot_general(a, b, (((1,), (1,)), ((), ())), preferred_element_type=F32)


def _mod_kernel(c_ref, w_ref, b_ref, o_ref):
    c = c_ref[...]
    s = (c * _sigmoid(c)).astype(BF16)
    o_ref[0] = _dot(s, w_ref[0].astype(BF16)) + b_ref[0]


def _mod_call(cc, w_mod, b_mod):
    depth, d, n = w_mod.shape
    tn = MOD_TILE_N
    return pl.pallas_call(
        _mod_kernel,
        grid=(depth, n // tn),
        in_specs=[pl.BlockSpec(cc.shape, lambda l, j: (0, 0)),
                  pl.BlockSpec((1, d, tn), lambda l, j: (l, 0, j)),
                  pl.BlockSpec((1, 1, tn), lambda l, j: (l, 0, j))],
        out_specs=pl.BlockSpec((1, cc.shape[0], tn), lambda l, j: (l, 0, j)),
        out_shape=jax.ShapeDtypeStruct((depth, cc.shape[0], n), F32),
        compiler_params=_cparams(2),
        name="mod",
    )(cc, w_mod, b_mod.reshape(depth, 1, n))


def _inproj_kernel(x_ref, sh_ref, sc_ref, g1_ref, win_ref, qn_ref, wq_ref, kvn_ref, wkv_ref,
                   wvt_ref, dc_ref, cs_ref, cst_ref,
                   qt_ref, k_ref, vt_ref, nq_ref, nk_ref, nv_ref, z_ref, cy_ref):
    x = x_ref[0]
    h = _rms(x, g1_ref[...]) * (1.0 + sc_ref[0]) + sh_ref[0]
    p = _dot(h.astype(BF16), win_ref[0])

    cs = cs_ref[...]
    cos_t, sin_t = cs[:, :HEAD_SLOT], cs[:, HEAD_SLOT:]
    cos4 = jnp.concatenate([cos_t] * N_HEADS, axis=-1)
    sin4 = jnp.concatenate([sin_t] * N_HEADS, axis=-1)

    r = _rms(p[:, 0:256], qn_ref[...]).astype(BF16)
    qq_t = _dot_nt(wq_ref[...], r)
    cst = cst_ref[...]
    cos4_t = jnp.concatenate([cst[:HEAD_SLOT]] * N_HEADS, axis=0)
    sin4_t = jnp.concatenate([cst[HEAD_SLOT:]] * N_HEADS, axis=0)
    q_t = (qq_t[:MLA_W] * cos4_t + qq_t[MLA_W:] * sin4_t) * (MLA_QK ** -0.5 * LOG2_E)
    qt_ref[0] = q_t.astype(BF16)

    kvr = p[:, 256:512]
    kvn = _rms(kvr[:, :128], kvn_ref[...])
    comb = jnp.concatenate([kvn, kvr[:, 128:]], axis=-1).astype(BF16)
    kk = _dot(comb, wkv_ref[...])
    k = kk[:, :MLA_W] * cos4 + kk[:, MLA_W:] * sin4
    k_ref[0] = k.astype(BF16)
    vt_ref[0] = _dot_nt(wvt_ref[...], comb).astype(BF16)

    nq_ref[0] = (p[:, 512:768] * (NA_HD ** -0.5 * LOG2_E)).astype(BF16)
    nk_ref[0] = p[:, 768:1024].astype(BF16)
    nv_ref[0] = p[:, 1024:1280].astype(BF16)

    z_ref[0] = _dot(p[:, 1280:1536].astype(BF16), dc_ref[...]).astype(BF16)

    cy_ref[0] = p[:, 1536:1792] * _sigmoid(p[:, 1792:2048])


def _inproj_call(x, sh, sc, g1, win, qn, wq, kvn, wkv, wvt, dc, cs, layer):
    b, l, d = x.shape
    t = min(l, INPROJ_TILE)
    tok = lambda w: pl.BlockSpec((1, t, w), lambda bi, i: (bi, i, 0))
    vec = pl.BlockSpec((1, 1, d), lambda bi, i: (bi, 0, 0))
    outs = [MLA_W, (MLA_W, BF16), N_HEADS * VT_ROWS, (GROUP_W, BF16), (GROUP_W, BF16),
            (GROUP_W, BF16), (2 * GROUP_W, BF16), (GROUP_W, F32)]
    transposed = lambda rows: pl.BlockSpec((1, rows, t), lambda bi, i: (bi, 0, i))
    out_specs = [tok(o[0]) if isinstance(o, tuple) else transposed(o) for o in outs]
    out_shape = [jax.ShapeDtypeStruct((b, l, o[0]), o[1]) if isinstance(o, tuple)
                 else jax.ShapeDtypeStruct((b, o, l), BF16) for o in outs]
    return pl.pallas_call(
        _inproj_kernel,
        grid=(b, l // t),
        in_specs=[tok(d), vec, vec, _const_spec(g1.shape),
                  pl.BlockSpec((1,) + win.shape[1:], lambda bi, i: (layer, 0, 0),
                               pipeline_mode=pl.Buffered(1)),
                  _const_spec(qn.shape), _const_spec(wq.shape), _const_spec(kvn.shape),
                  _const_spec(wkv.shape), _const_spec(wvt.shape), _const_spec(dc.shape),
                  pl.BlockSpec((t, 2 * HEAD_SLOT), lambda bi, i: (i, 0)),
                  pl.BlockSpec((2 * HEAD_SLOT, t), lambda bi, i: (0, i))],
        out_specs=out_specs,
        out_shape=out_shape,
        compiler_params=_cparams(2),
        name="inproj",
    )(x, sh, sc, g1, win, qn, wq, kvn, wkv, wvt, dc, cs, cs.T)


def _flash_kernel(*refs, n_chunks, tk, has_extra):
    if has_extra:
        q_ref, k_ref, vt_ref, k2_ref, vt2_ref, o_ref, sa_ref, sb_ref, sx_ref = refs
    else:
        q_ref, k_ref, vt_ref, o_ref, sa_ref, sb_ref = refs
    tq = q_ref.shape[2]

    def head_lanes(hh):
        return slice(hh * HEAD_SLOT, (hh + 1) * HEAD_SLOT)

    def head_scores(hh, kc, s_ref):
        st = _dot(kc, q_ref[0, head_lanes(hh), :])
        s_ref[...] = st
        return jnp.max(st, axis=0, keepdims=True)

    def first_scores(hh):
        return head_scores(hh, k_ref[0, 0:tk, head_lanes(hh)], sa_ref)

    def one_head(hh, mx_first, start_next):
        lanes = head_lanes(hh)
        vrows = slice(hh * VT_ROWS, (hh + 1) * VT_ROWS)
        scores = functools.partial(head_scores, hh)

        def accumulate(s_ref, mx, vtc, carry):
            m, (acc, l) = carry
            m_new = jnp.maximum(m, mx)
            p = jnp.exp2(s_ref[...] - m_new)
            alpha = jnp.exp2(m - m_new)
            l = alpha * l + jnp.sum(p, axis=0, keepdims=True)
            return m_new, (alpha * acc + _dot(vtc, p.astype(BF16)), l)

        def k_chunk(c):
            return k_ref[0, pl.ds(pl.multiple_of(c * tk, tk), tk), lanes]

        def vt_chunk(c):
            return vt_ref[0, vrows, pl.ds(pl.multiple_of(c * tk, tk), tk)]

        def pair(j, carry):
            mx_a, m, acc = carry
            mx_b = scores(k_chunk(2 * j + 1), sb_ref)
            m, acc = accumulate(sa_ref, mx_a, vt_chunk(2 * j), (m, acc))
            mx_a = scores(k_chunk(2 * j + 2), sa_ref)
            m, acc = accumulate(sb_ref, mx_b, vt_chunk(2 * j + 1), (m, acc))
            return mx_a, m, acc

        n_pairs = (n_chunks - 1) // 2
        carry = (mx_first, jnp.full((1, tq), -jnp.inf, F32),
                 (jnp.zeros((VT_ROWS, tq), F32), jnp.zeros((1, tq), F32)))
        mx, m, acc = lax.fori_loop(0, n_pairs, pair, carry)
        pending = (sa_ref, mx, vt_chunk(2 * n_pairs))
        tail = [(k_chunk(c), vt_chunk(c), sb_ref if c % 2 else sa_ref)
                for c in range(2 * n_pairs + 1, n_chunks)]
        if has_extra:
            tail.append((k2_ref[0, :, lanes], vt2_ref[0, vrows, :], sx_ref))
        for kc, vtc, s_ref in tail:
            mx_next = scores(kc, s_ref)
            m, acc = accumulate(*pending, (m, acc))
            pending = (s_ref, mx_next, vtc)
        early = start_next is not None and pending[0] is not sa_ref
        mx_next_head = start_next() if early else None
        m, (acc, l) = accumulate(*pending, (m, acc))
        if start_next is not None and not early:
            mx_next_head = start_next()
        return acc / l, mx_next_head

    outs = []
    mx_first = first_scores(0)
    for hh in range(ATTN_HEADS_PER_STEP):
        nxt = functools.partial(first_scores, hh + 1) if hh + 1 < ATTN_HEADS_PER_STEP else None
        o_h, mx_first = one_head(hh, mx_first, nxt)
        outs.append(o_h)
    o_ref[0] = jnp.concatenate(outs, axis=0).T.astype(BF16)


def _flash_call(qt, k, vt, k2=None, vt2=None):
    b, _, lq = qt.shape
    lk = k.shape[1]
    tq = min(lq, ATTN_TILE_Q)
    tk = min(lk, ATTN_TILE_K)
    has_extra = k2 is not None
    hps = ATTN_HEADS_PER_STEP
    qspec = pl.BlockSpec((1, hps * HEAD_SLOT, tq), lambda bi, h, i: (bi, h, i))
    kspec = lambda n: pl.BlockSpec((1, n, hps * HEAD_SLOT), lambda bi, h, i: (bi, 0, h))
    vspec = lambda n: pl.BlockSpec((1, hps * VT_ROWS, n), lambda bi, h, i: (bi, h, 0))
    ospec = pl.BlockSpec((1, tq, hps * MLA_V), lambda bi, h, i: (bi, i, h))
    in_specs = [qspec, kspec(lk), vspec(lk)]
    args = [qt, k, vt]
    scratch = [pltpu.VMEM((tk, tq), F32), pltpu.VMEM((tk, tq), F32)]
    if has_extra:
        in_specs += [kspec(k2.shape[1]), vspec(k2.shape[1])]
        args += [k2, vt2]
        scratch.append(pltpu.VMEM((k2.shape[1], tq), F32))
    return pl.pallas_call(
        functools.partial(_flash_kernel, n_chunks=lk // tk, tk=tk, has_extra=has_extra),
        grid=(b, N_HEADS // hps, lq // tq),
        in_specs=in_specs,
        out_specs=ospec,
        out_shape=jax.ShapeDtypeStruct((b, lq, N_HEADS * MLA_V), BF16),
        scratch_shapes=scratch,
        compiler_params=_cparams(3),
        name="mla_attn",
    )(*args)


def _head_stack(q):
    lane = lax.broadcasted_iota(jnp.int32, q.shape, 1)
    return jnp.concatenate(
        [jnp.where(lane // NA_HD == h, q, jnp.zeros_like(q)) for h in range(N_HEADS)], axis=0)


def _head_unstack(o, n):
    lane = lax.broadcasted_iota(jnp.int32, (n, GROUP_W), 1)
    out = jnp.zeros((n, GROUP_W), F32)
    for h in range(N_HEADS):
        out = out + jnp.where(lane // NA_HD == h, o[h * n:(h + 1) * n], 0.0)
    return out


def _natten_kernel(q_ref, k_ref, v_ref, kc_ref, vc_ref, bias_ref, o_ref, *, rows_per_step, rows):
    blk = pl.program_id(1)
    kc = kc_ref[0]
    vc = vc_ref[0]
    n_loc = NA_WIN_H * GRID_W

    def body(j, _):
        r = blk * rows_per_step + j
        rs = jnp.clip(r - NA_WIN_H // 2, 0, rows - NA_WIN_H)
        d0 = rs - r + (NA_WIN_H - 1)
        qs = _head_stack(q_ref[0, pl.ds(pl.multiple_of(j * GRID_W, GRID_W), GRID_W), :])
        kstart = pl.multiple_of(rs * GRID_W, GRID_W)
        s_loc = _dot_nt(qs, k_ref[0, pl.ds(kstart, n_loc), :])
        bias = jnp.concatenate(
            [jnp.concatenate([bias_ref[h, d0 + 2 * w] for w in range(NA_WIN_H // 2)], axis=-1)
             for h in range(N_HEADS)], axis=0)
        s_loc = s_loc + bias
        s_ctx = _dot_nt(qs, kc)
        m = jnp.maximum(jnp.max(s_loc, axis=-1, keepdims=True),
                        jnp.max(s_ctx, axis=-1, keepdims=True))
        p_loc = jnp.exp2(s_loc - m)
        p_ctx = jnp.exp2(s_ctx - m)
        denom = jnp.sum(p_loc, axis=-1, keepdims=True) + jnp.sum(p_ctx, axis=-1, keepdims=True)
        o = (_dot(p_loc.astype(BF16), v_ref[0, pl.ds(kstart, n_loc), :])
             + _dot(p_ctx.astype(BF16), vc))
        o = _head_unstack(o / denom, GRID_W)
        o_ref[0, pl.ds(pl.multiple_of(j * GRID_W, GRID_W), GRID_W), :] = o.astype(BF16)
        return 0

    lax.fori_loop(0, rows_per_step, body, 0, unroll=True)


def _natten_call(q, k, v, kc, vc, bias):
    b, l, w = q.shape
    rows = l // GRID_W
    rows_per_step = NATTEN_ROWS
    t = rows_per_step * GRID_W
    full = lambda n: pl.BlockSpec((1, n, w), lambda bi, i: (bi, 0, 0))
    return pl.pallas_call(
        functools.partial(_natten_kernel, rows_per_step=rows_per_step, rows=rows),
        grid=(b, rows // rows_per_step),
        in_specs=[pl.BlockSpec((1, t, w), lambda bi, i: (bi, i, 0)),
                  full(l), full(l), full(kc.shape[1]), full(kc.shape[1]),
                  _const_spec(bias.shape)],
        out_specs=pl.BlockSpec((1, t, w), lambda bi, i: (bi, i, 0)),
        out_shape=jax.ShapeDtypeStruct((b, l, w), BF16),
        compiler_params=_cparams(2),
        name="natten",
    )(q, k, v, kc, vc, bias)


def _na_ctx_kernel(q_ref, k_ref, v_ref, o_ref):
    n = q_ref.shape[1]
    s = _dot_nt(_head_stack(q_ref[0]), k_ref[0])
    m = jnp.max(s, axis=-1, keepdims=True)
    p = jnp.exp2(s - m)
    denom = jnp.sum(p, axis=-1, keepdims=True)
    o = _dot(p.astype(BF16), v_ref[0]) / denom
    o_ref[0] = _head_unstack(o, n).astype(BF16)


def _na_ctx_call(q, k, v):
    b, n, w = q.shape
    spec = pl.BlockSpec((1, n, w), lambda bi: (bi, 0, 0))
    return pl.pallas_call(
        _na_ctx_kernel, grid=(b,), in_specs=[spec, spec, spec], out_specs=spec,
        out_shape=jax.ShapeDtypeStruct((b, n, w), BF16),
        compiler_params=_cparams(1), name="na_ctx",
    )(q, k, v)


def _fnet1_kernel(z_ref, w_ref, y_ref, *, n_inner):
    zb = z_ref[0]
    p = _dot(w_ref[...], zb.reshape(zb.shape[0], zb.shape[1] * zb.shape[2]))
    n1 = p.shape[0] // 2
    re, im = [], []
    for j in range(n_inner):
        zr_c = p[:n1, j * 512:j * 512 + 256]
        zi_c = p[:n1, j * 512 + 256:(j + 1) * 512]
        zr_s = p[n1:, j * 512:j * 512 + 256]
        zi_s = p[n1:, j * 512 + 256:(j + 1) * 512]
        re.append(zr_c + zi_s)
        im.append(zi_c - zr_s)
    y_ref[0, 0] = jnp.concatenate(re, axis=-1).astype(BF16).reshape(n1, n_inner, GROUP_W)
    y_ref[0, 1] = jnp.concatenate(im, axis=-1).astype(BF16).reshape(n1, n_inner, GROUP_W)


def _fnet1_call(z2, w1s, n1):
    b = z2.shape[0]
    n_inner = FNET_BLOCK
    return pl.pallas_call(
        functools.partial(_fnet1_kernel, n_inner=n_inner),
        grid=(b, DFT_N2 // n_inner),
        in_specs=[pl.BlockSpec((1, n1, n_inner, 512), lambda bi, i: (bi, 0, i, 0)),
                  _const_spec(w1s.shape)],
        out_specs=pl.BlockSpec((1, 2, n1, n_inner, GROUP_W), lambda bi, i: (bi, 0, 0, i, 0)),
        out_shape=jax.ShapeDtypeStruct((b, 2, n1, DFT_N2, GROUP_W), BF16),
        compiler_params=_cparams(2),
        name="fnet_stage1",
    )(z2, w1s)


def _fnet2_kernel(y_ref, f_ref, o_ref, *, n_inner, norm):
    outs = []
    for j in range(n_inner):
        f = f_ref[j]
        o = _dot(f[:, :DFT_N2], y_ref[0, 0, j]) + _dot(f[:, DFT_N2:], y_ref[0, 1, j])
        outs.append((o * norm).astype(BF16))
    o_ref[0] = jnp.concatenate(outs, axis=-1).reshape(DFT_N2, n_inner, GROUP_W)


def _fnet2_call(y5, ftab, n1, norm):
    b = y5.shape[0]
    n_inner = FNET_BLOCK
    return pl.pallas_call(
        functools.partial(_fnet2_kernel, n_inner=n_inner, norm=norm),
        grid=(b, n1 // n_inner),
        in_specs=[pl.BlockSpec((1, 2, n_inner, DFT_N2, GROUP_W), lambda bi, i: (bi, 0, i, 0, 0)),
                  pl.BlockSpec((n_inner, DFT_N2, 2 * DFT_N2), lambda bi, i: (i, 0, 0))],
        out_specs=pl.BlockSpec((1, DFT_N2, n_inner, GROUP_W), lambda bi, i: (bi, 0, i, 0)),
        out_shape=jax.ShapeDtypeStruct((b, DFT_N2, n1, GROUP_W), BF16),
        compiler_params=_cparams(2),
        name="fnet_stage2",
    )(y5, ftab)


def _dft_small_kernel(z_ref, f_ref, o_ref, *, norm):
    z = z_ref[0]
    f = f_ref[...]
    n = z.shape[0]
    o = _dot(f[:, :n], z[:, :GROUP_W]) + _dot(f[:, n:], z[:, GROUP_W:])
    o_ref[0] = (o * norm).astype(BF16)


def _dft_small_call(z, ftab, norm):
    b, n, _ = z.shape
    return pl.pallas_call(
        functools.partial(_dft_small_kernel, norm=norm),
        grid=(b,),
        in_specs=[pl.BlockSpec((1, n, 2 * GROUP_W), lambda bi: (bi, 0, 0)), _const_spec(ftab.shape)],
        out_specs=pl.BlockSpec((1, n, GROUP_W), lambda bi: (bi, 0, 0)),
        out_shape=jax.ShapeDtypeStruct((b, n, GROUP_W), BF16),
        compiler_params=_cparams(1), name="fnet_ctx",
    )(z, ftab)


def _fourier_tables(l):
    if l <= 256:
        m = (np.arange(l)[:, None] * np.arange(l)[None, :]) % l
        ang = 2.0 * np.pi * m / l
        return None, np.concatenate([np.cos(ang), np.sin(ang)], axis=1).astype(np.float32), 0
    n1 = l // DFT_N2
    m1 = (np.arange(n1)[:, None] * np.arange(n1)[None, :]) % n1
    a1 = 2.0 * np.pi * m1 / n1
    w1s = np.concatenate([np.cos(a1), np.sin(a1)], axis=0).astype(np.float32)
    kk = np.arange(n1)[:, None, None] + n1 * np.arange(DFT_N2)[None, :, None]
    m2 = (kk * np.arange(DFT_N2)[None, None, :]) % l
    a2 = 2.0 * np.pi * m2 / l
    ftab = np.concatenate([np.cos(a2), np.sin(a2)], axis=2).astype(np.float32)
    return w1s, ftab, n1


def _fourier_mix(z):
    b, l, _ = z.shape
    norm = float((l * FN_GW) ** -0.5)
    w1s, ftab, n1 = _fourier_tables(l)
    if w1s is None:
        return _dft_small_call(z, jnp.asarray(ftab).astype(BF16), norm)
    y = _fnet1_call(z.reshape(b, n1, DFT_N2, 2 * GROUP_W), jnp.asarray(w1s).astype(BF16), n1)
    o = _fnet2_call(y, jnp.asarray(ftab).astype(BF16), n1, norm)
    return o.reshape(b, l, GROUP_W)


def _conv_kernel(prev_ref, cur_ref, next_ref, w_ref, b_ref, g_ref, beta_ref, o_ref, buf_ref):
    i = pl.program_id(1)
    n = pl.num_programs(1)
    t = cur_ref.shape[1]
    buf_ref[0:CV_HALO, :] = jnp.where(i > 0, prev_ref[0], 0.0)
    buf_ref[CV_HALO:CV_HALO + t, :] = cur_ref[0]
    buf_ref[CV_HALO + t:, :] = jnp.where(i < n - 1, next_ref[0], 0.0)
    w = w_ref[...]
    first = CV_HALO - CV_K // 2
    acc = None
    for res in range(SUBLANES):
        z = None
        for base in range(0, first + CV_K, SUBLANES):
            j = base + res - first
            if 0 <= j < CV_K:
                term = w[j:j + 1, :] * buf_ref[base:base + t + SUBLANES, :]
                z = term if z is None else z + term
        z = z[res:res + t, :]
        acc = z if acc is None else acc + z
    y = acc + b_ref[...]
    mu = jnp.mean(y, axis=-1, keepdims=True)
    var = jnp.mean(jnp.square(y - mu), axis=-1, keepdims=True)
    y = (y - mu) * lax.rsqrt(var + EPS) * g_ref[...] + beta_ref[...]
    o_ref[0] = (y * _sigmoid(y)).astype(BF16)


def _conv_call(cy, w_dw, b_dw, ln_g, ln_b):
    b, l, w = cy.shape
    t = min(l, CONV_TILE)
    hb = t // CV_HALO
    n_halo = l // CV_HALO
    return pl.pallas_call(
        _conv_kernel,
        grid=(b, l // t),
        in_specs=[pl.BlockSpec((1, CV_HALO, w), lambda bi, i: (bi, jnp.maximum(i * hb - 1, 0), 0)),
                  pl.BlockSpec((1, t, w), lambda bi, i: (bi, i, 0)),
                  pl.BlockSpec((1, CV_HALO, w),
                               lambda bi, i: (bi, jnp.minimum((i + 1) * hb, n_halo - 1), 0)),
                  _const_spec(w_dw.shape), _const_spec(b_dw.shape), _const_spec(ln_g.shape),
                  _const_spec(ln_b.shape)],
        out_specs=pl.BlockSpec((1, t, w), lambda bi, i: (bi, i, 0)),
        out_shape=jax.ShapeDtypeStruct((b, l, w), BF16),
        scratch_shapes=[pltpu.VMEM((t + 2 * CV_HALO, w), F32)],
        compiler_params=_cparams(2),
        name="conformer_conv",
    )(cy, cy, cy, w_dw, b_dw, ln_g, ln_b)


def _outmlp_kernel(x_ref, oa_ref, oc_ref, ob_ref, od_ref, wo_ref, gt1_ref, sh2_ref, sc2_ref,
                   gt2_ref, g2_ref, w1_ref, w2_ref, fg_ref, o_ref, *, last, ff_chunk):
    y = (_dot(oa_ref[0], wo_ref[0:GROUP_W, :])
         + _dot(oc_ref[0], wo_ref[GROUP_W:2 * GROUP_W, :])
         + _dot(ob_ref[0], wo_ref[2 * GROUP_W:3 * GROUP_W, :])
         + _dot(od_ref[0], wo_ref[3 * GROUP_W:, :]))
    x1 = x_ref[0] + gt1_ref[0] * y
    h = (_rms(x1, g2_ref[...]) * (1.0 + sc2_ref[0]) + sh2_ref[0]).astype(BF16)
    ff = jnp.zeros(x1.shape, F32)
    for c in range(D_FF // ff_chunk):
        u = jnp.maximum(_dot(h, w1_ref[0, :, c * ff_chunk:(c + 1) * ff_chunk]), 0.0)
        ff = ff + _dot((u * u).astype(BF16), w2_ref[0, c * ff_chunk:(c + 1) * ff_chunk, :])
    x2 = x1 + gt2_ref[0] * ff
    if last:
        x2 = _rms(x2, fg_ref[...])
    o_ref[0] = x2


def _outmlp_call(x, o_mla, o_na, o_fn, o_cv, wo, gt1, sh2, sc2, gt2, g2, w1, w2, fg, last, layer):
    b, l, d = x.shape
    t = min(l, TOKEN_TILE)
    tok = lambda w: pl.BlockSpec((1, t, w), lambda bi, i: (bi, i, 0))
    vec = pl.BlockSpec((1, 1, d), lambda bi, i: (bi, 0, 0))
    slab = lambda w: pl.BlockSpec((1,) + w.shape[1:], lambda bi, i: (layer, 0, 0),
                                  pipeline_mode=pl.Buffered(1))
    return pl.pallas_call(
        functools.partial(_outmlp_kernel, last=last, ff_chunk=FF_CHUNK),
        grid=(b, l // t),
        in_specs=[tok(d), tok(GROUP_W), tok(GROUP_W), tok(GROUP_W), tok(GROUP_W),
                  _const_spec(wo.shape), vec, vec, vec, vec, _const_spec(g2.shape),
                  slab(w1), slab(w2), _const_spec(fg.shape)],
        out_specs=tok(d),
        out_shape=jax.ShapeDtypeStruct((b, l, d), F32),
        compiler_params=_cparams(2),
        name="outproj_mlp",
    )(x, o_mla, o_na, o_fn, o_cv, wo, gt1, sh2, sc2, gt2, g2, w1, w2, fg)


def _rope_tables(l, with_rope):
    per_axis = MLA_ROPE // 2
    n_freq = per_axis // 2

    def slot(nope, row_part, col_part, sign):
        n = row_part.shape[0]
        return jnp.concatenate([jnp.full((n, MLA_NOPE), nope, F32), sign * row_part, sign * col_part,
                                row_part, col_part, jnp.zeros((n, HEAD_SLOT - MLA_QK), F32)], axis=-1)

    if not with_rope:
        one, zero = jnp.ones((l, n_freq), F32), jnp.zeros((l, n_freq), F32)
        return jnp.concatenate([slot(1.0, one, one, 1.0), slot(0.0, zero, zero, -1.0)], axis=-1)
    inv = ROPE_BASE ** (-jnp.arange(0, per_axis, 2, dtype=F32) / per_axis)
    r_ang = jnp.arange(l // GRID_W).astype(F32)[:, None] * inv
    c_ang = jnp.arange(GRID_W).astype(F32)[:, None] * inv
    zr, zc = jnp.zeros_like(r_ang), jnp.zeros_like(c_ang)
    by_row = jnp.concatenate([slot(0.0, jnp.cos(r_ang), zr, 1.0), slot(0.0, jnp.sin(r_ang), zr, -1.0)], axis=-1)
    by_col = jnp.concatenate([slot(1.0, zc, jnp.cos(c_ang), 1.0), slot(0.0, zc, jnp.sin(c_ang), -1.0)], axis=-1)
    return (by_row[:, None, :] + by_col[None, :, :]).reshape(l, 2 * HEAD_SLOT)


def _swap_halves(w):
    half = w.shape[-1] // 2
    return jnp.concatenate([w[..., half:], w[..., :half]], axis=-1)


def _pad_w_in(w_in):
    k_r_end = 256 + 128 + MLA_ROPE
    pad = jnp.zeros(w_in.shape[:-1] + (HEAD_SLOT - MLA_ROPE,), F32)
    return jnp.concatenate([w_in[..., :k_r_end], pad, w_in[..., k_r_end:]], axis=-1).astype(BF16)


def _layer_weights(w_uq, w_ukv, w_out, rpb):
    rq = w_uq.shape[0]
    w3 = w_uq.reshape(rq, N_HEADS, MLA_QK)
    zpad = jnp.zeros((rq, N_HEADS, HEAD_SLOT - MLA_QK), F32)
    plain = jnp.concatenate([w3, zpad], axis=-1)
    swapped = jnp.concatenate([jnp.zeros((rq, N_HEADS, MLA_NOPE), F32),
                               _swap_halves(w3[..., MLA_NOPE:]), zpad], axis=-1)
    wq = jnp.concatenate([plain.reshape(rq, MLA_W), swapped.reshape(rq, MLA_W)], axis=1).T.astype(BF16)

    rkv = w_ukv.shape[0]
    u3 = w_ukv.reshape(rkv, N_HEADS, MLA_NOPE + MLA_V)
    eye = jnp.eye(MLA_ROPE, dtype=F32)
    slot_pad_r = jnp.zeros((MLA_ROPE, N_HEADS, HEAD_SLOT - MLA_QK), F32)

    def slots(top, rope_block):
        top = jnp.concatenate([top, jnp.zeros((rkv, N_HEADS, HEAD_SLOT - top.shape[-1]), F32)], axis=-1)
        mid = jnp.concatenate([jnp.zeros((MLA_ROPE, N_HEADS, MLA_NOPE), F32),
                               jnp.broadcast_to(rope_block[:, None, :], (MLA_ROPE, N_HEADS, MLA_ROPE)),
                               slot_pad_r], axis=-1)
        bot = jnp.zeros((2 * HEAD_SLOT - rkv - MLA_ROPE, N_HEADS, HEAD_SLOT), F32)
        return jnp.concatenate([top, mid, bot], axis=0).reshape(2 * HEAD_SLOT, MLA_W)

    k_plain = slots(u3[..., :MLA_NOPE], eye)
    k_swap = slots(jnp.zeros((rkv, N_HEADS, MLA_NOPE), F32), _swap_halves(eye))
    wkv = jnp.concatenate([k_plain, k_swap], axis=1).astype(BF16)

    vt3 = jnp.transpose(u3[..., MLA_NOPE:], (1, 2, 0))
    vt3 = jnp.pad(vt3, ((0, 0), (0, VT_ROWS - MLA_V), (0, 2 * HEAD_SLOT - rkv)))
    wvt = vt3.reshape(N_HEADS * VT_ROWS, 2 * HEAD_SLOT).astype(BF16)

    wo = w_out.astype(BF16)

    qc = np.arange(GRID_W)[:, None]
    kc = np.arange(GRID_W)[None, :]
    ws = np.clip(qc - NA_WIN_W // 2, 0, GRID_W - NA_WIN_W)
    in_win = (kc >= ws) & (kc < ws + NA_WIN_W)
    edge = GRID_W - NA_WIN_W
    n = 2 * GRID_W - 1
    rp = jnp.pad(rpb.astype(F32), ((0, 0), (0, 0), (edge, edge)))
    tiled = jnp.tile(rp, (1, 1, GRID_W + 1))[:, :, :GRID_W * (n + 1)]
    toep = tiled.reshape(rp.shape[0], rp.shape[1], GRID_W, n + 1)[:, :, ::-1, :GRID_W]
    t2 = jnp.where(in_win[None, None], toep * LOG2_E, NEG_INF)
    bias = jnp.concatenate([t2[:, :-1], t2[:, 1:]], axis=-1)
    return wq, wkv, wvt, wo, bias


def _channel_dft():
    m = (np.arange(FN_GW)[:, None] * np.arange(FN_GW)[None, :]) % FN_GW
    ang = 2.0 * np.pi * m / FN_GW
    eye = np.eye(FN_GROUPS)
    return np.concatenate([np.kron(eye, np.cos(ang)), -np.kron(eye, np.sin(ang))],
                          axis=1).astype(np.float32)


def kernel(x, c, ctx, c_ctx, w_mod, b_mod, norm1_g, norm2_g, w_in, mla_q_norm, mla_w_uq, mla_kv_norm, mla_w_ukv, na_rpb, cv_w_dw, cv_b_dw, cv_ln_g, cv_ln_b, w_out, w_ff1, w_ff2, final_g):
    depth = w_mod.shape[0]
    b, s, d = x.shape
    n_ctx = ctx.shape[1]

    cc = jnp.concatenate([c, c_ctx[None, :], jnp.zeros((8 - b - 1, d), F32)], axis=0)
    mods = _mod_call(cc, w_mod, b_mod)

    cs_lat = _rope_tables(s, True)
    cs_ctx = _rope_tables(n_ctx, False)
    dc = jnp.asarray(_channel_dft()).astype(BF16)
    row = lambda p: p.reshape(1, -1)
    fg = row(final_g)
    w1 = w_ff1.astype(BF16)
    w2 = w_ff2.astype(BF16)
    win = _pad_w_in(w_in)

    xc = ctx
    for i in range(depth):
        last = i == depth - 1
        wq, wkv, wvt, wo, bias = _layer_weights(mla_w_uq[i], mla_w_ukv[i], w_out[i], na_rpb[i])
        mx =[m.reshape(b, 1, d) for m in jnp.split(mods[i, :b], 6, axis=-1)]
        mc = [jnp.broadcast_to(m.reshape(1, 1, d), (b, 1, d))
              for m in jnp.split(mods[i, b:b + 1], 6, axis=-1)]
        proj = functools.partial(_inproj_call, g1=row(norm1_g[i]), win=win, qn=row(mla_q_norm[i]),
                                 wq=wq, kvn=row(mla_kv_norm[i]), wkv=wkv, wvt=wvt, dc=dc, layer=i)
        conv = functools.partial(_conv_call, w_dw=cv_w_dw[i], b_dw=row(cv_b_dw[i]),
                                 ln_g=row(cv_ln_g[i]), ln_b=row(cv_ln_b[i]))
        mlp = functools.partial(_outmlp_call, wo=wo, g2=row(norm2_g[i]), w1=w1, w2=w2, fg=fg, layer=i)

        qt, k, vt, nq, nk, nv, z, cy = proj(x, mx[0], mx[1], cs=cs_lat)
        cqt, ck, cvt, cnq, cnk, cnv, cz, ccy = proj(xc, mc[0], mc[1], cs=cs_ctx)

        o_mla = _flash_call(qt, k, vt, ck, cvt)
        o_na = _natten_call(nq, nk, nv, cnk, cnv, bias)
        o_fn = _fourier_mix(z)
        o_cv = conv(cy)
        x = mlp(x, o_mla, o_na, o_fn, o_cv, gt1=mx[2], sh2=mx[3], sc2=mx[4], gt2=mx[5], last=last)

        if not last:
            co_mla = _flash_call(cqt, ck, cvt)
            co_na = _na_ctx_call(cnq, cnk, cnv)
            co_fn = _fourier_mix(cz)
            co_cv = conv(ccy)
            xc = mlp(xc, co_mla, co_na, co_fn, co_cv, gt1=mc[2], sh2=mc[3], sc2=mc[4], gt2=mc[5],
                     last=False)
    return x
```

```python
import functools

import jax
import jax.numpy as jnp
import numpy as np
from jax import lax
from jax.experimental import pallas as pl
from jax.experimental.pallas import tpu as pltpu

F32 = jnp.float32
BF16 = jnp.bfloat16

EPS = 1e-6
ROPE_BASE = 10000.0
NEG_INF = -1e30

D_MODEL = 1024
GRID_W = 64
N_HEADS = 4
GROUP_W = 256
HEAD_SLOT = 128
MLA_NOPE = 64
MLA_ROPE = 32
MLA_QK = MLA_NOPE + MLA_ROPE
MLA_V = 64
MLA_W = N_HEADS * HEAD_SLOT
VT_ROWS = MLA_V
LOG2_E = 1.4426950408889634
NA_HD = 64
NA_WIN_H = 8
NA_WIN_W = 16
FN_GROUPS = 4
FN_GW = GROUP_W // FN_GROUPS
CV_K = 31
CV_HALO = 16
SUBLANES = 8
D_FF = 4 * D_MODEL
DFT_N2 = 128
V7X_VMEM_LIMIT = 56 * 1024 * 1024

MOD_TILE_N = 1536
INPROJ_TILE = 1024
TOKEN_TILE = 512
CONV_TILE = 2048
ATTN_TILE_Q = 2048
ATTN_TILE_K = 1024
ATTN_HEADS_PER_STEP = 2
NATTEN_ROWS = 16
FNET_BLOCK = 32
FNET_FUSED_BLOCK = 16
FF_CHUNK = 1024


def _cparams(n_grid):
    return pltpu.CompilerParams(dimension_semantics=("parallel",) * n_grid,
                                vmem_limit_bytes=V7X_VMEM_LIMIT)


def _const_spec(shape):
    nd = len(shape)
    return pl.BlockSpec(shape, lambda *_: (0,) * nd, pipeline_mode=pl.Buffered(1))


def _sigmoid(v):
    return 1.0 / (1.0 + jnp.exp(-v))


def _rms(v, g):
    return v * lax.rsqrt(jnp.mean(v * v, axis=-1, keepdims=True) + EPS) * g


def _dot(a, b):
    return jnp.dot(a, b, preferred_element_type=F32)


def _dot_nt(a, b):
    return lax.dot_general(a, b, (((1,), (1,)), ((), ())), preferred_element_type=F32)


def _mod_kernel(c_ref, w_ref, b_ref, o_ref):
    c = c_ref[...]
    s = (c * _sigmoid(c)).astype(BF16)
    o_ref[0] = _dot(s, w_ref[0].astype(BF16)) + b_ref[0]


def _mod_call(cc, w_mod, b_mod):
    depth, d, n = w_mod.shape
    tn = MOD_TILE_N
    return pl.pallas_call(
        _mod_kernel,
        grid=(depth, n // tn),
        in_specs=[pl.BlockSpec(cc.shape, lambda l, j: (0, 0)),
                  pl.BlockSpec((1, d, tn), lambda l, j: (l, 0, j)),
                  pl.BlockSpec((1, 1, tn), lambda l, j: (l, 0, j))],
        out_specs=pl.BlockSpec((1, cc.shape[0], tn), lambda l, j: (l, 0, j)),
        out_shape=jax.ShapeDtypeStruct((depth, cc.shape[0], n), F32),
        compiler_params=_cparams(2),
        name="mod",
    )(cc, w_mod, b_mod.reshape(depth, 1, n))


def _inproj_kernel(x_ref, sh_ref, sc_ref, g1_ref, win_ref, qn_ref, wq_ref, kvn_ref, wkv_ref,
                   wvt_ref, dc_ref, cs_ref, cst_ref,
                   qt_ref, k_ref, vt_ref, nq_ref, nk_ref, nv_ref, z_ref, cy_ref):
    x = x_ref[0]
    h = _rms(x, g1_ref[...]) * (1.0 + sc_ref[0]) + sh_ref[0]
    p = _dot(h.astype(BF16), win_ref[0])

    cs = cs_ref[...]
    cos_t, sin_t = cs[:, :HEAD_SLOT], cs[:, HEAD_SLOT:]
    cos4 = jnp.concatenate([cos_t] * N_HEADS, axis=-1)
    sin4 = jnp.concatenate([sin_t] * N_HEADS, axis=-1)

    r = _rms(p[:, 0:256], qn_ref[...]).astype(BF16)
    qq_t = _dot_nt(wq_ref[...], r)
    cst = cst_ref[...]
    cos4_t = jnp.concatenate([cst[:HEAD_SLOT]] * N_HEADS, axis=0)
    sin4_t = jnp.concatenate([cst[HEAD_SLOT:]] * N_HEADS, axis=0)
    q_t = (qq_t[:MLA_W] * cos4_t + qq_t[MLA_W:] * sin4_t) * (MLA_QK ** -0.5 * LOG2_E)
    qt_ref[0] = q_t.astype(BF16)

    kvr = p[:, 256:512]
    kvn = _rms(kvr[:, :128], kvn_ref[...])
    comb = jnp.concatenate([kvn, kvr[:, 128:]], axis=-1).astype(BF16)
    kk = _dot(comb, wkv_ref[...])
    k = kk[:, :MLA_W] * cos4 + kk[:, MLA_W:] * sin4
    k_ref[0] = k.astype(BF16)
    vt_ref[0] = _dot_nt(wvt_ref[...], comb).astype(BF16)

    nq_ref[0] = (p[:, 512:768] * (NA_HD ** -0.5 * LOG2_E)).astype(BF16)
    nk_ref[0] = p[:, 768:1024].astype(BF16)
    nv_ref[0] = p[:, 1024:1280].astype(BF16)

    z_ref[0] = _dot(p[:, 1280:1536].astype(BF16), dc_ref[...]).astype(BF16)

    cy_ref[0] = p[:, 1536:1792] * _sigmoid(p[:, 1792:2048])


def _inproj_call(x, sh, sc, g1, win, qn, wq, kvn, wkv, wvt, dc, cs, layer):
    b, l, d = x.shape
    t = min(l, INPROJ_TILE)
    tok = lambda w: pl.BlockSpec((1, t, w), lambda bi, i: (bi, i, 0))
    vec = pl.BlockSpec((1, 1, d), lambda bi, i: (bi, 0, 0))
    outs = [MLA_W, (MLA_W, BF16), N_HEADS * VT_ROWS, (GROUP_W, BF16), (GROUP_W, BF16),
            (GROUP_W, BF16), (2 * GROUP_W, BF16), (GROUP_W, F32)]
    transposed = lambda rows: pl.BlockSpec((1, rows, t), lambda bi, i: (bi, 0, i))
    out_specs = [tok(o[0]) if isinstance(o, tuple) else transposed(o) for o in outs]
    out_shape = [jax.ShapeDtypeStruct((b, l, o[0]), o[1]) if isinstance(o, tuple)
                 else jax.ShapeDtypeStruct((b, o, l), BF16) for o in outs]
    return pl.pallas_call(
        _inproj_kernel,
        grid=(b, l // t),
        in_specs=[tok(d), vec, vec, _const_spec(g1.shape),
                  pl.BlockSpec((1,) + win.shape[1:], lambda bi, i: (layer, 0, 0),
                               pipeline_mode=pl.Buffered(1)),
                  _const_spec(qn.shape), _const_spec(wq.shape), _const_spec(kvn.shape),
                  _const_spec(wkv.shape), _const_spec(wvt.shape), _const_spec(dc.shape),
                  pl.BlockSpec((t, 2 * HEAD_SLOT), lambda bi, i: (i, 0)),
                  pl.BlockSpec((2 * HEAD_SLOT, t), lambda bi, i: (0, i))],
        out_specs=out_specs,
        out_shape=out_shape,
        compiler_params=_cparams(2),
        name="inproj",
    )(x, sh, sc, g1, win, qn, wq, kvn, wkv, wvt, dc, cs, cs.T)


def _flash_kernel(*refs, n_chunks, tk, has_extra):
    if has_extra:
        q_ref, k_ref, vt_ref, k2_ref, vt2_ref, o_ref, sa_ref, sb_ref, sx_ref = refs
    else:
        q_ref, k_ref, vt_ref, o_ref, sa_ref, sb_ref = refs
    tq = q_ref.shape[2]

    def head_lanes(hh):
        return slice(hh * HEAD_SLOT, (hh + 1) * HEAD_SLOT)

    def head_scores(hh, kc, s_ref):
        st = _dot(kc, q_ref[0, head_lanes(hh), :])
        s_ref[...] = st
        return jnp.max(st, axis=0, keepdims=True)

    def first_scores(hh):
        return head_scores(hh, k_ref[0, 0:tk, head_lanes(hh)], sa_ref)

    def one_head(hh, mx_first, start_next):
        lanes = head_lanes(hh)
        vrows = slice(hh * VT_ROWS, (hh + 1) * VT_ROWS)
        scores = functools.partial(head_scores, hh)

        def accumulate(s_ref, mx, vtc, carry):
            m, (acc, l) = carry
            m_new = jnp.maximum(m, mx)
            p = jnp.exp2(s_ref[...] - m_new)
            alpha = jnp.exp2(m - m_new)
            l = alpha * l + jnp.sum(p, axis=0, keepdims=True)
            return m_new, (alpha * acc + _dot(vtc, p.astype(BF16)), l)

        def k_chunk(c):
            return k_ref[0, pl.ds(pl.multiple_of(c * tk, tk), tk), lanes]

        def vt_chunk(c):
            return vt_ref[0, vrows, pl.ds(pl.multiple_of(c * tk, tk), tk)]

        def pair(j, carry):
            mx_a, m, acc = carry
            mx_b = scores(k_chunk(2 * j + 1), sb_ref)
            m, acc = accumulate(sa_ref, mx_a, vt_chunk(2 * j), (m, acc))
            mx_a = scores(k_chunk(2 * j + 2), sa_ref)
            m, acc = accumulate(sb_ref, mx_b, vt_chunk(2 * j + 1), (m, acc))
            return mx_a, m, acc

        n_pairs = (n_chunks - 1) // 2
        carry = (mx_first, jnp.full((1, tq), -jnp.inf, F32),
                 (jnp.zeros((VT_ROWS, tq), F32), jnp.zeros((1, tq), F32)))
        mx, m, acc = lax.fori_loop(0, n_pairs, pair, carry)
        pending = (sa_ref, mx, vt_chunk(2 * n_pairs))
        tail = [(k_chunk(c), vt_chunk(c), sb_ref if c % 2 else sa_ref)
                for c in range(2 * n_pairs + 1, n_chunks)]
        if has_extra:
            tail.append((k2_ref[0, :, lanes], vt2_ref[0, vrows, :], sx_ref))
        for kc, vtc, s_ref in tail:
            mx_next = scores(kc, s_ref)
            m, acc = accumulate(*pending, (m, acc))
            pending = (s_ref, mx_next, vtc)
        early = start_next is not None and pending[0] is not sa_ref
        mx_next_head = start_next() if early else None
        m, (acc, l) = accumulate(*pending, (m, acc))
        if start_next is not None and not early:
            mx_next_head = start_next()
        return acc / l, mx_next_head

    outs = []
    mx_first = first_scores(0)
    for hh in range(ATTN_HEADS_PER_STEP):
        nxt = functools.partial(first_scores, hh + 1) if hh + 1 < ATTN_HEADS_PER_STEP else None
        o_h, mx_first = one_head(hh, mx_first, nxt)
        outs.append(o_h)
    o_ref[0] = jnp.concatenate(outs, axis=0).T.astype(BF16)


def _flash_call(qt, k, vt, k2=None, vt2=None):
    b, _, lq = qt.shape
    lk = k.shape[1]
    tq = min(lq, ATTN_TILE_Q)
    tk = min(lk, ATTN_TILE_K)
    has_extra = k2 is not None
    hps = ATTN_HEADS_PER_STEP
    qspec = pl.BlockSpec((1, hps * HEAD_SLOT, tq), lambda bi, h, i: (bi, h, i))
    kspec = lambda n: pl.BlockSpec((1, n, hps * HEAD_SLOT), lambda bi, h, i: (bi, 0, h))
    vspec = lambda n: pl.BlockSpec((1, hps * VT_ROWS, n), lambda bi, h, i: (bi, h, 0))
    ospec = pl.BlockSpec((1, tq, hps * MLA_V), lambda bi, h, i: (bi, i, h))
    in_specs = [qspec, kspec(lk), vspec(lk)]
    args = [qt, k, vt]
    scratch = [pltpu.VMEM((tk, tq), F32), pltpu.VMEM((tk, tq), F32)]
    if has_extra:
        in_specs += [kspec(k2.shape[1]), vspec(k2.shape[1])]
        args += [k2, vt2]
        scratch.append(pltpu.VMEM((k2.shape[1], tq), F32))
    return pl.pallas_call(
        functools.partial(_flash_kernel, n_chunks=lk // tk, tk=tk, has_extra=has_extra),
        grid=(b, N_HEADS // hps, lq // tq),
        in_specs=in_specs,
        out_specs=ospec,
        out_shape=jax.ShapeDtypeStruct((b, lq, N_HEADS * MLA_V), BF16),
        scratch_shapes=scratch,
        compiler_params=_cparams(3),
        name="mla_attn",
    )(*args)


def _head_stack(q):
    lane = lax.broadcasted_iota(jnp.int32, q.shape, 1)
    return jnp.concatenate(
        [jnp.where(lane // NA_HD == h, q, jnp.zeros_like(q)) for h in range(N_HEADS)], axis=0)


def _head_unstack(o, n):
    lane = lax.broadcasted_iota(jnp.int32, (n, GROUP_W), 1)
    out = jnp.zeros((n, GROUP_W), F32)
    for h in range(N_HEADS):
        out = out + jnp.where(lane // NA_HD == h, o[h * n:(h + 1) * n], 0.0)
    return out


def _natten_kernel(q_ref, k_ref, v_ref, kc_ref, vc_ref, bias_ref, o_ref, *, rows_per_step, rows):
    blk = pl.program_id(1)
    kc = kc_ref[0]
    vc = vc_ref[0]
    n_loc = NA_WIN_H * GRID_W

    def body(j, _):
        r = blk * rows_per_step + j
        rs = jnp.clip(r - NA_WIN_H // 2, 0, rows - NA_WIN_H)
        d0 = rs - r + (NA_WIN_H - 1)
        qs = _head_stack(q_ref[0, pl.ds(pl.multiple_of(j * GRID_W, GRID_W), GRID_W), :])
        kstart = pl.multiple_of(rs * GRID_W, GRID_W)
        s_loc = _dot_nt(qs, k_ref[0, pl.ds(kstart, n_loc), :])
        bias = jnp.concatenate(
            [jnp.concatenate([bias_ref[h, d0 + 2 * w] for w in range(NA_WIN_H // 2)], axis=-1)
             for h in range(N_HEADS)], axis=0)
        s_loc = s_loc + bias
        s_ctx = _dot_nt(qs, kc)
        m = jnp.maximum(jnp.max(s_loc, axis=-1, keepdims=True),
                        jnp.max(s_ctx, axis=-1, keepdims=True))
        p_loc = jnp.exp2(s_loc - m)
        p_ctx = jnp.exp2(s_ctx - m)
        denom = jnp.sum(p_loc, axis=-1, keepdims=True) + jnp.sum(p_ctx, axis=-1, keepdims=True)
        o = (_dot(p_loc.astype(BF16), v_ref[0, pl.ds(kstart, n_loc), :])
             + _dot(p_ctx.astype(BF16), vc))
        o = _head_unstack(o / denom, GRID_W)
        o_ref[0, pl.ds(pl.multiple_of(j * GRID_W, GRID_W), GRID_W), :] = o.astype(BF16)
        return 0

    lax.fori_loop(0, rows_per_step, body, 0, unroll=True)


def _natten_call(q, k, v, kc, vc, bias):
    b, l, w = q.shape
    rows = l // GRID_W
    rows_per_step = NATTEN_ROWS
    t = rows_per_step * GRID_W
    full = lambda n: pl.BlockSpec((1, n, w), lambda bi, i: (bi, 0, 0))
    return pl.pallas_call(
        functools.partial(_natten_kernel, rows_per_step=rows_per_step, rows=rows),
        grid=(b, rows // rows_per_step),
        in_specs=[pl.BlockSpec((1, t, w), lambda bi, i: (bi, i, 0)),
                  full(l), full(l), full(kc.shape[1]), full(kc.shape[1]),
                  _const_spec(bias.shape)],
        out_specs=pl.BlockSpec((1, t, w), lambda bi, i: (bi, i, 0)),
        out_shape=jax.ShapeDtypeStruct((b, l, w), BF16),
        compiler_params=_cparams(2),
        name="natten",
    )(q, k, v, kc, vc, bias)


def _na_ctx_kernel(q_ref, k_ref, v_ref, o_ref):
    n = q_ref.shape[1]
    s = _dot_nt(_head_stack(q_ref[0]), k_ref[0])
    m = jnp.max(s, axis=-1, keepdims=True)
    p = jnp.exp2(s - m)
    denom = jnp.sum(p, axis=-1, keepdims=True)
    o = _dot(p.astype(BF16), v_ref[0]) / denom
    o_ref[0] = _head_unstack(o, n).astype(BF16)


def _na_ctx_call(q, k, v):
    b, n, w = q.shape
    spec = pl.BlockSpec((1, n, w), lambda bi: (bi, 0, 0))
    return pl.pallas_call(
        _na_ctx_kernel, grid=(b,), in_specs=[spec, spec, spec], out_specs=spec,
        out_shape=jax.ShapeDtypeStruct((b, n, w), BF16),
        compiler_params=_cparams(1), name="na_ctx",
    )(q, k, v)


def _fnet1_kernel(z_ref, w_ref, y_ref, *, n_inner):
    zb = z_ref[0]
    p = _dot(w_ref[...], zb.reshape(zb.shape[0], zb.shape[1] * zb.shape[2]))
    n1 = p.shape[0] // 2
    re, im = [], []
    for j in range(n_inner):
        zr_c = p[:n1, j * 512:j * 512 + 256]
        zi_c = p[:n1, j * 512 + 256:(j + 1) * 512]
        zr_s = p[n1:, j * 512:j * 512 + 256]
        zi_s = p[n1:, j * 512 + 256:(j + 1) * 512]
        re.append(zr_c + zi_s)
        im.append(zi_c - zr_s)
    y_ref[0, 0] = jnp.concatenate(re, axis=-1).astype(BF16).reshape(n1, n_inner, GROUP_W)
    y_ref[0, 1] = jnp.concatenate(im, axis=-1).astype(BF16).reshape(n1, n_inner, GROUP_W)


def _fnet1_call(z2, w1s, n1):
    b = z2.shape[0]
    n_inner = FNET_BLOCK
    return pl.pallas_call(
        functools.partial(_fnet1_kernel, n_inner=n_inner),
        grid=(b, DFT_N2 // n_inner),
        in_specs=[pl.BlockSpec((1, n1, n_inner, 512), lambda bi, i: (bi, 0, i, 0)),
                  _const_spec(w1s.shape)],
        out_specs=pl.BlockSpec((1, 2, n1, n_inner, GROUP_W), lambda bi, i: (bi, 0, 0, i, 0)),
        out_shape=jax.ShapeDtypeStruct((b, 2, n1, DFT_N2, GROUP_W), BF16),
        compiler_params=_cparams(2),
        name="fnet_stage1",
    )(z2, w1s)


def _fnet2_kernel(y_ref, f_ref, o_ref, *, n_inner, norm):
    outs = []
    for j in range(n_inner):
        f = f_ref[j]
        o = _dot(f[:, :DFT_N2], y_ref[0, 0, j]) + _dot(f[:, DFT_N2:], y_ref[0, 1, j])
        outs.append((o * norm).astype(BF16))
    o_ref[0] = jnp.concatenate(outs, axis=-1).reshape(DFT_N2, n_inner, GROUP_W)


def _fnet2_call(y5, ftab, n1, norm):
    b = y5.shape[0]
    n_inner = FNET_BLOCK
    return pl.pallas_call(
        functools.partial(_fnet2_kernel, n_inner=n_inner, norm=norm),
        grid=(b, n1 // n_inner),
        in_specs=[pl.BlockSpec((1, 2, n_inner, DFT_N2, GROUP_W), lambda bi, i: (bi, 0, i, 0, 0)),
                  pl.BlockSpec((n_inner, DFT_N2, 2 * DFT_N2), lambda bi, i: (i, 0, 0))],
        out_specs=pl.BlockSpec((1, DFT_N2, n_inner, GROUP_W), lambda bi, i: (bi, 0, i, 0)),
        out_shape=jax.ShapeDtypeStruct((b, DFT_N2, n1, GROUP_W), BF16),
        compiler_params=_cparams(2),
        name="fnet_stage2",
    )(y5, ftab)


def _fnet_fused_kernel(z_ref, w_ref, f_ref, o_ref, y_ref, *, n1, norm):
    blk = FNET_FUSED_BLOCK

    def stage1(i, _):
        l0 = pl.multiple_of(i * blk, blk)
        zb = z_ref[0, :, pl.ds(l0, blk), :]
        p = _dot(w_ref[...], zb.reshape(n1, blk * 2 * GROUP_W))
        re, im = [], []
        for j in range(blk):
            zr_c = p[:n1, j * 512:j * 512 + 256]
            zi_c = p[:n1, j * 512 + 256:(j + 1) * 512]
            zr_s = p[n1:, j * 512:j * 512 + 256]
            zi_s = p[n1:, j * 512 + 256:(j + 1) * 512]
            re.append(zr_c + zi_s)
            im.append(zi_c - zr_s)
        y_ref[0, :, pl.ds(l0, blk), :] = jnp.concatenate(re, axis=-1).astype(BF16).reshape(n1, blk, GROUP_W)
        y_ref[1, :, pl.ds(l0, blk), :] = jnp.concatenate(im, axis=-1).astype(BF16).reshape(n1, blk, GROUP_W)
        return 0

    lax.fori_loop(0, DFT_N2 // blk, stage1, 0)

    def stage2(i, _):
        k0 = pl.multiple_of(i * blk, blk)
        outs = []
        for j in range(blk):
            f = f_ref[k0 + j]
            o = _dot(f[:, :DFT_N2], y_ref[0, k0 + j]) + _dot(f[:, DFT_N2:], y_ref[1, k0 + j])
            outs.append((o * norm).astype(BF16))
        o_ref[0, :, pl.ds(k0, blk), :] = jnp.concatenate(outs, axis=-1).reshape(DFT_N2, blk, GROUP_W)
        return 0

    lax.fori_loop(0, n1 // blk, stage2, 0)


def _fnet_fused_call(z4, w1s, ftab, n1, norm):
    b = z4.shape[0]
    return pl.pallas_call(
        functools.partial(_fnet_fused_kernel, n1=n1, norm=norm),
        grid=(b,),
        in_specs=[pl.BlockSpec((1, n1, DFT_N2, 2 * GROUP_W), lambda bi: (bi, 0, 0, 0)),
                  _const_spec(w1s.shape), _const_spec(ftab.shape)],
        out_specs=pl.BlockSpec((1, DFT_N2, n1, GROUP_W), lambda bi: (bi, 0, 0, 0)),
        out_shape=jax.ShapeDtypeStruct((b, DFT_N2, n1, GROUP_W), BF16),
        scratch_shapes=[pltpu.VMEM((2, n1, DFT_N2, GROUP_W), BF16)],
        compiler_params=_cparams(1),
        name="fnet_fused",
    )(z4, w1s, ftab)


def _dft_small_kernel(z_ref, f_ref, o_ref, *, norm):
    z = z_ref[0]
    f = f_ref[...]
    n = z.shape[0]
    o = _dot(f[:, :n], z[:, :GROUP_W]) + _dot(f[:, n:], z[:, GROUP_W:])
    o_ref[0] = (o * norm).astype(BF16)


def _dft_small_call(z, ftab, norm):
    b, n, _ = z.shape
    return pl.pallas_call(
        functools.partial(_dft_small_kernel, norm=norm),
        grid=(b,),
        in_specs=[pl.BlockSpec((1, n, 2 * GROUP_W), lambda bi: (bi, 0, 0)), _const_spec(ftab.shape)],
        out_specs=pl.BlockSpec((1, n, GROUP_W), lambda bi: (bi, 0, 0)),
        out_shape=jax.ShapeDtypeStruct((b, n, GROUP_W), BF16),
        compiler_params=_cparams(1), name="fnet_ctx",
    )(z, ftab)


def _fourier_tables(l):
    if l <= 256:
        m = (np.arange(l)[:, None] * np.arange(l)[None, :]) % l
        ang = 2.0 * np.pi * m / l
        return None, np.concatenate([np.cos(ang), np.sin(ang)], axis=1).astype(np.float32), 0
    n1 = l // DFT_N2
    m1 = (np.arange(n1)[:, None] * np.arange(n1)[None, :]) % n1
    a1 = 2.0 * np.pi * m1 / n1
    w1s = np.concatenate([np.cos(a1), np.sin(a1)], axis=0).astype(np.float32)
    kk = np.arange(n1)[:, None, None] + n1 * np.arange(DFT_N2)[None, :, None]
    m2 = (kk * np.arange(DFT_N2)[None, None, :]) % l
    a2 = 2.0 * np.pi * m2 / l
    ftab = np.concatenate([np.cos(a2), np.sin(a2)], axis=2).astype(np.float32)
    return w1s, ftab, n1


def _fourier_mix(z):
    b, l, _ = z.shape
    norm = float((l * FN_GW) ** -0.5)
    w1s, ftab, n1 = _fourier_tables(l)
    if w1s is None:
        return _dft_small_call(z, jnp.asarray(ftab).astype(BF16), norm)
    o = _fnet_fused_call(z.reshape(b, n1, DFT_N2, 2 * GROUP_W), jnp.asarray(w1s).astype(BF16),
                         jnp.asarray(ftab).astype(BF16), n1, norm)
    return o.reshape(b, l, GROUP_W)


def _conv_kernel(prev_ref, cur_ref, next_ref, w_ref, b_ref, g_ref, beta_ref, o_ref, buf_ref):
    i = pl.program_id(1)
    n = pl.num_programs(1)
    t = cur_ref.shape[1]
    buf_ref[0:CV_HALO, :] = jnp.where(i > 0, prev_ref[0], 0.0)
    buf_ref[CV_HALO:CV_HALO + t, :] = cur_ref[0]
    buf_ref[CV_HALO + t:, :] = jnp.where(i < n - 1, next_ref[0], 0.0)
    w = w_ref[...]
    first = CV_HALO - CV_K // 2
    acc = None
    for res in range(SUBLANES):
        z = None
        for base in range(0, first + CV_K, SUBLANES):
            j = base + res - first
            if 0 <= j < CV_K:
                term = w[j:j + 1, :] * buf_ref[base:base + t + SUBLANES, :]
                z = term if z is None else z + term
        z = z[res:res + t, :]
        acc = z if acc is None else acc + z
    y = acc + b_ref[...]
    mu = jnp.mean(y, axis=-1, keepdims=True)
    var = jnp.mean(jnp.square(y - mu), axis=-1, keepdims=True)
    y = (y - mu) * lax.rsqrt(var + EPS) * g_ref[...] + beta_ref[...]
    o_ref[0] = (y * _sigmoid(y)).astype(BF16)


def _conv_call(cy, w_dw, b_dw, ln_g, ln_b):
    b, l, w = cy.shape
    t = min(l, CONV_TILE)
    hb = t // CV_HALO
    n_halo = l // CV_HALO
    return pl.pallas_call(
        _conv_kernel,
        grid=(b, l // t),
        in_specs=[pl.BlockSpec((1, CV_HALO, w), lambda bi, i: (bi, jnp.maximum(i * hb - 1, 0), 0)),
                  pl.BlockSpec((1, t, w), lambda bi, i: (bi, i, 0)),
                  pl.BlockSpec((1, CV_HALO, w),
                               lambda bi, i: (bi, jnp.minimum((i + 1) * hb, n_halo - 1), 0)),
                  _const_spec(w_dw.shape), _const_spec(b_dw.shape), _const_spec(ln_g.shape),
                  _const_spec(ln_b.shape)],
        out_specs=pl.BlockSpec((1, t, w), lambda bi, i: (bi, i, 0)),
        out_shape=jax.ShapeDtypeStruct((b, l, w), BF16),
        scratch_shapes=[pltpu.VMEM((t + 2 * CV_HALO, w), F32)],
        compiler_params=_cparams(2),
        name="conformer_conv",
    )(cy, cy, cy, w_dw, b_dw, ln_g, ln_b)


def _outmlp_kernel(x_ref, oa_ref, oc_ref, ob_ref, od_ref, wo_ref, gt1_ref, sh2_ref, sc2_ref,
                   gt2_ref, g2_ref, w1_ref, w2_ref, fg_ref, o_ref, *, last, ff_chunk):
    y = (_dot(oa_ref[0], wo_ref[0:GROUP_W, :])
         + _dot(oc_ref[0], wo_ref[GROUP_W:2 * GROUP_W, :])
         + _dot(ob_ref[0], wo_ref[2 * GROUP_W:3 * GROUP_W, :])
         + _dot(od_ref[0], wo_ref[3 * GROUP_W:, :]))
    x1 = x_ref[0] + gt1_ref[0] * y
    h = (_rms(x1, g2_ref[...]) * (1.0 + sc2_ref[0]) + sh2_ref[0]).astype(BF16)
    ff = jnp.zeros(x1.shape, F32)
    for c in range(D_FF // ff_chunk):
        u = jnp.maximum(_dot(h, w1_ref[0, :, c * ff_chunk:(c + 1) * ff_chunk]), 0.0)
        ff = ff + _dot((u * u).astype(BF16), w2_ref[0, c * ff_chunk:(c + 1) * ff_chunk, :])
    x2 = x1 + gt2_ref[0] * ff
    if last:
        x2 = _rms(x2, fg_ref[...])
    o_ref[0] = x2


def _outmlp_call(x, o_mla, o_na, o_fn, o_cv, wo, gt1, sh2, sc2, gt2, g2, w1, w2, fg, last, layer):
    b, l, d = x.shape
    t = min(l, TOKEN_TILE)
    tok = lambda w: pl.BlockSpec((1, t, w), lambda bi, i: (bi, i, 0))
    vec = pl.BlockSpec((1, 1, d), lambda bi, i: (bi, 0, 0))
    slab = lambda w: pl.BlockSpec((1,) + w.shape[1:], lambda bi, i: (layer, 0, 0),
                                  pipeline_mode=pl.Buffered(1))
    return pl.pallas_call(
        functools.partial(_outmlp_kernel, last=last, ff_chunk=FF_CHUNK),
        grid=(b, l // t),
        in_specs=[tok(d), tok(GROUP_W), tok(GROUP_W), tok(GROUP_W), tok(GROUP_W),
                  _const_spec(wo.shape), vec, vec, vec, vec, _const_spec(g2.shape),
                  slab(w1), slab(w2), _const_spec(fg.shape)],
        out_specs=tok(d),
        out_shape=jax.ShapeDtypeStruct((b, l, d), F32),
        compiler_params=_cparams(2),
        name="outproj_mlp",
    )(x, o_mla, o_na, o_fn, o_cv, wo, gt1, sh2, sc2, gt2, g2, w1, w2, fg)


def _rope_tables(l, with_rope):
    per_axis = MLA_ROPE // 2
    n_freq = per_axis // 2

    def slot(nope, row_part, col_part, sign):
        n = row_part.shape[0]
        return jnp.concatenate([jnp.full((n, MLA_NOPE), nope, F32), sign * row_part, sign * col_part,
                                row_part, col_part, jnp.zeros((n, HEAD_SLOT - MLA_QK), F32)], axis=-1)

    if not with_rope:
        one, zero = jnp.ones((l, n_freq), F32), jnp.zeros((l, n_freq), F32)
        return jnp.concatenate([slot(1.0, one, one, 1.0), slot(0.0, zero, zero, -1.0)], axis=-1)
    inv = ROPE_BASE ** (-jnp.arange(0, per_axis, 2, dtype=F32) / per_axis)
    r_ang = jnp.arange(l // GRID_W).astype(F32)[:, None] * inv
    c_ang = jnp.arange(GRID_W).astype(F32)[:, None] * inv
    zr, zc = jnp.zeros_like(r_ang), jnp.zeros_like(c_ang)
    by_row = jnp.concatenate([slot(0.0, jnp.cos(r_ang), zr, 1.0), slot(0.0, jnp.sin(r_ang), zr, -1.0)], axis=-1)
    by_col = jnp.concatenate([slot(1.0, zc, jnp.cos(c_ang), 1.0), slot(0.0, zc, jnp.sin(c_ang), -1.0)], axis=-1)
    return (by_row[:, None, :] + by_col[None, :, :]).reshape(l, 2 * HEAD_SLOT)


def _swap_halves(w):
    half = w.shape[-1] // 2
    return jnp.concatenate([w[..., half:], w[..., :half]], axis=-1)


def _pad_w_in(w_in):
    k_r_end = 256 + 128 + MLA_ROPE
    pad = jnp.zeros(w_in.shape[:-1] + (HEAD_SLOT - MLA_ROPE,), F32)
    return jnp.concatenate([w_in[..., :k_r_end], pad, w_in[..., k_r_end:]], axis=-1).astype(BF16)


def _layer_weights(w_uq, w_ukv, w_out, rpb):
    rq = w_uq.shape[0]
    w3 = w_uq.reshape(rq, N_HEADS, MLA_QK)
    zpad = jnp.zeros((rq, N_HEADS, HEAD_SLOT - MLA_QK), F32)
    plain = jnp.concatenate([w3, zpad], axis=-1)
    swapped = jnp.concatenate([jnp.zeros((rq, N_HEADS, MLA_NOPE), F32),
                               _swap_halves(w3[..., MLA_NOPE:]), zpad], axis=-1)
    wq = jnp.concatenate([plain.reshape(rq, MLA_W), swapped.reshape(rq, MLA_W)], axis=1).T.astype(BF16)

    rkv = w_ukv.shape[0]
    u3 = w_ukv.reshape(rkv, N_HEADS, MLA_NOPE + MLA_V)
    eye = jnp.eye(MLA_ROPE, dtype=F32)
    slot_pad_r = jnp.zeros((MLA_ROPE, N_HEADS, HEAD_SLOT - MLA_QK), F32)

    def slots(top, rope_block):
        top = jnp.concatenate([top, jnp.zeros((rkv, N_HEADS, HEAD_SLOT - top.shape[-1]), F32)], axis=-1)
        mid = jnp.concatenate([jnp.zeros((MLA_ROPE, N_HEADS, MLA_NOPE), F32),
                               jnp.broadcast_to(rope_block[:, None, :], (MLA_ROPE, N_HEADS, MLA_ROPE)),
                               slot_pad_r], axis=-1)
        bot = jnp.zeros((2 * HEAD_SLOT - rkv - MLA_ROPE, N_HEADS, HEAD_SLOT), F32)
        return jnp.concatenate([top, mid, bot], axis=0).reshape(2 * HEAD_SLOT, MLA_W)

    k_plain = slots(u3[..., :MLA_NOPE], eye)
    k_swap = slots(jnp.zeros((rkv, N_HEADS, MLA_NOPE), F32), _swap_halves(eye))
    wkv = jnp.concatenate([k_plain, k_swap], axis=1).astype(BF16)

    vt3 = jnp.transpose(u3[..., MLA_NOPE:], (1, 2, 0))
    vt3 = jnp.pad(vt3, ((0, 0), (0, VT_ROWS - MLA_V), (0, 2 * HEAD_SLOT - rkv)))
    wvt = vt3.reshape(N_HEADS * VT_ROWS, 2 * HEAD_SLOT).astype(BF16)

    wo = w_out.astype(BF16)

    qc = np.arange(GRID_W)[:, None]
    kc = np.arange(GRID_W)[None, :]
    ws = np.clip(qc - NA_WIN_W // 2, 0, GRID_W - NA_WIN_W)
    in_win = (kc >= ws) & (kc < ws + NA_WIN_W)
    edge = GRID_W - NA_WIN_W
    n = 2 * GRID_W - 1
    rp = jnp.pad(rpb.astype(F32), ((0, 0), (0, 0), (edge, edge)))
    tiled = jnp.tile(rp, (1, 1, GRID_W + 1))[:, :, :GRID_W * (n + 1)]
    toep = tiled.reshape(rp.shape[0], rp.shape[1], GRID_W, n + 1)[:, :, ::-1, :GRID_W]
    t2 = jnp.where(in_win[None, None], toep * LOG2_E, NEG_INF)
    bias = jnp.concatenate([t2[:, :-1], t2[:, 1:]], axis=-1)
    return wq, wkv, wvt, wo, bias


def _channel_dft():
    m = (np.arange(FN_GW)[:, None] * np.arange(FN_GW)[None, :]) % FN_GW
    ang = 2.0 * np.pi * m / FN_GW
    eye = np.eye(FN_GROUPS)
    return np.concatenate([np.kron(eye, np.cos(ang)), -np.kron(eye, np.sin(ang))],
                          axis=1).astype(np.float32)


def kernel(x, c, ctx, c_ctx, w_mod, b_mod, norm1_g, norm2_g, w_in, mla_q_norm, mla_w_uq, mla_kv_norm, mla_w_ukv, na_rpb, cv_w_dw, cv_b_dw, cv_ln_g, cv_ln_b, w_out, w_ff1, w_ff2, final_g):
    depth = w_mod.shape[0]
    b, s, d = x.shape
    n_ctx = ctx.shape[1]

    cc = jnp.concatenate([c, c_ctx[None, :], jnp.zeros((8 - b - 1, d), F32)], axis=0)
    mods = _mod_call(cc, w_mod, b_mod)

    cs_lat = _rope_tables(s, True)
    cs_ctx = _rope_tables(n_ctx, False)
    dc = jnp.asarray(_channel_dft()).astype(BF16)
    row = lambda p: p.reshape(1, -1)
    fg = row(final_g)
    w1 = w_ff1.astype(BF16)
    w2 = w_ff2.astype(BF16)
    win = _pad_w_in(w_in)

    xc = ctx
    for i in range(depth):
        last = i == depth - 1
        wq, wkv, wvt, wo, bias = _layer_weights(mla_w_uq[i], mla_w_ukv[i], w_out[i], na_rpb[i])
        mx =[m.reshape(b, 1, d) for m in jnp.split(mods[i, :b], 6, axis=-1)]
        mc = [jnp.broadcast_to(m.reshape(1, 1, d), (b, 1, d))
              for m in jnp.split(mods[i, b:b + 1], 6, axis=-1)]
        proj = functools.partial(_inproj_call, g1=row(norm1_g[i]), win=win, qn=row(mla_q_norm[i]),
                                 wq=wq, kvn=row(mla_kv_norm[i]), wkv=wkv, wvt=wvt, dc=dc, layer=i)
        conv = functools.partial(_conv_call, w_dw=cv_w_dw[i], b_dw=row(cv_b_dw[i]),
                                 ln_g=row(cv_ln_g[i]), ln_b=row(cv_ln_b[i]))
        mlp = functools.partial(_outmlp_call, wo=wo, g2=row(norm2_g[i]), w1=w1, w2=w2, fg=fg, layer=i)

        qt, k, vt, nq, nk, nv, z, cy = proj(x, mx[0], mx[1], cs=cs_lat)
        cqt, ck, cvt, cnq, cnk, cnv, cz, ccy = proj(xc, mc[0], mc[1], cs=cs_ctx)

        o_mla = _flash_call(qt, k, vt, ck, cvt)
        o_na = _natten_call(nq, nk, nv, cnk, cnv, bias)
        o_fn = _fourier_mix(z)
        o_cv = conv(cy)
        x = mlp(x, o_mla, o_na, o_fn, o_cv, gt1=mx[2], sh2=mx[3], sc2=mx[4], gt2=mx[5], last=last)

        if not last:
            co_mla = _flash_call(cqt, ck, cvt)
            co_na = _na_ctx_call(cnq, cnk, cnv)
            co_fn = _fourier_mix(cz)
            co_cv = conv(ccy)
            xc = mlp(xc, co_mla, co_na, co_fn, co_cv, gt1=mc[2], sh2=mc[3], sc2=mc[4], gt2=mc[5],
                     last=False)
    return x
```
